```python
import math
import jax, jax.numpy as jnp
from jax import lax
import numpy as np

D_MODEL = 1024
BATCH = 16
SEQ = 2048
DEPTH = 1
DEC_BATCH = 32
DEC_SEQ = 64
PAST_LEN = 1024

CHUNK = 64
QBLK = 128
N_HEADS = 8
N_KV_HEADS = 2
HEAD_DIM = 64
ATTN_WIDTH = N_HEADS * HEAD_DIM
KV_WIDTH = N_KV_HEADS * HEAD_DIM
ROT_DIM = HEAD_DIM // 4
ROPE_THETA = 500000.0
N_IDX_HEADS = 8
IDX_DIM = 64
TOPK_MAX = 256
SSM_WIDTH = D_MODEL // 2
SSM_GROUP = 16
N_SSM_GROUPS = SSM_WIDTH // SSM_GROUP
SSM_STATE = 64
NORM_EPS = 1e-6
SPLITS = (ATTN_WIDTH, KV_WIDTH, KV_WIDTH, N_IDX_HEADS * IDX_DIM, IDX_DIM, N_IDX_HEADS,
          ATTN_WIDTH, SSM_WIDTH, SSM_WIDTH, D_MODEL, D_MODEL)
IN_WIDTH = (2 * ATTN_WIDTH + 2 * KV_WIDTH + N_IDX_HEADS * IDX_DIM + IDX_DIM + N_IDX_HEADS
            + 2 * SSM_WIDTH + 2 * D_MODEL)

kernel_name = 'hybrid_dsa_s5_streaming_step'

F32 = jnp.float32


def rms_norm(x, g):
    xf = x.astype(F32)
    y = xf * lax.rsqrt(jnp.mean(xf * xf, axis=-1, keepdims=True) + NORM_EPS)
    return (y * g.astype(F32)).astype(x.dtype)


def split_cols(z):
    out, off = [], 0
    for w in SPLITS:
        out.append(z[..., off:off + w])
        off += w
    return out


def partial_rope(x, pos):
    half = ROT_DIM // 2
    inv = jnp.power(ROPE_THETA, -jnp.arange(half, dtype=F32) * (2.0 / ROT_DIM))
    ang = pos.astype(F32)[:, None] * inv[None, :]
    ang = ang.reshape((ang.shape[0],) + (1,) * (x.ndim - 3) + (half,))
    cos, sin = jnp.cos(ang), jnp.sin(ang)
    xf = x.astype(F32)
    x1, x2, rest = xf[..., :half], xf[..., half:ROT_DIM], xf[..., ROT_DIM:]
    out = jnp.concatenate([x1 * cos - x2 * sin, x2 * cos + x1 * sin, rest], axis=-1)
    return out.astype(x.dtype)


def dsa_attend(q, qi, wi, qpos, k_all, v_all, ki_all, kpos, n_sel):
    B, T = q.shape[0], q.shape[1]
    qb = QBLK if T % QBLK == 0 else T
    nblk = T // qb
    gather = jax.vmap(lambda a, i: a[i])
    kif = ki_all.astype(F32)
    kchunk = kpos // CHUNK

    def to_blocks(a):
        return jnp.swapaxes(a.reshape((B, nblk, qb) + a.shape[2:]), 0, 1)

    def one_block(blk):
        qb_, qib, wib, pb = blk
        s = jnp.einsum('bqhd,bsd->bqhs', qib.astype(F32), kif) * (IDX_DIM ** -0.5)
        score = jnp.einsum('bqh,bqhs->bqs', wib.astype(F32), jax.nn.relu(s))
        adm = kchunk[None, :] <= (pb // CHUNK)[:, None]
        score = jnp.where(adm[None], score, -jnp.inf)
        top_val, top_idx = lax.top_k(score, n_sel)
        valid = jnp.isfinite(top_val)
        k_sel = gather(k_all, top_idx).astype(F32)
        v_sel = gather(v_all, top_idx).astype(F32)
        qg = qb_.reshape(B, qb, N_KV_HEADS, N_HEADS // N_KV_HEADS, HEAD_DIM).astype(F32)
        logits = jnp.einsum('bqgrd,bqngd->bqgrn', qg, k_sel) * (HEAD_DIM ** -0.5)
        logits = jnp.where(valid[:, :, None, None, :], logits, -jnp.inf)
        p = jax.nn.softmax(logits, axis=-1)
        o = jnp.einsum('bqgrn,bqngd->bqgrd', p, v_sel)
        return o.reshape(B, qb, ATTN_WIDTH).astype(q.dtype)

    out = lax.map(one_block, (to_blocks(q), to_blocks(qi), to_blocks(wi), qpos.reshape(nblk, qb)))
    return jnp.swapaxes(out, 0, 1).reshape(B, T, ATTN_WIDTH)


def s5_discretize(lambda_re, lambda_im, log_dt, b_re, b_im):
    dt = jnp.exp(log_dt.astype(F32))[:, None]
    lr, li = lambda_re.astype(F32), lambda_im.astype(F32)
    mag = jnp.exp(lr * dt)
    ar, ai = mag * jnp.cos(li * dt), mag * jnp.sin(li * dt)
    den = lr * lr + li * li
    zr = ((ar - 1.0) * lr + ai * li) / den
    zi = (ai * lr - (ar - 1.0) * li) / den
    br, bi = b_re.astype(F32), b_im.astype(F32)
    bbar_re = zr[..., None] * br - zi[..., None] * bi
    bbar_im = zr[..., None] * bi + zi[..., None] * br
    return ar, ai, bbar_re, bbar_im


def complex_linear_combine(e1, e2):
    a1r, a1i, b1r, b1i = e1
    a2r, a2i, b2r, b2i = e2
    return (a2r * a1r - a2i * a1i, a2r * a1i + a2i * a1r,
            a2r * b1r - a2i * b1i + b2r, a2r * b1i + a2i * b1r + b2i)


def s5_mix(u, h0, lambda_re, lambda_im, log_dt, b_re, b_im, c_re, c_im, d_skip):
    B, T = u.shape[0], u.shape[1]
    ar, ai, bbr, bbi = s5_discretize(lambda_re, lambda_im, log_dt, b_re, b_im)
    ug = u.astype(F32).reshape(B, T, N_SSM_GROUPS, SSM_GROUP)
    bu_re = jnp.einsum('btgc,gpc->btgp', ug, bbr)
    bu_im = jnp.einsum('btgc,gpc->btgp', ug, bbi)
    if h0 is not None:
        h0r, h0i = h0[0].astype(F32), h0[1].astype(F32)
        bu_re = bu_re.at[:, 0].add(ar * h0r - ai * h0i)
        bu_im = bu_im.at[:, 0].add(ar * h0i + ai * h0r)
    a_re = jnp.broadcast_to(ar, bu_re.shape)
    a_im = jnp.broadcast_to(ai, bu_im.shape)
    _, _, h_re, h_im = lax.associative_scan(complex_linear_combine, (a_re, a_im, bu_re, bu_im), axis=1)
    y = (jnp.einsum('gcp,btgp->btgc', c_re.astype(F32), h_re)
         - jnp.einsum('gcp,btgp->btgc', c_im.astype(F32), h_im))
    y = y.reshape(B, T, SSM_WIDTH) + d_skip.astype(F32) * u.astype(F32)
    return y, h_re[:, -1], h_im[:, -1]


def hybrid_layer(x, c, pos, past, prm):
    (w_mod, b_mod, g_norm, w_in, lambda_re, lambda_im, log_dt, ssm_b_re, ssm_b_im,
     ssm_c_re, ssm_c_im, d_skip, w_glu, w_attn_proj, w_ssm_proj, w_out) = prm
    B, T, _ = x.shape
    mod = jax.nn.silu(c) @ w_mod + b_mod
    shift, scale, gate = mod[:, :D_MODEL], mod[:, D_MODEL:2 * D_MODEL], mod[:, 2 * D_MODEL:]
    h = rms_norm(x, g_norm) * (1.0 + scale[:, None, :]) + shift[:, None, :]
    z = h @ w_in
    zq, zk, zv, zqi, zki, zwi, zga, zu, zgs, zma, zmb = split_cols(z)
    q = partial_rope(zq.reshape(B, T, N_HEADS, HEAD_DIM), pos)
    k = partial_rope(zk.reshape(B, T, N_KV_HEADS, HEAD_DIM), pos)
    v = zv.reshape(B, T, N_KV_HEADS, HEAD_DIM)
    qi = partial_rope(zqi.reshape(B, T, N_IDX_HEADS, IDX_DIM), pos)
    ki = partial_rope(zki, pos)
    wi = zwi * (N_IDX_HEADS ** -0.5)
    if past is None:
        k_all, v_all, ki_all, kpos, h0 = k, v, ki, pos, None
    else:
        ck, cv, cki, h0r, h0i = past
        k_all = jnp.concatenate([ck.astype(k.dtype), k], axis=1)
        v_all = jnp.concatenate([cv.astype(v.dtype), v], axis=1)
        ki_all = jnp.concatenate([cki.astype(ki.dtype), ki], axis=1)
        kpos = jnp.arange(k_all.shape[1], dtype=jnp.int32)
        h0 = (h0r, h0i)
    n_sel = min(TOPK_MAX, k_all.shape[1] // 4)
    o_attn = dsa_attend(q, qi, wi, pos, k_all, v_all, ki_all, kpos, n_sel)
    branch_a = (o_attn * jax.nn.silu(zga)) @ w_attn_proj
    y_ssm, h_re, h_im = s5_mix(zu, h0, lambda_re, lambda_im, log_dt, ssm_b_re, ssm_b_im,
                               ssm_c_re, ssm_c_im, d_skip)
    g_lin = jax.nn.gelu(y_ssm).astype(x.dtype) @ w_glu
    y_glu = g_lin[..., :SSM_WIDTH] * jax.nn.sigmoid(g_lin[..., SSM_WIDTH:])
    branch_b = (y_glu.astype(x.dtype) * jax.nn.silu(zgs)) @ w_ssm_proj
    merged = jax.nn.sigmoid(zma) * branch_a + jax.nn.sigmoid(zmb) * branch_b
    x = x + gate[:, None, :] * (merged @ w_out)
    return x, k, v, ki, h_re, h_im


def setup_inputs(seed: int = 0) -> dict:
    key = jax.random.key(seed)
    ks = jax.random.split(key, 32)

    def nrm(k, shape, s):
        return jax.random.normal(k, shape, F32) * s

    G, P, C = N_SSM_GROUPS, SSM_STATE, SSM_GROUP
    return {
        'x_prompt': nrm(ks[0], (BATCH, SEQ, D_MODEL), 1.0),
        'x_sample': nrm(ks[1], (DEC_BATCH, DEC_SEQ, D_MODEL), 1.0),
        'cache_k': nrm(ks[2], (DEPTH, DEC_BATCH, PAST_LEN, N_KV_HEADS, HEAD_DIM), 1.0),
        'cache_v': nrm(ks[3], (DEPTH, DEC_BATCH, PAST_LEN, N_KV_HEADS, HEAD_DIM), 1.0),
        'cache_idx_k': nrm(ks[4], (DEPTH, DEC_BATCH, PAST_LEN, IDX_DIM), 1.0),
        'state_ssm_re': nrm(ks[5], (DEPTH, DEC_BATCH, G, P), 0.1),
        'state_ssm_im': nrm(ks[6], (DEPTH, DEC_BATCH, G, P), 0.1),
        'c_prompt': nrm(ks[7], (BATCH, D_MODEL), 1.0),
        'c_sample': nrm(ks[8], (DEC_BATCH, D_MODEL), 1.0),
        'w_mod': nrm(ks[9], (DEPTH, D_MODEL, 3 * D_MODEL), 0.5 * D_MODEL ** -0.5),
        'b_mod': nrm(ks[10], (DEPTH, 3 * D_MODEL), 0.02),
        'g_norm': 1.0 + nrm(ks[11], (DEPTH, D_MODEL), 0.02),
        'w_in': nrm(ks[12], (DEPTH, D_MODEL, IN_WIDTH), D_MODEL ** -0.5),
        'lambda_re': -0.5 + nrm(ks[13], (DEPTH, G, P), 0.01),
        'lambda_im': math.pi * jnp.arange(P, dtype=F32) + nrm(ks[14], (DEPTH, G, P), 0.01),
        'log_dt': jax.random.uniform(ks[15], (DEPTH, G), F32, math.log(1e-3), math.log(1e-1)),
        'ssm_b_re': nrm(ks[16], (DEPTH, G, P, C), (2 * C) ** -0.5),
        'ssm_b_im': nrm(ks[17], (DEPTH, G, P, C), (2 * C) ** -0.5),
        'ssm_c_re': nrm(ks[18], (DEPTH, G, C, P), P ** -0.5),
        'ssm_c_im': nrm(ks[19], (DEPTH, G, C, P), P ** -0.5),
        'd_skip': nrm(ks[20], (DEPTH, SSM_WIDTH), 1.0),
        'w_glu': nrm(ks[21], (DEPTH, SSM_WIDTH, 2 * SSM_WIDTH), SSM_WIDTH ** -0.5),
        'w_attn_proj': nrm(ks[22], (DEPTH, ATTN_WIDTH, D_MODEL), ATTN_WIDTH ** -0.5),
        'w_ssm_proj': nrm(ks[23], (DEPTH, SSM_WIDTH, D_MODEL), SSM_WIDTH ** -0.5),
        'w_out': nrm(ks[24], (DEPTH, D_MODEL, D_MODEL), D_MODEL ** -0.5),
        'g_final': 1.0 + nrm(ks[25], (D_MODEL,), 0.02),
    }


def reference(x_prompt, x_sample, cache_k, cache_v, cache_idx_k, state_ssm_re, state_ssm_im,
              c_prompt, c_sample, w_mod, b_mod, g_norm, w_in, lambda_re, lambda_im, log_dt,
              ssm_b_re, ssm_b_im, ssm_c_re, ssm_c_im, d_skip, w_glu, w_attn_proj, w_ssm_proj,
              w_out, g_final):
    pos_p = jnp.arange(x_prompt.shape[1], dtype=jnp.int32)
    past_len = cache_k.shape[2]
    pos_s = past_len + jnp.arange(x_sample.shape[1], dtype=jnp.int32)
    xp, xs = x_prompt, x_sample
    kp_l, vp_l, kip_l, hrp_l, hip_l = [], [], [], [], []
    ks_l, vs_l, kis_l, hrs_l, his_l = [], [], [], [], []
    for l in range(DEPTH):
        prm = (w_mod[l], b_mod[l], g_norm[l], w_in[l], lambda_re[l], lambda_im[l], log_dt[l],
               ssm_b_re[l], ssm_b_im[l], ssm_c_re[l], ssm_c_im[l], d_skip[l], w_glu[l],
               w_attn_proj[l], w_ssm_proj[l], w_out[l])
        xp, kp, vp, kip, hrp, hip = hybrid_layer(xp, c_prompt, pos_p, None, prm)
        past = (cache_k[l], cache_v[l], cache_idx_k[l], state_ssm_re[l], state_ssm_im[l])
        xs, ks_, vs_, kis, hrs, his = hybrid_layer(xs, c_sample, pos_s, past, prm)
        kp_l.append(kp); vp_l.append(vp); kip_l.append(kip)
        hrp_l.append(hrp.astype(state_ssm_re.dtype)); hip_l.append(hip.astype(state_ssm_im.dtype))
        ks_l.append(ks_); vs_l.append(vs_); kis_l.append(kis)
        hrs_l.append(hrs.astype(state_ssm_re.dtype)); his_l.append(his.astype(state_ssm_im.dtype))
    y_prompt = rms_norm(xp, g_final)
    y_sample = rms_norm(xs, g_final)
    return (y_prompt, y_sample,
            jnp.stack(kp_l), jnp.stack(vp_l), jnp.stack(kip_l), jnp.stack(hrp_l), jnp.stack(hip_l),
            jnp.stack(ks_l), jnp.stack(vs_l), jnp.stack(kis_l), jnp.stack(hrs_l), jnp.stack(his_l))
```

```python
import functools
import math

import jax
import jax.numpy as jnp
from jax import lax
from jax.experimental import pallas as pl
from jax.experimental.pallas import tpu as pltpu

F32 = jnp.float32
BF16 = jnp.bfloat16

LANES = 128
VMEM_LIMIT = 56 * 1024 * 1024

D_MODEL = 1024
CHUNK = 64
QBLK = 128
N_HEADS = 8
N_KV_HEADS = 2
HEAD_DIM = 64
ATTN_WIDTH = N_HEADS * HEAD_DIM
KV_WIDTH = N_KV_HEADS * HEAD_DIM
ROT_DIM = HEAD_DIM // 4
ROPE_THETA = 500000.0
N_IDX_HEADS = 8
IDX_DIM = 64
IDX_WIDTH = N_IDX_HEADS * IDX_DIM
TOPK_MAX = 256
SSM_WIDTH = D_MODEL // 2
SSM_GROUP = 16
N_SSM_GROUPS = SSM_WIDTH // SSM_GROUP
SSM_STATE = 64
SSM_LANES = N_SSM_GROUPS * SSM_STATE
NORM_EPS = 1e-6

_KIWI_PAD = LANES - IDX_DIM - N_IDX_HEADS
_SEG_WIDTHS = (ATTN_WIDTH, KV_WIDTH, KV_WIDTH, IDX_WIDTH, LANES, ATTN_WIDTH, SSM_WIDTH, SSM_WIDTH,
               D_MODEL, D_MODEL)
_SEG_NAMES = ("q", "k", "v", "qi", "kiwi", "ga", "u", "gs", "ma", "mb")
_SEG_OFF = {}
_off = 0
for _n, _w in zip(_SEG_NAMES, _SEG_WIDTHS):
    _SEG_OFF[_n] = (_off, _off + _w)
    _off += _w
IN_PACKED = _off
_WI_COL = IDX_DIM

ROW_TILE = 512
INT_MIN = -2 ** 31


def _cparams(sem):
    return pltpu.CompilerParams(dimension_semantics=sem, vmem_limit_bytes=VMEM_LIMIT)


def _const_spec(shape):
    nd = len(shape)
    return pl.BlockSpec(shape, lambda *_: (0,) * nd)


def _mod_kernel(c_ref, w_ref, b_ref, o_ref):
    s = jax.nn.silu(c_ref[...])
    o_ref[...] = jnp.dot(s.astype(BF16), w_ref[...], preferred_element_type=F32) + b_ref[...]


def _modulation(c, w_mod, b_mod):
    n = c.shape[0]
    return pl.pallas_call(
        _mod_kernel,
        out_shape=jax.ShapeDtypeStruct((n, 3 * D_MODEL), F32),
        compiler_params=pltpu.CompilerParams(vmem_limit_bytes=VMEM_LIMIT),
        name="mod",
    )(c, w_mod.astype(BF16), b_mod.reshape(1, 3 * D_MODEL))


def _disc_kernel(lre_ref, lim_ref, ldt_ref, bre_ref, bim_ref, ar_ref, ai_ref, bbr_ref, bbi_ref):
    dt = jnp.exp(ldt_ref[...])
    lr, li = lre_ref[...], lim_ref[...]
    mag = jnp.exp(lr * dt)
    ar, ai = mag * jnp.cos(li * dt), mag * jnp.sin(li * dt)
    den = lr * lr + li * li
    zr = ((ar - 1.0) * lr + ai * li) / den
    zi = (ai * lr - (ar - 1.0) * li) / den
    br, bi = bre_ref[...], bim_ref[...]
    ar_ref[...] = ar
    ai_ref[...] = ai
    bbr_ref[...] = zr[None] * br - zi[None] * bi
    bbi_ref[...] = zr[None] * bi + zi[None] * br


def _discretize(lambda_re, lambda_im, log_dt, b_re, b_im):
    g, p, c = b_re.shape
    b_re_t = jnp.transpose(b_re, (2, 0, 1))
    b_im_t = jnp.transpose(b_im, (2, 0, 1))
    return pl.pallas_call(
        _disc_kernel,
        out_shape=(jax.ShapeDtypeStruct((g, p), F32), jax.ShapeDtypeStruct((g, p), F32),
                   jax.ShapeDtypeStruct((c, g, p), F32), jax.ShapeDtypeStruct((c, g, p), F32)),
        name="disc",
    )(lambda_re, lambda_im, log_dt.reshape(g, 1), b_re_t, b_im_t)


def _rope_block(z, cos, sa, sb):
    up = pltpu.roll(z, LANES - ROT_DIM // 2, axis=1)
    dn = pltpu.roll(z, ROT_DIM // 2, axis=1)
    return z * cos + up * sa + dn * sb


def _inproj_kernel(x_ref, mod_ref, g_ref, w_ref, rope2_ref, rope1_ref,
                   q_ref, k_ref, v_ref, qi_ref, ki_ref, kiwi_ref, ga_ref, u_ref, gs_ref,
                   ma_ref, mb_ref):
    bb, tt, d = x_ref.shape
    rows = bb * tt
    x = x_ref[...]
    y = x * lax.rsqrt(jnp.mean(x * x, axis=-1, keepdims=True) + NORM_EPS) * g_ref[...]
    shift = mod_ref[:, :, 0:d]
    scale = mod_ref[:, :, d:2 * d]
    h = (y * (1.0 + scale) + shift).reshape(rows, d).astype(BF16)

    def seg(name):
        a, b = _SEG_OFF[name]
        return jnp.dot(h, w_ref[:, a:b], preferred_element_type=F32)

    def roped(z, tab_ref):
        cos, sa, sb = tab_ref[0], tab_ref[1], tab_ref[2]
        blocks = [_rope_block(z[:, i:i + LANES], cos, sa, sb) for i in range(0, z.shape[1], LANES)]
        return blocks[0] if len(blocks) == 1 else jnp.concatenate(blocks, axis=1)

    def put(ref, val):
        ref[...] = val.reshape(ref.shape).astype(ref.dtype)

    put(q_ref, roped(seg("q"), rope2_ref) * (HEAD_DIM ** -0.5))
    put(k_ref, roped(seg("k"), rope2_ref))
    put(v_ref, seg("v"))
    put(qi_ref, roped(seg("qi"), rope2_ref) * (IDX_DIM ** -0.5))
    kiwi = roped(seg("kiwi"), rope1_ref)
    put(kiwi_ref, kiwi)
    put(ki_ref, kiwi[:, 0:IDX_DIM])
    put(ga_ref, jax.nn.silu(seg("ga")))
    put(u_ref, seg("u"))
    put(gs_ref, jax.nn.silu(seg("gs")))
    put(ma_ref, jax.nn.sigmoid(seg("ma")))
    put(mb_ref, jax.nn.sigmoid(seg("mb")))


def _rope_tables(pos, heads_in_block):
    half = ROT_DIM // 2
    inv = jnp.power(ROPE_THETA, -jnp.arange(half, dtype=F32) * (2.0 / ROT_DIM))
    ang = pos.astype(F32)[:, None] * inv[None, :]
    cos, sin = jnp.cos(ang), jnp.sin(ang)
    t = pos.shape[0]
    ones = jnp.ones((t, HEAD_DIM - ROT_DIM), F32)
    zeros = jnp.zeros((t, HEAD_DIM - ROT_DIM), F32)
    zh = jnp.zeros((t, half), F32)
    c_head = jnp.concatenate([cos, cos, ones], axis=1)
    sa_head = jnp.concatenate([-sin, zh, zeros], axis=1)
    sb_head = jnp.concatenate([zh, sin, zeros], axis=1)
    n_id = LANES // HEAD_DIM - heads_in_block
    ident = [jnp.ones((t, HEAD_DIM), F32)] * n_id
    zero = [jnp.zeros((t, HEAD_DIM), F32)] * n_id
    c = jnp.concatenate([c_head] * heads_in_block + ident, axis=1)
    sa = jnp.concatenate([sa_head] * heads_in_block + zero, axis=1)
    sb = jnp.concatenate([sb_head] * heads_in_block + zero, axis=1)
    return jnp.stack([c, sa, sb])


def _row_blocking(b, t):
    tt = min(t, ROW_TILE)
    bb = ROW_TILE // tt
    assert t % tt == 0 and b % bb == 0 and tt % 16 == 0
    return bb, tt


def _inproj(x, mod, g_norm, w_packed, pos):
    b, t, d = x.shape
    bb, tt = _row_blocking(b, t)
    rows = bb * tt
    rope2 = jnp.tile(_rope_tables(pos, 2), (1, bb, 1)) if bb > 1 else _rope_tables(pos, 2)
    rope1 = jnp.tile(_rope_tables(pos, 1), (1, bb, 1)) if bb > 1 else _rope_tables(pos, 1)

    def tok(width, dtype):
        return (jax.ShapeDtypeStruct((b, t, width), dtype),
                pl.BlockSpec((bb, tt, width), lambda i, j: (i, j, 0)))

    outs = [tok(ATTN_WIDTH, BF16), tok(KV_WIDTH, F32), tok(KV_WIDTH, F32), tok(IDX_WIDTH, BF16),
            tok(IDX_DIM, F32), tok(LANES, F32), tok(ATTN_WIDTH, BF16), tok(SSM_WIDTH, BF16),
            tok(SSM_WIDTH, BF16), tok(D_MODEL, BF16), tok(D_MODEL, BF16)]
    return pl.pallas_call(
        _inproj_kernel,
        grid=(b // bb, t // tt),
        in_specs=[
            pl.BlockSpec((bb, tt, d), lambda i, j: (i, j, 0)),
            pl.BlockSpec((bb, 1, 3 * d), lambda i, j: (i, 0, 0)),
            _const_spec((1, 1, d)),
            pl.BlockSpec((d, IN_PACKED), lambda i, j: (0, 0), pipeline_mode=pl.Buffered(1)),
            pl.BlockSpec((3, rows, LANES), lambda i, j: (0, j, 0)),
            pl.BlockSpec((3, rows, LANES), lambda i, j: (0, j, 0)),
        ],
        out_specs=[o[1] for o in outs],
        out_shape=[o[0] for o in outs],
        compiler_params=_cparams(("parallel", "parallel")),
        name="inproj",
    )(x, mod.reshape(b, 1, 3 * d), g_norm.reshape(1, 1, d), w_packed, rope2, rope1)


def _float_of_key(key):
    bits = jnp.where(key >= 0, key, key ^ jnp.int32(0x7FFFFFFF))
    return lax.bitcast_convert_type(bits, F32)


def _kth_largest(score, kf):
    rows = score.shape[0]

    def body(i, key):
        cand = key ^ jnp.left_shift(jnp.int32(1), 31 - i)
        cnt = jnp.sum(jnp.where(score >= _float_of_key(cand), 1.0, 0.0), axis=1, keepdims=True)
        return jnp.where(cnt >= kf, cand, key)

    key = lax.fori_loop(0, 32, body, jnp.full((rows, 1), INT_MIN, jnp.int32))
    return _float_of_key(key)


def _attend_kernel(q_ref, qi_ref, kiwi_ref, ga_ref, ki_ref, k_ref, v_ref, o_ref, *,
                   q_pos0, l_real, n_sel):
    tq = q_ref.shape[1]
    l_pad = ki_ref.shape[1]
    j = pl.program_id(1)

    qpos = q_pos0 + j * tq + lax.broadcasted_iota(jnp.int32, (tq, 1), 0)
    kpos = lax.broadcasted_iota(jnp.int32, (1, l_pad), 1)
    qchunk = qpos // CHUNK
    adm = ((kpos // CHUNK) <= qchunk) & (kpos < l_real)
    n_adm = jnp.minimum((qchunk + 1) * CHUNK, l_real)
    kf = jnp.minimum(n_adm, n_sel).astype(F32)

    kib = ki_ref[0].astype(BF16)
    qi = qi_ref[0]
    kiwi = kiwi_ref[0]
    score = jnp.zeros((tq, l_pad), F32)
    for h in range(N_IDX_HEADS):
        s = lax.dot_general(qi[:, h * IDX_DIM:(h + 1) * IDX_DIM], kib,
                            (((1,), (1,)), ((), ())), preferred_element_type=F32)
        w = kiwi[:, _WI_COL + h:_WI_COL + h + 1] * (N_IDX_HEADS ** -0.5)
        score = score + w * jnp.maximum(s, 0.0)
    score = jnp.where(adm, score, -jnp.inf)

    thr = _kth_largest(score, kf)
    gt = score > thr
    eq = score == thr
    need = kf - jnp.sum(jnp.where(gt, 1.0, 0.0), axis=1, keepdims=True)
    tri = (lax.broadcasted_iota(jnp.int32, (LANES, LANES), 0)
           <= lax.broadcasted_iota(jnp.int32, (LANES, LANES), 1)).astype(BF16)
    seen = jnp.zeros((tq, 1), F32)
    bias_blocks = []
    for c0 in range(0, l_pad, LANES):
        e = jnp.where(eq[:, c0:c0 + LANES], 1.0, 0.0)
        rank = jnp.dot(e.astype(BF16), tri, preferred_element_type=F32) + seen
        sel = gt[:, c0:c0 + LANES] | (eq[:, c0:c0 + LANES] & (rank <= need))
        bias_blocks.append(jnp.where(sel, 0.0, -jnp.inf))
        seen = seen + jnp.sum(e, axis=1, keepdims=True)
    bias = jnp.concatenate(bias_blocks, axis=1)

    kb = k_ref[0].astype(BF16)
    vb = v_ref[0].astype(BF16)
    q = q_ref[0]
    rep = N_HEADS // N_KV_HEADS
    outs = []
    for h in range(N_HEADS):
        g = h // rep
        kg = kb[:, g * HEAD_DIM:(g + 1) * HEAD_DIM]
        vg = vb[:, g * HEAD_DIM:(g + 1) * HEAD_DIM]
        logits = lax.dot_general(q[:, h * HEAD_DIM:(h + 1) * HEAD_DIM], kg,
                                 (((1,), (1,)), ((), ())), preferred_element_type=F32) + bias
        m = jnp.max(logits, axis=1, keepdims=True)
        p = jnp.exp(logits - m)
        denom = jnp.sum(p, axis=1, keepdims=True)
        o = jnp.dot(p.astype(BF16), vg, preferred_element_type=F32)
        outs.append(o / denom)
    o_all = jnp.concatenate(outs, axis=1)
    o_ref[0] = (o_all * ga_ref[0].astype(F32)).astype(o_ref.dtype)


def _attend(q, qi, kiwi, ga, ki_all, k_all, v_all, q_pos0, l_real):
    b, t, _ = q.shape
    l_pad = ki_all.shape[1]
    tq = QBLK if t % QBLK == 0 else t
    n_sel = min(TOPK_MAX, l_real // 4)
    kern = functools.partial(_attend_kernel, q_pos0=q_pos0, l_real=l_real, n_sel=n_sel)

    def qspec(width):
        return pl.BlockSpec((1, tq, width), lambda i, j: (i, j, 0))

    def kspec(width):
        return pl.BlockSpec((1, l_pad, width), lambda i, j: (i, 0, 0))

    return pl.pallas_call(
        kern,
        grid=(b, t // tq),
        in_specs=[qspec(ATTN_WIDTH), qspec(IDX_WIDTH), qspec(LANES), qspec(ATTN_WIDTH),
                  kspec(IDX_DIM), kspec(KV_WIDTH), kspec(KV_WIDTH)],
        out_specs=qspec(ATTN_WIDTH),
        out_shape=jax.ShapeDtypeStruct((b, t, ATTN_WIDTH), BF16),
        compiler_params=_cparams(("parallel", "parallel")),
        name="attend",
    )(q, qi, kiwi, ga, ki_all, k_all, v_all)


SSM_LANE_CHUNK = 512


def _ssm_kernel(u_ref, h0r_ref, h0i_ref, a_ref, bbd_ref, cbd_ref, d_ref,
                y_ref, hr_ref, hi_ref, s_ref, st_ref):
    tt, nb, w = u_ref.shape
    rows = tt * nb
    n = SSM_LANES

    @pl.when(pl.program_id(0) == 0)
    def _():
        st_ref[0] = h0r_ref[...]
        st_ref[1] = h0i_ref[...]

    u2 = u_ref[...].reshape(rows, w)
    s_ref[...] = jnp.dot(u2, bbd_ref[...], preferred_element_type=F32)

    for c0 in range(0, n, SSM_LANE_CHUNK):
        cs = slice(c0, c0 + SSM_LANE_CHUNK)
        ci = slice(n + c0, n + c0 + SSM_LANE_CHUNK)
        ar = jnp.broadcast_to(a_ref[0:1, cs], (nb, SSM_LANE_CHUNK))
        ai = jnp.broadcast_to(a_ref[1:2, cs], (nb, SSM_LANE_CHUNK))

        def step(t, carry):
            hr, hi = carry
            r0 = pl.multiple_of(t * nb, nb)
            nhr = ar * hr - ai * hi + s_ref[pl.ds(r0, nb), cs]
            nhi = ar * hi + ai * hr + s_ref[pl.ds(r0, nb), ci]
            s_ref[pl.ds(r0, nb), cs] = nhr
            s_ref[pl.ds(r0, nb), ci] = nhi
            return nhr, nhi

        hr, hi = lax.fori_loop(0, tt, step, (st_ref[0, :, cs], st_ref[1, :, cs]), unroll=8)
        st_ref[0, :, cs] = hr
        st_ref[1, :, cs] = hi

    y = jnp.dot(s_ref[...].astype(BF16), cbd_ref[...], preferred_element_type=F32)
    y = y + d_ref[...] * u2.astype(F32)
    y_ref[...] = jax.nn.gelu(y).astype(y_ref.dtype).reshape(tt, nb, w)
    hr_ref[...] = st_ref[0]
    hi_ref[...] = st_ref[1]


def _ssm(u_tb, h0r, h0i, a2, bbd, cbd, d_skip):
    t, nb, w = u_tb.shape
    tt = max(1, 512 // nb)
    assert t % tt == 0
    n = SSM_LANES
    return pl.pallas_call(
        _ssm_kernel,
        grid=(t // tt,),
        in_specs=[pl.BlockSpec((tt, nb, w), lambda i: (i, 0, 0)),
                  _const_spec((nb, n)), _const_spec((nb, n)), _const_spec((2, n)),
                  _const_spec((w, 2 * n)), _const_spec((2 * n, w)), _const_spec((1, w))],
        out_specs=[pl.BlockSpec((tt, nb, w), lambda i: (i, 0, 0)),
                   _const_spec((nb, n)), _const_spec((nb, n))],
        out_shape=[jax.ShapeDtypeStruct((t, nb, w), BF16),
                   jax.ShapeDtypeStruct((nb, n), F32), jax.ShapeDtypeStruct((nb, n), F32)],
        scratch_shapes=[pltpu.VMEM((tt * nb, 2 * n), F32), pltpu.VMEM((2, nb, n), F32)],
        compiler_params=_cparams(("arbitrary",)),
        name="ssm",
    )(u_tb, h0r, h0i, a2, bbd, cbd, d_skip.reshape(1, w))


def _outproj_kernel(x_ref, mod_ref, a_ref, yg_ref, gs_ref, ma_ref, mb_ref,
                    wa_ref, wg_ref, ws_ref, wo_ref, gf_ref, y_ref):
    bb, tt, d = x_ref.shape
    rows = bb * tt

    def flat(ref):
        return ref[...].reshape(rows, ref.shape[-1])

    branch_a = jnp.dot(flat(a_ref), wa_ref[...], preferred_element_type=F32)
    g_lin = jnp.dot(flat(yg_ref), wg_ref[...], preferred_element_type=F32)
    y_glu = g_lin[:, :SSM_WIDTH] * jax.nn.sigmoid(g_lin[:, SSM_WIDTH:])
    gated = y_glu * flat(gs_ref).astype(F32)
    branch_b = jnp.dot(gated.astype(BF16), ws_ref[...], preferred_element_type=F32)
    merged = flat(ma_ref).astype(F32) * branch_a + flat(mb_ref).astype(F32) * branch_b
    proj = jnp.dot(merged.astype(BF16), wo_ref[...], preferred_element_type=F32)
    gate = mod_ref[:, :, 2 * d:3 * d]
    xo = x_ref[...] + gate * proj.reshape(bb, tt, d)
    y = xo * lax.rsqrt(jnp.mean(xo * xo, axis=-1, keepdims=True) + NORM_EPS) * gf_ref[...]
    y_ref[...] = y


def _outproj(x, mod, a, yg, gs, ma, mb, wa, wg, ws, wo, g_final):
    b, t, d = x.shape
    bb, tt = _row_blocking(b, t)

    def tok(width):
        return pl.BlockSpec((bb, tt, width), lambda i, j: (i, j, 0))

    return pl.pallas_call(
        _outproj_kernel,
        grid=(b // bb, t // tt),
        in_specs=[tok(d), pl.BlockSpec((bb, 1, 3 * d), lambda i, j: (i, 0, 0)),
                  tok(ATTN_WIDTH), tok(SSM_WIDTH), tok(SSM_WIDTH), tok(d), tok(d),
                  _const_spec(wa.shape), _const_spec(wg.shape), _const_spec(ws.shape),
                  _const_spec(wo.shape), _const_spec((1, 1, d))],
        out_specs=tok(d),
        out_shape=jax.ShapeDtypeStruct((b, t, d), F32),
        compiler_params=_cparams(("parallel", "parallel")),
        name="outproj",
    )(x, mod.reshape(b, 1, 3 * d), a, yg, gs, ma, mb, wa, wg, ws, wo, g_final.reshape(1, 1, d))


def _pad_keys(a, l_pad):
    return jnp.pad(a, ((0, 0), (0, l_pad - a.shape[1]), (0, 0)))


def _layer(x, mod, pos0, past, prm):
    (g_norm, w_packed, a2, bbd, cbd, d_skip, wg, wa, ws, wo, g_final) = prm
    b, t, _ = x.shape
    pos = pos0 + jnp.arange(t, dtype=jnp.int32)
    q, k, v, qi, ki, kiwi, ga, u, gs, ma, mb = _inproj(x, mod, g_norm, w_packed, pos)

    if past is None:
        k_all, v_all, ki_all = k, v, ki
        h0r = jnp.zeros((b, SSM_LANES), F32)
        h0i = jnp.zeros((b, SSM_LANES), F32)
    else:
        ck, cv, cki, h0r, h0i = past
        k_all = jnp.concatenate([ck.reshape(b, -1, KV_WIDTH), k], axis=1)
        v_all = jnp.concatenate([cv.reshape(b, -1, KV_WIDTH), v], axis=1)
        ki_all = jnp.concatenate([cki, ki], axis=1)
        h0r = h0r.reshape(b, SSM_LANES)
        h0i = h0i.reshape(b, SSM_LANES)
    l_real = k_all.shape[1]
    l_pad = -(-l_real // LANES) * LANES
    if l_pad != l_real:
        k_all, v_all, ki_all = (_pad_keys(z, l_pad) for z in (k_all, v_all, ki_all))

    a = _attend(q, qi, kiwi, ga, ki_all, k_all, v_all, pos0, l_real)

    yg_tb, hr, hi = _ssm(jnp.swapaxes(u, 0, 1), h0r, h0i, a2, bbd, cbd, d_skip)
    yg = jnp.swapaxes(yg_tb, 0, 1)

    y = _outproj(x, mod, a, yg, gs, ma, mb, wa, wg, ws, wo, g_final)
    return (y, k.reshape(b, t, N_KV_HEADS, HEAD_DIM), v.reshape(b, t, N_KV_HEADS, HEAD_DIM), ki,
            hr.reshape(b, N_SSM_GROUPS, SSM_STATE), hi.reshape(b, N_SSM_GROUPS, SSM_STATE))


def kernel(x_prompt, x_sample, cache_k, cache_v, cache_idx_k, state_ssm_re, state_ssm_im,
           c_prompt, c_sample, w_mod, b_mod, g_norm, w_in, lambda_re, lambda_im, log_dt,
           ssm_b_re, ssm_b_im, ssm_c_re, ssm_c_im, d_skip, w_glu, w_attn_proj, w_ssm_proj,
           w_out, g_final):
    depth = w_in.shape[0]
    assert depth == 1, "the final norm is fused into the (single) layer's output kernel"
    nbp = x_prompt.shape[0]
    past_len = cache_k.shape[2]
    l = 0

    mod = _modulation(jnp.concatenate([c_prompt, c_sample], axis=0), w_mod[l], b_mod[l])
    mod_p, mod_s = mod[:nbp], mod[nbp:]

    cut = _SEG_OFF["kiwi"][0] + IDX_DIM + N_IDX_HEADS
    w = w_in[l]
    w_packed = jnp.concatenate(
        [w[:, :cut], jnp.zeros((D_MODEL, _KIWI_PAD), w.dtype), w[:, cut:]], axis=1).astype(BF16)

    ar, ai, bbr, bbi = _discretize(lambda_re[l], lambda_im[l], log_dt[l], ssm_b_re[l], ssm_b_im[l])
    eye = jnp.eye(N_SSM_GROUPS, dtype=F32)
    a2 = jnp.stack([ar.reshape(SSM_LANES), ai.reshape(SSM_LANES)])

    def b_diag(bb_cgp):
        return jnp.einsum("cgp,gh->gchp", bb_cgp, eye).reshape(SSM_WIDTH, SSM_LANES)

    def c_diag(c_gcp):
        return jnp.einsum("gcp,gh->gphc", c_gcp, eye).reshape(SSM_LANES, SSM_WIDTH)

    bbd = jnp.concatenate([b_diag(bbr), b_diag(bbi)], axis=1).astype(BF16)
    cbd = jnp.concatenate([c_diag(ssm_c_re[l]), -c_diag(ssm_c_im[l])], axis=0).astype(BF16)

    prm = (g_norm[l], w_packed, a2, bbd, cbd, d_skip[l], w_glu[l].astype(BF16),
           w_attn_proj[l].astype(BF16), w_ssm_proj[l].astype(BF16), w_out[l].astype(BF16), g_final)

    yp, kp, vp, kip, hrp, hip = _layer(x_prompt, mod_p, 0, None, prm)
    past = (cache_k[l], cache_v[l], cache_idx_k[l], state_ssm_re[l], state_ssm_im[l])
    ys, ks, vs, kis, hrs, his = _layer(x_sample, mod_s, past_len, past, prm)

    def st(z):
        return z[None]

    return (yp, ys, st(kp), st(vp), st(kip), st(hrp), st(hip),
            st(ks), st(vs), st(kis), st(hrs), st(his))
```

```python
import functools
import math

import jax
import jax.numpy as jnp
from jax import lax
from jax.experimental import pallas as pl
from jax.experimental.pallas import tpu as pltpu

F32 = jnp.float32
BF16 = jnp.bfloat16

LANES = 128
VMEM_LIMIT = 56 * 1024 * 1024

D_MODEL = 1024
CHUNK = 64
QBLK = 128
N_HEADS = 8
N_KV_HEADS = 2
HEAD_DIM = 64
ATTN_WIDTH = N_HEADS * HEAD_DIM
KV_WIDTH = N_KV_HEADS * HEAD_DIM
ROT_DIM = HEAD_DIM // 4
ROPE_THETA = 500000.0
N_IDX_HEADS = 8
IDX_DIM = 64
IDX_WIDTH = N_IDX_HEADS * IDX_DIM
TOPK_MAX = 256
SSM_WIDTH = D_MODEL // 2
SSM_GROUP = 16
N_SSM_GROUPS = SSM_WIDTH // SSM_GROUP
SSM_STATE = 64
SSM_LANES = N_SSM_GROUPS * SSM_STATE
NORM_EPS = 1e-6

_KIWI_PAD = LANES - IDX_DIM - N_IDX_HEADS
_SEG_WIDTHS = (ATTN_WIDTH, KV_WIDTH, KV_WIDTH, IDX_WIDTH, LANES, ATTN_WIDTH, SSM_WIDTH, SSM_WIDTH,
               D_MODEL, D_MODEL)
_SEG_NAMES = ("q", "k", "v", "qi", "kiwi", "ga", "u", "gs", "ma", "mb")
_SEG_OFF = {}
_off = 0
for _n, _w in zip(_SEG_NAMES, _SEG_WIDTHS):
    _SEG_OFF[_n] = (_off, _off + _w)
    _off += _w
IN_PACKED = _off
_WI_COL = IDX_DIM

ROW_TILE = 512
INT_MIN = -2 ** 31


def _cparams(sem):
    return pltpu.CompilerParams(dimension_semantics=sem, vmem_limit_bytes=VMEM_LIMIT)


def _const_spec(shape):
    nd = len(shape)
    return pl.BlockSpec(shape, lambda *_: (0,) * nd)


def _mod_kernel(c_ref, w_ref, b_ref, o_ref):
    s = jax.nn.silu(c_ref[...])
    o_ref[...] = jnp.dot(s.astype(BF16), w_ref[...], preferred_element_type=F32) + b_ref[...]


def _modulation(c, w_mod, b_mod):
    n = c.shape[0]
    return pl.pallas_call(
        _mod_kernel,
        out_shape=jax.ShapeDtypeStruct((n, 3 * D_MODEL), F32),
        compiler_params=pltpu.CompilerParams(vmem_limit_bytes=VMEM_LIMIT),
        name="mod",
    )(c, w_mod.astype(BF16), b_mod.reshape(1, 3 * D_MODEL))


def _disc_kernel(lre_ref, lim_ref, ldt_ref, bre_ref, bim_ref, ar_ref, ai_ref, bbr_ref, bbi_ref):
    dt = jnp.exp(ldt_ref[...])
    lr, li = lre_ref[...], lim_ref[...]
    mag = jnp.exp(lr * dt)
    ar, ai = mag * jnp.cos(li * dt), mag * jnp.sin(li * dt)
    den = lr * lr + li * li
    zr = ((ar - 1.0) * lr + ai * li) / den
    zi = (ai * lr - (ar - 1.0) * li) / den
    br, bi = bre_ref[...], bim_ref[...]
    ar_ref[...] = ar
    ai_ref[...] = ai
    bbr_ref[...] = zr[None] * br - zi[None] * bi
    bbi_ref[...] = zr[None] * bi + zi[None] * br


def _discretize(lambda_re, lambda_im, log_dt, b_re, b_im):
    g, p, c = b_re.shape
    b_re_t = jnp.transpose(b_re, (2, 0, 1))
    b_im_t = jnp.transpose(b_im, (2, 0, 1))
    return pl.pallas_call(
        _disc_kernel,
        out_shape=(jax.ShapeDtypeStruct((g, p), F32), jax.ShapeDtypeStruct((g, p), F32),
                   jax.ShapeDtypeStruct((c, g, p), F32), jax.ShapeDtypeStruct((c, g, p), F32)),
        name="disc",
    )(lambda_re, lambda_im, log_dt.reshape(g, 1), b_re_t, b_im_t)


def _rope_block(z, cos, sa, sb):
    up = pltpu.roll(z, LANES - ROT_DIM // 2, axis=1)
    dn = pltpu.roll(z, ROT_DIM // 2, axis=1)
    return z * cos + up * sa + dn * sb


def _inproj_kernel(x_ref, mod_ref, g_ref, w_ref, rope2_ref, rope1_ref,
                   q_ref, k_ref, v_ref, qi_ref, ki_ref, kiwi_ref, ga_ref, u_ref, gs_ref,
                   ma_ref, mb_ref):
    bb, tt, d = x_ref.shape
    rows = bb * tt
    x = x_ref[...]
    y = x * lax.rsqrt(jnp.mean(x * x, axis=-1, keepdims=True) + NORM_EPS) * g_ref[...]
    shift = mod_ref[:, :, 0:d]
    scale = mod_ref[:, :, d:2 * d]
    h = (y * (1.0 + scale) + shift).reshape(rows, d).astype(BF16)

    def seg(name):
        a, b = _SEG_OFF[name]
        return jnp.dot(h, w_ref[:, a:b], preferred_element_type=F32)

    def roped(z, tab_ref):
        cos, sa, sb = tab_ref[0], tab_ref[1], tab_ref[2]
        blocks = [_rope_block(z[:, i:i + LANES], cos, sa, sb) for i in range(0, z.shape[1], LANES)]
        return blocks[0] if len(blocks) == 1 else jnp.concatenate(blocks, axis=1)

    def put(ref, val):
        ref[...] = val.reshape(ref.shape).astype(ref.dtype)

    put(q_ref, roped(seg("q"), rope2_ref) * (HEAD_DIM ** -0.5))
    put(k_ref, roped(seg("k"), rope2_ref))
    put(v_ref, seg("v"))
    put(qi_ref, roped(seg("qi"), rope2_ref) * (IDX_DIM ** -0.5))
    kiwi = roped(seg("kiwi"), rope1_ref)
    put(kiwi_ref, kiwi)
    put(ki_ref, kiwi[:, 0:IDX_DIM])
    put(ga_ref, jax.nn.silu(seg("ga")))
    put(u_ref, seg("u"))
    put(gs_ref, jax.nn.silu(seg("gs")))
    put(ma_ref, jax.nn.sigmoid(seg("ma")))
    put(mb_ref, jax.nn.sigmoid(seg("mb")))


def _rope_tables(pos, heads_in_block):
    half = ROT_DIM // 2
    inv = jnp.power(ROPE_THETA, -jnp.arange(half, dtype=F32) * (2.0 / ROT_DIM))
    ang = pos.astype(F32)[:, None] * inv[None, :]
    cos, sin = jnp.cos(ang), jnp.sin(ang)
    t = pos.shape[0]
    ones = jnp.ones((t, HEAD_DIM - ROT_DIM), F32)
    zeros = jnp.zeros((t, HEAD_DIM - ROT_DIM), F32)
    zh = jnp.zeros((t, half), F32)
    c_head = jnp.concatenate([cos, cos, ones], axis=1)
    sa_head = jnp.concatenate([-sin, zh, zeros], axis=1)
    sb_head = jnp.concatenate([zh, sin, zeros], axis=1)
    n_id = LANES // HEAD_DIM - heads_in_block
    ident = [jnp.ones((t, HEAD_DIM), F32)] * n_id
    zero = [jnp.zeros((t, HEAD_DIM), F32)] * n_id
    c = jnp.concatenate([c_head] * heads_in_block + ident, axis=1)
    sa = jnp.concatenate([sa_head] * heads_in_block + zero, axis=1)
    sb = jnp.concatenate([sb_head] * heads_in_block + zero, axis=1)
    return jnp.stack([c, sa, sb])


def _row_blocking(b, t):
    tt = min(t, ROW_TILE)
    bb = ROW_TILE // tt
    assert t % tt == 0 and b % bb == 0 and tt % 16 == 0
    return bb, tt


def _inproj(x, mod, g_norm, w_packed, pos):
    b, t, d = x.shape
    bb, tt = _row_blocking(b, t)
    rows = bb * tt
    rope2 = jnp.tile(_rope_tables(pos, 2), (1, bb, 1)) if bb > 1 else _rope_tables(pos, 2)
    rope1 = jnp.tile(_rope_tables(pos, 1), (1, bb, 1)) if bb > 1 else _rope_tables(pos, 1)

    def tok(width, dtype):
        return (jax.ShapeDtypeStruct((b, t, width), dtype),
                pl.BlockSpec((bb, tt, width), lambda i, j: (i, j, 0)))

    outs = [tok(ATTN_WIDTH, BF16), tok(KV_WIDTH, F32), tok(KV_WIDTH, F32), tok(IDX_WIDTH, BF16),
            tok(IDX_DIM, F32), tok(LANES, F32), tok(ATTN_WIDTH, BF16), tok(SSM_WIDTH, BF16),
            tok(SSM_WIDTH, BF16), tok(D_MODEL, BF16), tok(D_MODEL, BF16)]
    return pl.pallas_call(
        _inproj_kernel,
        grid=(b // bb, t // tt),
        in_specs=[
            pl.BlockSpec((bb, tt, d), lambda i, j: (i, j, 0)),
            pl.BlockSpec((bb, 1, 3 * d), lambda i, j: (i, 0, 0)),
            _const_spec((1, 1, d)),
            pl.BlockSpec((d, IN_PACKED), lambda i, j: (0, 0), pipeline_mode=pl.Buffered(1)),
            pl.BlockSpec((3, rows, LANES), lambda i, j: (0, j, 0)),
            pl.BlockSpec((3, rows, LANES), lambda i, j: (0, j, 0)),
        ],
        out_specs=[o[1] for o in outs],
        out_shape=[o[0] for o in outs],
        compiler_params=_cparams(("parallel", "parallel")),
        name="inproj",
    )(x, mod.reshape(b, 1, 3 * d), g_norm.reshape(1, 1, d), w_packed, rope2, rope1)


KEY_GROUP = 512
ATTEND_ROWS = 256


def _float_of_key(key):
    bits = jnp.where(key >= 0, key, key ^ jnp.int32(0x7FFFFFFF))
    return lax.bitcast_convert_type(bits, F32)


def _kth_largest(scores, kf):
    def body(i, keys):
        bit = jnp.left_shift(jnp.int32(1), 31 - i)
        out = []
        for score, key in zip(scores, keys):
            cand = key ^ bit
            cnt = jnp.sum(jnp.where(score >= _float_of_key(cand), 1.0, 0.0), axis=1, keepdims=True)
            out.append(jnp.where(cnt >= kf, cand, key))
        return tuple(out)

    init = tuple(jnp.full((s.shape[0], 1), INT_MIN, jnp.int32) for s in scores)
    keys = lax.fori_loop(0, 32, body, init)
    return [_float_of_key(k) for k in keys]


def _topk_bias(score, thr, kf):
    rows, l_keys = score.shape
    gt = score > thr
    eq = score == thr
    need = kf - jnp.sum(jnp.where(gt, 1.0, 0.0), axis=1, keepdims=True)
    tri = (lax.broadcasted_iota(jnp.int32, (LANES, LANES), 0)
           <= lax.broadcasted_iota(jnp.int32, (LANES, LANES), 1)).astype(BF16)
    seen = jnp.zeros((rows, 1), F32)
    blocks = []
    for c0 in range(0, l_keys, LANES):
        e = jnp.where(eq[:, c0:c0 + LANES], 1.0, 0.0)
        rank = jnp.dot(e.astype(BF16), tri, preferred_element_type=F32) + seen
        sel = gt[:, c0:c0 + LANES] | (eq[:, c0:c0 + LANES] & (rank <= need))
        blocks.append(jnp.where(sel, 0.0, -jnp.inf))
        seen = seen + jnp.sum(e, axis=1, keepdims=True)
    return jnp.concatenate(blocks, axis=1)


def _attend_body(refs, j, *, l_keys, search, q_pos0, l_real, n_sel):
    q_ref, qi_ref, kiwi_ref, ga_ref, ki_ref, k_ref, v_ref, o_ref = refs
    nb, tq = q_ref.shape[0], q_ref.shape[1]
    nt = (((1,), (1,)), ((), ()))

    qpos = q_pos0 + j * tq + lax.broadcasted_iota(jnp.int32, (tq, 1), 0)
    kpos = lax.broadcasted_iota(jnp.int32, (1, l_keys), 1)
    qchunk = qpos // CHUNK
    adm = ((kpos // CHUNK) <= qchunk) & (kpos < l_real)
    kf = jnp.minimum(jnp.minimum((qchunk + 1) * CHUNK, l_real), n_sel).astype(F32)

    if search:
        scores = []
        for bi in range(nb):
            kib = ki_ref[bi, 0:l_keys, :].astype(BF16)
            qi = qi_ref[bi]
            kiwi = kiwi_ref[bi]
            score = jnp.zeros((tq, l_keys), F32)
            for h in range(N_IDX_HEADS):
                s = lax.dot_general(qi[:, h * IDX_DIM:(h + 1) * IDX_DIM], kib, nt,
                                    preferred_element_type=F32)
                w = kiwi[:, _WI_COL + h:_WI_COL + h + 1] * (N_IDX_HEADS ** -0.5)
                score = score + w * jnp.maximum(s, 0.0)
            scores.append(jnp.where(adm, score, -jnp.inf))
        thrs = _kth_largest(scores, kf)
        biases = [_topk_bias(s, t, kf) for s, t in zip(scores, thrs)]
    else:
        biases = [jnp.where(adm, 0.0, -jnp.inf)] * nb

    rep = N_HEADS // N_KV_HEADS
    for bi in range(nb):
        kb = k_ref[bi, 0:l_keys, :].astype(BF16)
        vb = v_ref[bi, 0:l_keys, :].astype(BF16)
        q = q_ref[bi]
        outs = []
        for h in range(N_HEADS):
            g = h // rep
            kg = kb[:, g * HEAD_DIM:(g + 1) * HEAD_DIM]
            vg = vb[:, g * HEAD_DIM:(g + 1) * HEAD_DIM]
            logits = lax.dot_general(q[:, h * HEAD_DIM:(h + 1) * HEAD_DIM], kg, nt,
                                     preferred_element_type=F32) + biases[bi]
            m = jnp.max(logits, axis=1, keepdims=True)
            p = jnp.exp(logits - m)
            denom = jnp.sum(p, axis=1, keepdims=True)
            o = jnp.dot(p.astype(BF16), vg, preferred_element_type=F32)
            outs.append(o / denom)
        o_all = jnp.concatenate(outs, axis=1)
        o_ref[bi] = (o_all * ga_ref[bi].astype(F32)).astype(o_ref.dtype)


def _attend_kernel(*refs, branches, **static):
    j = pl.program_id(1)
    if len(branches) == 1:
        _attend_body(refs, j, l_keys=branches[0][2], search=branches[0][3], **static)
        return
    for lo, hi, l_keys, search in branches:
        pl.when((j >= lo) & (j < hi))(
            functools.partial(_attend_body, refs, j, l_keys=l_keys, search=search, **static))


def _attend_branches(nq, tq, q_pos0, l_real, l_pad, n_sel):
    out = []
    for j in range(nq):
        n_max = min(((q_pos0 + (j + 1) * tq - 1) // CHUNK + 1) * CHUNK, l_real)
        search = n_max > n_sel
        gran = KEY_GROUP if search else LANES
        spec = (min(-(-n_max // gran) * gran, l_pad), search)
        if out and out[-1][2:] == spec:
            out[-1] = (out[-1][0], j + 1) + spec
        else:
            out.append((j, j + 1) + spec)
    return tuple(out)


def _attend(q, qi, kiwi, ga, ki_all, k_all, v_all, q_pos0, l_real):
    b, t, _ = q.shape
    l_pad = ki_all.shape[1]
    tq = QBLK if t % QBLK == 0 else t
    nb = max(1, ATTEND_ROWS // tq)
    assert b % nb == 0
    n_sel = min(TOPK_MAX, l_real // 4)
    branches = _attend_branches(t // tq, tq, q_pos0, l_real, l_pad, n_sel)
    kern = functools.partial(_attend_kernel, branches=branches, q_pos0=q_pos0, l_real=l_real,
                             n_sel=n_sel)

    def qspec(width):
        return pl.BlockSpec((nb, tq, width), lambda i, j: (i, j, 0))

    def kspec(width):
        return pl.BlockSpec((nb, l_pad, width), lambda i, j: (i, 0, 0))

    return pl.pallas_call(
        kern,
        grid=(b // nb, t // tq),
        in_specs=[qspec(ATTN_WIDTH), qspec(IDX_WIDTH), qspec(LANES), qspec(ATTN_WIDTH),
                  kspec(IDX_DIM), kspec(KV_WIDTH), kspec(KV_WIDTH)],
        out_specs=qspec(ATTN_WIDTH),
        out_shape=jax.ShapeDtypeStruct((b, t, ATTN_WIDTH), BF16),
        compiler_params=_cparams(("parallel", "parallel")),
        name="attend",
    )(q, qi, kiwi, ga, ki_all, k_all, v_all)


SSM_LANE_CHUNK = 512


def _ssm_kernel(u_ref, h0r_ref, h0i_ref, a_ref, bbd_ref, cbd_ref, d_ref,
                y_ref, hr_ref, hi_ref, s_ref, st_ref):
    tt, nb, w = u_ref.shape
    rows = tt * nb
    n = SSM_LANES

    @pl.when(pl.program_id(0) == 0)
    def _():
        st_ref[0] = h0r_ref[...]
        st_ref[1] = h0i_ref[...]

    u2 = u_ref[...].reshape(rows, w)
    s_ref[...] = jnp.dot(u2, bbd_ref[...], preferred_element_type=F32)

    for c0 in range(0, n, SSM_LANE_CHUNK):
        cs = slice(c0, c0 + SSM_LANE_CHUNK)
        ci = slice(n + c0, n + c0 + SSM_LANE_CHUNK)
        ar = jnp.broadcast_to(a_ref[0:1, cs], (nb, SSM_LANE_CHUNK))
        ai = jnp.broadcast_to(a_ref[1:2, cs], (nb, SSM_LANE_CHUNK))

        def step(t, carry):
            hr, hi = carry
            r0 = pl.multiple_of(t * nb, nb)
            nhr = ar * hr - ai * hi + s_ref[pl.ds(r0, nb), cs]
            nhi = ar * hi + ai * hr + s_ref[pl.ds(r0, nb), ci]
            s_ref[pl.ds(r0, nb), cs] = nhr
            s_ref[pl.ds(r0, nb), ci] = nhi
            return nhr, nhi

        hr, hi = lax.fori_loop(0, tt, step, (st_ref[0, :, cs], st_ref[1, :, cs]), unroll=8)
        st_ref[0, :, cs] = hr
        st_ref[1, :, cs] = hi

    y = jnp.dot(s_ref[...].astype(BF16), cbd_ref[...], preferred_element_type=F32)
    y = y + d_ref[...] * u2.astype(F32)
    y_ref[...] = jax.nn.gelu(y).astype(y_ref.dtype).reshape(tt, nb, w)
    hr_ref[...] = st_ref[0]
    hi_ref[...] = st_ref[1]


def _ssm(u_tb, h0r, h0i, a2, bbd, cbd, d_skip):
    t, nb, w = u_tb.shape
    tt = max(1, 512 // nb)
    assert t % tt == 0
    n = SSM_LANES
    return pl.pallas_call(
        _ssm_kernel,
        grid=(t // tt,),
        in_specs=[pl.BlockSpec((tt, nb, w), lambda i: (i, 0, 0)),
                  _const_spec((nb, n)), _const_spec((nb, n)), _const_spec((2, n)),
                  _const_spec((w, 2 * n)), _const_spec((2 * n, w)), _const_spec((1, w))],
        out_specs=[pl.BlockSpec((tt, nb, w), lambda i: (i, 0, 0)),
                   _const_spec((nb, n)), _const_spec((nb, n))],
        out_shape=[jax.ShapeDtypeStruct((t, nb, w), BF16),
                   jax.ShapeDtypeStruct((nb, n), F32), jax.ShapeDtypeStruct((nb, n), F32)],
        scratch_shapes=[pltpu.VMEM((tt * nb, 2 * n), F32), pltpu.VMEM((2, nb, n), F32)],
        compiler_params=_cparams(("arbitrary",)),
        name="ssm",
    )(u_tb, h0r, h0i, a2, bbd, cbd, d_skip.reshape(1, w))


def _outproj_kernel(x_ref, mod_ref, a_ref, yg_ref, gs_ref, ma_ref, mb_ref,
                    wa_ref, wg_ref, ws_ref, wo_ref, gf_ref, y_ref):
    bb, tt, d = x_ref.shape
    rows = bb * tt

    def flat(ref):
        return ref[...].reshape(rows, ref.shape[-1])

    branch_a = jnp.dot(flat(a_ref), wa_ref[...], preferred_element_type=F32)
    g_lin = jnp.dot(flat(yg_ref), wg_ref[...], preferred_element_type=F32)
    y_glu = g_lin[:, :SSM_WIDTH] * jax.nn.sigmoid(g_lin[:, SSM_WIDTH:])
    gated = y_glu * flat(gs_ref).astype(F32)
    branch_b = jnp.dot(gated.astype(BF16), ws_ref[...], preferred_element_type=F32)
    merged = flat(ma_ref).astype(F32) * branch_a + flat(mb_ref).astype(F32) * branch_b
    proj = jnp.dot(merged.astype(BF16), wo_ref[...], preferred_element_type=F32)
    gate = mod_ref[:, :, 2 * d:3 * d]
    xo = x_ref[...] + gate * proj.reshape(bb, tt, d)
    y = xo * lax.rsqrt(jnp.mean(xo * xo, axis=-1, keepdims=True) + NORM_EPS) * gf_ref[...]
    y_ref[...] = y


def _outproj(x, mod, a, yg, gs, ma, mb, wa, wg, ws, wo, g_final):
    b, t, d = x.shape
    bb, tt = _row_blocking(b, t)

    def tok(width):
        return pl.BlockSpec((bb, tt, width), lambda i, j: (i, j, 0))

    return pl.pallas_call(
        _outproj_kernel,
        grid=(b // bb, t // tt),
        in_specs=[tok(d), pl.BlockSpec((bb, 1, 3 * d), lambda i, j: (i, 0, 0)),
                  tok(ATTN_WIDTH), tok(SSM_WIDTH), tok(SSM_WIDTH), tok(d), tok(d),
                  _const_spec(wa.shape), _const_spec(wg.shape), _const_spec(ws.shape),
                  _const_spec(wo.shape), _const_spec((1, 1, d))],
        out_specs=tok(d),
        out_shape=jax.ShapeDtypeStruct((b, t, d), F32),
        compiler_params=_cparams(("parallel", "parallel")),
        name="outproj",
    )(x, mod.reshape(b, 1, 3 * d), a, yg, gs, ma, mb, wa, wg, ws, wo, g_final.reshape(1, 1, d))


def _pad_keys(a, l_pad):
    return jnp.pad(a, ((0, 0), (0, l_pad - a.shape[1]), (0, 0)))


def _layer(x, mod, pos0, past, prm):
    (g_norm, w_packed, a2, bbd, cbd, d_skip, wg, wa, ws, wo, g_final) = prm
    b, t, _ = x.shape
    pos = pos0 + jnp.arange(t, dtype=jnp.int32)
    q, k, v, qi, ki, kiwi, ga, u, gs, ma, mb = _inproj(x, mod, g_norm, w_packed, pos)

    if past is None:
        k_all, v_all, ki_all = k, v, ki
        h0r = jnp.zeros((b, SSM_LANES), F32)
        h0i = jnp.zeros((b, SSM_LANES), F32)
    else:
        ck, cv, cki, h0r, h0i = past
        k_all = jnp.concatenate([ck.reshape(b, -1, KV_WIDTH), k], axis=1)
        v_all = jnp.concatenate([cv.reshape(b, -1, KV_WIDTH), v], axis=1)
        ki_all = jnp.concatenate([cki, ki], axis=1)
        h0r = h0r.reshape(b, SSM_LANES)
        h0i = h0i.reshape(b, SSM_LANES)
    l_real = k_all.shape[1]
    l_pad = -(-l_real // LANES) * LANES
    if l_pad != l_real:
        k_all, v_all, ki_all = (_pad_keys(z, l_pad) for z in (k_all, v_all, ki_all))

    a = _attend(q, qi, kiwi, ga, ki_all, k_all, v_all, pos0, l_real)

    yg_tb, hr, hi = _ssm(jnp.swapaxes(u, 0, 1), h0r, h0i, a2, bbd, cbd, d_skip)
    yg = jnp.swapaxes(yg_tb, 0, 1)

    y = _outproj(x, mod, a, yg, gs, ma, mb, wa, wg, ws, wo, g_final)
    return (y, k.reshape(b, t, N_KV_HEADS, HEAD_DIM), v.reshape(b, t, N_KV_HEADS, HEAD_DIM), ki,
            hr.reshape(b, N_SSM_GROUPS, SSM_STATE), hi.reshape(b, N_SSM_GROUPS, SSM_STATE))


def kernel(x_prompt, x_sample, cache_k, cache_v, cache_idx_k, state_ssm_re, state_ssm_im,
           c_prompt, c_sample, w_mod, b_mod, g_norm, w_in, lambda_re, lambda_im, log_dt,
           ssm_b_re, ssm_b_im, ssm_c_re, ssm_c_im, d_skip, w_glu, w_attn_proj, w_ssm_proj,
           w_out, g_final):
    depth = w_in.shape[0]
    assert depth == 1, "the final norm is fused into the (single) layer's output kernel"
    nbp = x_prompt.shape[0]
    past_len = cache_k.shape[2]
    l = 0

    mod = _modulation(jnp.concatenate([c_prompt, c_sample], axis=0), w_mod[l], b_mod[l])
    mod_p, mod_s = mod[:nbp], mod[nbp:]

    cut = _SEG_OFF["kiwi"][0] + IDX_DIM + N_IDX_HEADS
    w = w_in[l]
    w_packed = jnp.concatenate(
        [w[:, :cut], jnp.zeros((D_MODEL, _KIWI_PAD), w.dtype), w[:, cut:]], axis=1).astype(BF16)

    ar, ai, bbr, bbi = _discretize(lambda_re[l], lambda_im[l], log_dt[l], ssm_b_re[l], ssm_b_im[l])
    eye = jnp.eye(N_SSM_GROUPS, dtype=F32)
    a2 = jnp.stack([ar.reshape(SSM_LANES), ai.reshape(SSM_LANES)])

    def b_diag(bb_cgp):
        return jnp.einsum("cgp,gh->gchp", bb_cgp, eye).reshape(SSM_WIDTH, SSM_LANES)

    def c_diag(c_gcp):
        return jnp.einsum("gcp,gh->gphc", c_gcp, eye).reshape(SSM_LANES, SSM_WIDTH)

    bbd = jnp.concatenate([b_diag(bbr), b_diag(bbi)], axis=1).astype(BF16)
    cbd = jnp.concatenate([c_diag(ssm_c_re[l]), -c_diag(ssm_c_im[l])], axis=0).astype(BF16)

    prm = (g_norm[l], w_packed, a2, bbd, cbd, d_skip[l], w_glu[l].astype(BF16),
           w_attn_proj[l].astype(BF16), w_ssm_proj[l].astype(BF16), w_out[l].astype(BF16), g_final)

    yp, kp, vp, kip, hrp, hip = _layer(x_prompt, mod_p, 0, None, prm)
    past = (cache_k[l], cache_v[l], cache_idx_k[l], state_ssm_re[l], state_ssm_im[l])
    ys, ks, vs, kis, hrs, his = _layer(x_sample, mod_s, past_len, past, prm)

    def st(z):
        return z[None]

    return (yp, ys, st(kp), st(vp), st(kip), st(hrp), st(hip),
            st(ks), st(vs), st(kis), st(hrs), st(his))
```

```python
import functools
import math

import jax
import jax.numpy as jnp
from jax import lax
from jax.experimental import pallas as pl
from jax.experimental.pallas import tpu as pltpu

F32 = jnp.float32
BF16 = jnp.bfloat16

LANES = 128
VMEM_LIMIT = 56 * 1024 * 1024

D_MODEL = 1024
CHUNK = 64
QBLK = 128
N_HEADS = 8
N_KV_HEADS = 2
HEAD_DIM = 64
ATTN_WIDTH = N_HEADS * HEAD_DIM
KV_WIDTH = N_KV_HEADS * HEAD_DIM
ROT_DIM = HEAD_DIM // 4
ROPE_THETA = 500000.0
N_IDX_HEADS = 8
IDX_DIM = 64
IDX_WIDTH = N_IDX_HEADS * IDX_DIM
TOPK_MAX = 256
SSM_WIDTH = D_MODEL // 2
SSM_GROUP = 16
N_SSM_GROUPS = SSM_WIDTH // SSM_GROUP
SSM_STATE = 64
SSM_LANES = N_SSM_GROUPS * SSM_STATE
NORM_EPS = 1e-6

_KIWI_PAD = LANES - IDX_DIM - N_IDX_HEADS
_SEG_WIDTHS = (ATTN_WIDTH, KV_WIDTH, KV_WIDTH, IDX_WIDTH, LANES, ATTN_WIDTH, SSM_WIDTH, SSM_WIDTH,
               D_MODEL, D_MODEL)
_SEG_NAMES = ("q", "k", "v", "qi", "kiwi", "ga", "u", "gs", "ma", "mb")
_SEG_OFF = {}
_off = 0
for _n, _w in zip(_SEG_NAMES, _SEG_WIDTHS):
    _SEG_OFF[_n] = (_off, _off + _w)
    _off += _w
IN_PACKED = _off
_WI_COL = IDX_DIM

ROW_TILE = 512
PAD_HEAD = 2 * HEAD_DIM
INT_MIN = -2 ** 31


def _cparams(sem):
    return pltpu.CompilerParams(dimension_semantics=sem, vmem_limit_bytes=VMEM_LIMIT)


def _const_spec(shape):
    nd = len(shape)
    return pl.BlockSpec(shape, lambda *_: (0,) * nd)


def _mod_kernel(c_ref, w_ref, b_ref, o_ref):
    s = jax.nn.silu(c_ref[...])
    o_ref[...] = jnp.dot(s.astype(BF16), w_ref[...], preferred_element_type=F32) + b_ref[...]


def _modulation(c, w_mod, b_mod):
    n = c.shape[0]
    return pl.pallas_call(
        _mod_kernel,
        out_shape=jax.ShapeDtypeStruct((n, 3 * D_MODEL), F32),
        compiler_params=pltpu.CompilerParams(vmem_limit_bytes=VMEM_LIMIT),
        name="mod",
    )(c, w_mod.astype(BF16), b_mod.reshape(1, 3 * D_MODEL))


def _disc_kernel(lre_ref, lim_ref, ldt_ref, bre_ref, bim_ref, ar_ref, ai_ref, bbr_ref, bbi_ref):
    dt = jnp.exp(ldt_ref[...])
    lr, li = lre_ref[...], lim_ref[...]
    mag = jnp.exp(lr * dt)
    ar, ai = mag * jnp.cos(li * dt), mag * jnp.sin(li * dt)
    den = lr * lr + li * li
    zr = ((ar - 1.0) * lr + ai * li) / den
    zi = (ai * lr - (ar - 1.0) * li) / den
    br, bi = bre_ref[...], bim_ref[...]
    ar_ref[...] = ar
    ai_ref[...] = ai
    bbr_ref[...] = zr[None] * br - zi[None] * bi
    bbi_ref[...] = zr[None] * bi + zi[None] * br


def _discretize(lambda_re, lambda_im, log_dt, b_re, b_im):
    g, p, c = b_re.shape
    b_re_t = jnp.transpose(b_re, (2, 0, 1))
    b_im_t = jnp.transpose(b_im, (2, 0, 1))
    return pl.pallas_call(
        _disc_kernel,
        out_shape=(jax.ShapeDtypeStruct((g, p), F32), jax.ShapeDtypeStruct((g, p), F32),
                   jax.ShapeDtypeStruct((c, g, p), F32), jax.ShapeDtypeStruct((c, g, p), F32)),
        name="disc",
    )(lambda_re, lambda_im, log_dt.reshape(g, 1), b_re_t, b_im_t)


def _rope_block(z, cos, sa, sb):
    up = pltpu.roll(z, LANES - ROT_DIM // 2, axis=1)
    dn = pltpu.roll(z, ROT_DIM // 2, axis=1)
    return z * cos + up * sa + dn * sb


def _split_heads(z, fill):
    low = lax.broadcasted_iota(jnp.int32, z.shape, 1) < HEAD_DIM
    return jnp.concatenate([jnp.where(low, z, fill),
                            jnp.where(low, pltpu.roll(z, HEAD_DIM, axis=1), fill)], axis=1)


def _inproj_kernel(x_ref, mod_ref, g_ref, w_ref, rope2_ref, rope1_ref,
                   q_ref, k_ref, v_ref, qi_ref, ki_ref, kiwi_ref, kp_ref, vx_ref, kip_ref,
                   ga_ref, u_ref, gs_ref, ma_ref, mb_ref):
    bb, tt, d = x_ref.shape
    rows = bb * tt
    x = x_ref[...]
    y = x * lax.rsqrt(jnp.mean(x * x, axis=-1, keepdims=True) + NORM_EPS) * g_ref[...]
    shift = mod_ref[:, :, 0:d]
    scale = mod_ref[:, :, d:2 * d]
    h = (y * (1.0 + scale) + shift).reshape(rows, d).astype(BF16)

    def seg(name):
        a, b = _SEG_OFF[name]
        return jnp.dot(h, w_ref[:, a:b], preferred_element_type=F32)

    def roped(z, tab_ref):
        cos, sa, sb = tab_ref[0], tab_ref[1], tab_ref[2]
        blocks = [_rope_block(z[:, i:i + LANES], cos, sa, sb) for i in range(0, z.shape[1], LANES)]
        return blocks[0] if len(blocks) == 1 else jnp.concatenate(blocks, axis=1)

    def put(ref, val):
        ref[...] = val.reshape(ref.shape).astype(ref.dtype)

    def split_all(z, fill=0.0):
        return jnp.concatenate([_split_heads(z[:, i:i + LANES], fill)
                                for i in range(0, z.shape[1], LANES)], axis=1)

    def put_heads(ref, z):
        zp = split_all(z)
        for hd in range(ref.shape[1]):
            ref[:, hd] = zp[:, hd * PAD_HEAD:(hd + 1) * PAD_HEAD].reshape(bb, tt, PAD_HEAD).astype(ref.dtype)

    put_heads(q_ref, roped(seg("q"), rope2_ref) * (HEAD_DIM ** -0.5 * math.log2(math.e)))
    k = roped(seg("k"), rope2_ref)
    put(k_ref, k)
    put(kp_ref, split_all(k))
    v = seg("v")
    put(v_ref, v)
    put(vx_ref, split_all(v, 1.0))
    put_heads(qi_ref, roped(seg("qi"), rope2_ref) * (IDX_DIM ** -0.5))
    kiwi = roped(seg("kiwi"), rope1_ref)
    put(kiwi_ref, kiwi)
    put(ki_ref, kiwi[:, 0:IDX_DIM])
    low = lax.broadcasted_iota(jnp.int32, kiwi.shape, 1) < IDX_DIM
    put(kip_ref, jnp.where(low, kiwi, 0.0))
    put(ga_ref, jax.nn.silu(seg("ga")))
    put(u_ref, seg("u"))
    put(gs_ref, jax.nn.silu(seg("gs")))
    put(ma_ref, jax.nn.sigmoid(seg("ma")))
    put(mb_ref, jax.nn.sigmoid(seg("mb")))


def _rope_tables(pos, heads_in_block):
    half = ROT_DIM // 2
    inv = jnp.power(ROPE_THETA, -jnp.arange(half, dtype=F32) * (2.0 / ROT_DIM))
    ang = pos.astype(F32)[:, None] * inv[None, :]
    cos, sin = jnp.cos(ang), jnp.sin(ang)
    t = pos.shape[0]
    ones = jnp.ones((t, HEAD_DIM - ROT_DIM), F32)
    zeros = jnp.zeros((t, HEAD_DIM - ROT_DIM), F32)
    zh = jnp.zeros((t, half), F32)
    c_head = jnp.concatenate([cos, cos, ones], axis=1)
    sa_head = jnp.concatenate([-sin, zh, zeros], axis=1)
    sb_head = jnp.concatenate([zh, sin, zeros], axis=1)
    n_id = LANES // HEAD_DIM - heads_in_block
    ident = [jnp.ones((t, HEAD_DIM), F32)] * n_id
    zero = [jnp.zeros((t, HEAD_DIM), F32)] * n_id
    c = jnp.concatenate([c_head] * heads_in_block + ident, axis=1)
    sa = jnp.concatenate([sa_head] * heads_in_block + zero, axis=1)
    sb = jnp.concatenate([sb_head] * heads_in_block + zero, axis=1)
    return jnp.stack([c, sa, sb])


def _row_blocking(b, t):
    tt = min(t, ROW_TILE)
    bb = ROW_TILE // tt
    assert t % tt == 0 and b % bb == 0 and tt % 16 == 0
    return bb, tt


def _inproj(x, mod, g_norm, w_packed, pos):
    b, t, d = x.shape
    bb, tt = _row_blocking(b, t)
    rows = bb * tt
    rope2 = jnp.tile(_rope_tables(pos, 2), (1, bb, 1)) if bb > 1 else _rope_tables(pos, 2)
    rope1 = jnp.tile(_rope_tables(pos, 1), (1, bb, 1)) if bb > 1 else _rope_tables(pos, 1)

    def tok(width, dtype):
        return (jax.ShapeDtypeStruct((b, t, width), dtype),
                pl.BlockSpec((bb, tt, width), lambda i, j: (i, j, 0)))

    def heads(n):
        return (jax.ShapeDtypeStruct((b, n, t, PAD_HEAD), BF16),
                pl.BlockSpec((bb, n, tt, PAD_HEAD), lambda i, j: (i, 0, j, 0)))

    outs = [heads(N_HEADS), tok(KV_WIDTH, F32), tok(KV_WIDTH, F32),
            heads(N_IDX_HEADS), tok(IDX_DIM, F32), tok(LANES, F32),
            tok(2 * KV_WIDTH, BF16), tok(2 * KV_WIDTH, BF16), tok(LANES, BF16),
            tok(ATTN_WIDTH, BF16), tok(SSM_WIDTH, BF16),
            tok(SSM_WIDTH, BF16), tok(D_MODEL, BF16), tok(D_MODEL, BF16)]
    return pl.pallas_call(
        _inproj_kernel,
        grid=(b // bb, t // tt),
        in_specs=[
            pl.BlockSpec((bb, tt, d), lambda i, j: (i, j, 0)),
            pl.BlockSpec((bb, 1, 3 * d), lambda i, j: (i, 0, 0)),
            _const_spec((1, 1, d)),
            pl.BlockSpec((d, IN_PACKED), lambda i, j: (0, 0), pipeline_mode=pl.Buffered(1)),
            pl.BlockSpec((3, rows, LANES), lambda i, j: (0, j, 0)),
            pl.BlockSpec((3, rows, LANES), lambda i, j: (0, j, 0)),
        ],
        out_specs=[o[1] for o in outs],
        out_shape=[o[0] for o in outs],
        compiler_params=_cparams(("parallel", "parallel")),
        name="inproj",
    )(x, mod.reshape(b, 1, 3 * d), g_norm.reshape(1, 1, d), w_packed, rope2, rope1)


KEY_BLOCK = 256
ATTEND_ROWS = 512


def _float_of_key(key):
    bits = jnp.where(key >= 0, key, key ^ jnp.int32(0x7FFFFFFF))
    return lax.bitcast_convert_type(bits, F32)


def _count(sc_ref, bi, nkb, pred):
    acc = None
    for blk in range(nkb):
        hit = jnp.where(pred(sc_ref[bi, blk]), 1.0, 0.0)
        part = hit[:, :LANES] + hit[:, LANES:]
        acc = part if acc is None else acc + part
    return jnp.sum(acc, axis=1, keepdims=True)


def _search_thresholds(sc_ref, thr_ref, need_ref, kf, nkb):
    nb, _, rows, _ = sc_ref.shape

    def body(i, keys):
        bit = jnp.left_shift(jnp.int32(1), 31 - i)
        out = []
        for bi, key in enumerate(keys):
            cand = key ^ bit
            cf = _float_of_key(cand)
            cnt = _count(sc_ref, bi, nkb, lambda x, cf=cf: x >= cf)
            out.append(jnp.where(cnt >= kf, cand, key))
        return tuple(out)

    init = tuple(jnp.full((rows, 1), INT_MIN, jnp.int32) for _ in range(nb))
    keys = lax.fori_loop(0, 32, body, init)
    for bi, key in enumerate(keys):
        thr = _float_of_key(key)
        need = kf - _count(sc_ref, bi, nkb, lambda x, thr=thr: x > thr)
        thr_ref[bi] = jnp.broadcast_to(thr, thr_ref.shape[1:])
        need_ref[bi] = jnp.broadcast_to(need, need_ref.shape[1:])


def _attend_kernel(*refs, has_past, variants, search_from, q_pos0, past_len, present_len, n_sel):
    q_ref, qi_ref, kiwi_ref, ga_ref = refs[:4]
    n_in = 10 if has_past else 7
    past = refs[4:7] if has_past else None
    present = refs[n_in - 3:n_in]
    o_ref = refs[n_in]
    sc_ref, thr_ref, need_ref, macc_ref, oacc_ref = refs[n_in + 1:]
    nb, tq = q_ref.shape[0], q_ref.shape[2]
    kb = KEY_BLOCK
    n_past = past_len // kb
    nt = (((1,), (1,)), ((), ()))
    rep = N_HEADS // N_KV_HEADS
    j = pl.program_id(1)

    qpos = q_pos0 + j * tq + lax.broadcasted_iota(jnp.int32, (tq, 1), 0)
    qchunk = qpos // CHUNK
    n_present = jnp.minimum((qchunk + 1) * CHUNK - past_len, present_len)
    kf = jnp.minimum(past_len + n_present, n_sel).astype(F32)
    last_chunk = (q_pos0 + (j + 1) * tq - 1) // CHUNK
    nkb_present = (jnp.minimum((last_chunk + 1) * CHUNK - past_len, present_len) + kb - 1) // kb

    def admissible(i):
        local = i * kb + lax.broadcasted_iota(jnp.int32, (1, kb), 1)
        return (((past_len + local) // CHUNK) <= qchunk) & (local < present_len)

    def for_blocks(fn, carry=0):
        if n_past:
            carry = lax.fori_loop(0, n_past, lambda i, c: fn(past, i, i, c), carry)
        return lax.fori_loop(0, nkb_present, lambda i, c: fn(present, i, n_past + i, c), carry)

    def rows_of(i):
        return pl.ds(pl.multiple_of(i * kb, kb), kb)

    def search_path():
        qi_rows = [qi_ref[bi].reshape(N_IDX_HEADS * tq, PAD_HEAD) for bi in range(nb)]
        wts = [[jnp.broadcast_to(kiwi_ref[bi, :, _WI_COL + h:_WI_COL + h + 1]
                                 * (N_IDX_HEADS ** -0.5), (tq, kb))
                for h in range(N_IDX_HEADS)] for bi in range(nb)]

        def score_block(src, i, blk, c):
            for bi in range(nb):
                s = lax.dot_general(qi_rows[bi], src[0][bi, rows_of(i), :], nt,
                                    preferred_element_type=F32)
                acc = None
                for h in range(N_IDX_HEADS):
                    term = wts[bi][h] * jnp.maximum(s[h * tq:(h + 1) * tq], 0.0)
                    acc = term if acc is None else acc + term
                if src is present:
                    acc = jnp.where(admissible(i), acc, -jnp.inf)
                sc_ref[bi, blk] = acc
            return c

        for_blocks(score_block)

        for lo, hi, nkb in variants:
            if len(variants) == 1:
                _search_thresholds(sc_ref, thr_ref, need_ref, kf, nkb)
            else:
                pl.when((j >= lo) & (j < hi))(
                    functools.partial(_search_thresholds, sc_ref, thr_ref, need_ref, kf, nkb))

        tri = (lax.broadcasted_iota(jnp.int32, (kb, kb), 0)
               <= lax.broadcasted_iota(jnp.int32, (kb, kb), 1)).astype(BF16)
        ones = jnp.ones((kb, LANES), BF16)

        def twice(a):
            return jnp.concatenate([a, a], axis=1)

        def bias_block(src, i, blk, seen):
            out = []
            for bi in range(nb):
                x = sc_ref[bi, blk]
                thr = twice(thr_ref[bi])
                eq = x == thr
                e = jnp.where(eq, 1.0, 0.0).astype(BF16)
                rank = jnp.dot(e, tri, preferred_element_type=F32) + twice(seen[bi])
                tie = jnp.where(rank <= twice(need_ref[bi]), 0.0, -jnp.inf)
                sc_ref[bi, blk] = jnp.where(eq, tie, jnp.where(x > thr, 0.0, -jnp.inf))
                out.append(seen[bi] + jnp.dot(e, ones, preferred_element_type=F32))
            return tuple(out)

        for_blocks(bias_block, tuple(jnp.zeros((tq, LANES), F32) for _ in range(nb)))

    def all_admissible_path():
        def bias_block(src, i, blk, c):
            for bi in range(nb):
                if src is present:
                    sc_ref[bi, blk] = jnp.where(admissible(i), 0.0, -jnp.inf)
                else:
                    sc_ref[bi, blk] = jnp.zeros((tq, kb), F32)
            return c

        for_blocks(bias_block)

    if search_from == 0:
        search_path()
    else:
        pl.when(j >= search_from)(search_path)
        pl.when(j < search_from)(all_admissible_path)

    q_rows = [[q_ref[bi, g * rep:(g + 1) * rep].reshape(rep * tq, PAD_HEAD)
               for g in range(N_KV_HEADS)] for bi in range(nb)]

    macc_ref[...] = jnp.full(macc_ref.shape, jnp.finfo(F32).min, F32)
    oacc_ref[...] = jnp.zeros(oacc_ref.shape, F32)

    def pv_block(src, i, blk, c):
        for bi in range(nb):
            bias = sc_ref[bi, blk]
            for g in range(N_KV_HEADS):
                lanes = slice(g * PAD_HEAD, (g + 1) * PAD_HEAD)
                lg = lax.dot_general(q_rows[bi][g], src[1][bi, rows_of(i), lanes], nt,
                                     preferred_element_type=F32)
                lg = (lg.reshape(rep, tq, kb) + bias[None]).reshape(rep * tq, kb)
                m_old = macc_ref[bi, g]
                row_max = jnp.max(jnp.maximum(lg[:, :LANES], lg[:, LANES:]), axis=1, keepdims=True)
                m_new = jnp.maximum(m_old, row_max)
                p = jnp.exp2(lg - jnp.concatenate([m_new, m_new], axis=1))
                oacc_ref[bi, g] = (oacc_ref[bi, g] * jnp.exp2(m_old - m_new)
                                   + jnp.dot(p.astype(BF16), src[2][bi, rows_of(i), lanes],
                                             preferred_element_type=F32))
                macc_ref[bi, g] = m_new
        return c

    for_blocks(pv_block)

    low = lax.broadcasted_iota(jnp.int32, (tq, LANES), 1) < HEAD_DIM
    for bi in range(nb):
        outs = []
        for g in range(N_KV_HEADS):
            acc = oacc_ref[bi, g]
            o = acc / pltpu.roll(acc, HEAD_DIM, axis=1)
            for h in range(0, rep, 2):
                outs.append(jnp.where(low, o[h * tq:(h + 1) * tq],
                                      pltpu.roll(o[(h + 1) * tq:(h + 2) * tq], HEAD_DIM, axis=1)))
        o_all = jnp.concatenate(outs, axis=1)
        o_ref[bi] = (o_all * ga_ref[bi].astype(F32)).astype(o_ref.dtype)


def _attend_variants(nq, tq, q_pos0, past_len, present_len, n_sel):
    kb = KEY_BLOCK
    variants, search_from = [], nq
    for j in range(nq):
        n_present = min(((q_pos0 + (j + 1) * tq - 1) // CHUNK + 1) * CHUNK - past_len, present_len)
        if past_len + n_present <= n_sel:
            assert not variants
            continue
        search_from = min(search_from, j)
        nkb = past_len // kb + -(-n_present // kb)
        if variants and variants[-1][2] == nkb:
            variants[-1] = (variants[-1][0], j + 1, nkb)
        else:
            variants.append((j, j + 1, nkb))
    return search_from, tuple(variants)


def _attend(q, qi, kiwi, ga, present, past, q_pos0, past_len, present_len):
    b, _, t, _ = q.shape
    kb = KEY_BLOCK
    tq = QBLK if t % QBLK == 0 else t
    nb = max(1, ATTEND_ROWS // tq)
    l_present = present[0].shape[1]
    assert b % nb == 0 and past_len % kb == 0 and l_present % kb == 0 and q_pos0 >= past_len
    n_sel = min(TOPK_MAX, (past_len + present_len) // 4)
    search_from, variants = _attend_variants(t // tq, tq, q_pos0, past_len, present_len, n_sel)
    nkb_max = past_len // kb + l_present // kb
    kern = functools.partial(_attend_kernel, has_past=past is not None, variants=variants,
                             search_from=search_from, q_pos0=q_pos0, past_len=past_len,
                             present_len=present_len, n_sel=n_sel)

    def qspec(width):
        return pl.BlockSpec((nb, tq, width), lambda i, j: (i, j, 0))

    def hspec(a):
        return pl.BlockSpec((nb, a.shape[1], tq, PAD_HEAD), lambda i, j: (i, 0, j, 0))

    def kspec(a):
        return pl.BlockSpec((nb,) + a.shape[1:], lambda i, j: (i, 0, 0))

    keys = (tuple(past) if past is not None else ()) + tuple(present)
    return pl.pallas_call(
        kern,
        grid=(b // nb, t // tq),
        in_specs=[hspec(q), hspec(qi), qspec(LANES), qspec(ATTN_WIDTH)]
        + [kspec(a) for a in keys],
        out_specs=qspec(ATTN_WIDTH),
        out_shape=jax.ShapeDtypeStruct((b, t, ATTN_WIDTH), BF16),
        scratch_shapes=[pltpu.VMEM((nb, nkb_max, tq, kb), F32),
                        pltpu.VMEM((nb, tq, LANES), F32), pltpu.VMEM((nb, tq, LANES), F32),
                        pltpu.VMEM((nb, N_KV_HEADS, N_HEADS // N_KV_HEADS * tq, LANES), F32),
                        pltpu.VMEM((nb, N_KV_HEADS, N_HEADS // N_KV_HEADS * tq, LANES), F32)],
        compiler_params=_cparams(("parallel", "parallel")),
        name="attend",
    )(q, qi, kiwi, ga, *keys)


SSM_LANE_CHUNK = 512


def _ssm_kernel(u_ref, h0r_ref, h0i_ref, a_ref, bbd_ref, cbd_ref, d_ref,
                y_ref, hr_ref, hi_ref, s_ref, st_ref):
    tt, nb, w = u_ref.shape
    rows = tt * nb
    n = SSM_LANES

    @pl.when(pl.program_id(0) == 0)
    def _():
        st_ref[0] = h0r_ref[...]
        st_ref[1] = h0i_ref[...]

    u2 = u_ref[...].reshape(rows, w)
    s_ref[...] = jnp.dot(u2, bbd_ref[...], preferred_element_type=F32)

    for c0 in range(0, n, SSM_LANE_CHUNK):
        cs = slice(c0, c0 + SSM_LANE_CHUNK)
        ci = slice(n + c0, n + c0 + SSM_LANE_CHUNK)
        ar = jnp.broadcast_to(a_ref[0:1, cs], (nb, SSM_LANE_CHUNK))
        ai = jnp.broadcast_to(a_ref[1:2, cs], (nb, SSM_LANE_CHUNK))

        def step(t, carry):
            hr, hi = carry
            r0 = pl.multiple_of(t * nb, nb)
            nhr = ar * hr - ai * hi + s_ref[pl.ds(r0, nb), cs]
            nhi = ar * hi + ai * hr + s_ref[pl.ds(r0, nb), ci]
            s_ref[pl.ds(r0, nb), cs] = nhr
            s_ref[pl.ds(r0, nb), ci] = nhi
            return nhr, nhi

        hr, hi = lax.fori_loop(0, tt, step, (st_ref[0, :, cs], st_ref[1, :, cs]), unroll=8)
        st_ref[0, :, cs] = hr
        st_ref[1, :, cs] = hi

    y = jnp.dot(s_ref[...].astype(BF16), cbd_ref[...], preferred_element_type=F32)
    y = y + d_ref[...] * u2.astype(F32)
    y_ref[...] = jax.nn.gelu(y).astype(y_ref.dtype).reshape(tt, nb, w)
    hr_ref[...] = st_ref[0]
    hi_ref[...] = st_ref[1]


def _ssm(u_tb, h0r, h0i, a2, bbd, cbd, d_skip):
    t, nb, w = u_tb.shape
    tt = max(1, 512 // nb)
    assert t % tt == 0
    n = SSM_LANES
    return pl.pallas_call(
        _ssm_kernel,
        grid=(t // tt,),
        in_specs=[pl.BlockSpec((tt, nb, w), lambda i: (i, 0, 0)),
                  _const_spec((nb, n)), _const_spec((nb, n)), _const_spec((2, n)),
                  _const_spec((w, 2 * n)), _const_spec((2 * n, w)), _const_spec((1, w))],
        out_specs=[pl.BlockSpec((tt, nb, w), lambda i: (i, 0, 0)),
                   _const_spec((nb, n)), _const_spec((nb, n))],
        out_shape=[jax.ShapeDtypeStruct((t, nb, w), BF16),
                   jax.ShapeDtypeStruct((nb, n), F32), jax.ShapeDtypeStruct((nb, n), F32)],
        scratch_shapes=[pltpu.VMEM((tt * nb, 2 * n), F32), pltpu.VMEM((2, nb, n), F32)],
        compiler_params=_cparams(("arbitrary",)),
        name="ssm",
    )(u_tb, h0r, h0i, a2, bbd, cbd, d_skip.reshape(1, w))


def _outproj_kernel(x_ref, mod_ref, a_ref, yg_ref, gs_ref, ma_ref, mb_ref,
                    wa_ref, wg_ref, ws_ref, wo_ref, gf_ref, y_ref):
    bb, tt, d = x_ref.shape
    rows = bb * tt

    def flat(ref):
        return ref[...].reshape(rows, ref.shape[-1])

    branch_a = jnp.dot(flat(a_ref), wa_ref[...], preferred_element_type=F32)
    g_lin = jnp.dot(flat(yg_ref), wg_ref[...], preferred_element_type=F32)
    y_glu = g_lin[:, :SSM_WIDTH] * jax.nn.sigmoid(g_lin[:, SSM_WIDTH:])
    gated = y_glu * flat(gs_ref).astype(F32)
    branch_b = jnp.dot(gated.astype(BF16), ws_ref[...], preferred_element_type=F32)
    merged = flat(ma_ref).astype(F32) * branch_a + flat(mb_ref).astype(F32) * branch_b
    proj = jnp.dot(merged.astype(BF16), wo_ref[...], preferred_element_type=F32)
    gate = mod_ref[:, :, 2 * d:3 * d]
    xo = x_ref[...] + gate * proj.reshape(bb, tt, d)
    y = xo * lax.rsqrt(jnp.mean(xo * xo, axis=-1, keepdims=True) + NORM_EPS) * gf_ref[...]
    y_ref[...] = y


def _outproj(x, mod, a, yg, gs, ma, mb, wa, wg, ws, wo, g_final):
    b, t, d = x.shape
    bb, tt = _row_blocking(b, t)

    def tok(width):
        return pl.BlockSpec((bb, tt, width), lambda i, j: (i, j, 0))

    return pl.pallas_call(
        _outproj_kernel,
        grid=(b // bb, t // tt),
        in_specs=[tok(d), pl.BlockSpec((bb, 1, 3 * d), lambda i, j: (i, 0, 0)),
                  tok(ATTN_WIDTH), tok(SSM_WIDTH), tok(SSM_WIDTH), tok(d), tok(d),
                  _const_spec(wa.shape), _const_spec(wg.shape), _const_spec(ws.shape),
                  _const_spec(wo.shape), _const_spec((1, 1, d))],
        out_specs=tok(d),
        out_shape=jax.ShapeDtypeStruct((b, t, d), F32),
        compiler_params=_cparams(("parallel", "parallel")),
        name="outproj",
    )(x, mod.reshape(b, 1, 3 * d), a, yg, gs, ma, mb, wa, wg, ws, wo, g_final.reshape(1, 1, d))


def _pad_rows(a, rows):
    return jnp.pad(a, ((0, 0), (0, rows - a.shape[1]), (0, 0)))


def _pad_lanes(a, fill):
    return jnp.concatenate([a, jnp.full(a.shape, fill, a.dtype)], axis=-1)


def _layer(x, mod, pos0, past, prm):
    (g_norm, w_packed, a2, bbd, cbd, d_skip, wg, wa, ws, wo, g_final) = prm
    b, t, _ = x.shape
    pos = pos0 + jnp.arange(t, dtype=jnp.int32)
    q, k, v, qi, ki, kiwi, kp, vx, kip, ga, u, gs, ma, mb = _inproj(x, mod, g_norm, w_packed, pos)

    t_pad = -(-t // KEY_BLOCK) * KEY_BLOCK
    present = tuple(_pad_rows(z, t_pad) for z in (kip, kp, vx)) if t_pad != t else (kip, kp, vx)
    if past is None:
        past_keys, past_len = None, 0
        h0r = jnp.zeros((b, SSM_LANES), F32)
        h0i = jnp.zeros((b, SSM_LANES), F32)
    else:
        ck, cv, cki, h0r, h0i = past
        past_len = ck.shape[1]
        past_keys = (_pad_lanes(cki, 0.0).astype(BF16),
                     _pad_lanes(ck, 0.0).reshape(b, past_len, 2 * KV_WIDTH).astype(BF16),
                     _pad_lanes(cv, 1.0).reshape(b, past_len, 2 * KV_WIDTH).astype(BF16))
        h0r = h0r.reshape(b, SSM_LANES)
        h0i = h0i.reshape(b, SSM_LANES)

    a = _attend(q, qi, kiwi, ga, present, past_keys, pos0, past_len, t)

    yg_tb, hr, hi = _ssm(jnp.swapaxes(u, 0, 1), h0r, h0i, a2, bbd, cbd, d_skip)
    yg = jnp.swapaxes(yg_tb, 0, 1)

    y = _outproj(x, mod, a, yg, gs, ma, mb, wa, wg, ws, wo, g_final)
    return (y, k.reshape(b, t, N_KV_HEADS, HEAD_DIM), v.reshape(b, t, N_KV_HEADS, HEAD_DIM), ki,
            hr.reshape(b, N_SSM_GROUPS, SSM_STATE), hi.reshape(b, N_SSM_GROUPS, SSM_STATE))


def kernel(x_prompt, x_sample, cache_k, cache_v, cache_idx_k, state_ssm_re, state_ssm_im,
           c_prompt, c_sample, w_mod, b_mod, g_norm, w_in, lambda_re, lambda_im, log_dt,
           ssm_b_re, ssm_b_im, ssm_c_re, ssm_c_im, d_skip, w_glu, w_attn_proj, w_ssm_proj,
           w_out, g_final):
    depth = w_in.shape[0]
    assert depth == 1, "the final norm is fused into the (single) layer's output kernel"
    nbp = x_prompt.shape[0]
    past_len = cache_k.shape[2]
    l = 0

    mod = _modulation(jnp.concatenate([c_prompt, c_sample], axis=0), w_mod[l], b_mod[l])
    mod_p, mod_s = mod[:nbp], mod[nbp:]

    cut = _SEG_OFF["kiwi"][0] + IDX_DIM + N_IDX_HEADS
    w = w_in[l]
    w_packed = jnp.concatenate(
        [w[:, :cut], jnp.zeros((D_MODEL, _KIWI_PAD), w.dtype), w[:, cut:]], axis=1).astype(BF16)

    ar, ai, bbr, bbi = _discretize(lambda_re[l], lambda_im[l], log_dt[l], ssm_b_re[l], ssm_b_im[l])
    eye = jnp.eye(N_SSM_GROUPS, dtype=F32)
    a2 = jnp.stack([ar.reshape(SSM_LANES), ai.reshape(SSM_LANES)])

    def b_diag(bb_cgp):
        return jnp.einsum("cgp,gh->gchp", bb_cgp, eye).reshape(SSM_WIDTH, SSM_LANES)

    def c_diag(c_gcp):
        return jnp.einsum("gcp,gh->gphc", c_gcp, eye).reshape(SSM_LANES, SSM_WIDTH)

    bbd = jnp.concatenate([b_diag(bbr), b_diag(bbi)], axis=1).astype(BF16)
    cbd = jnp.concatenate([c_diag(ssm_c_re[l]), -c_diag(ssm_c_im[l])], axis=0).astype(BF16)

    prm = (g_norm[l], w_packed, a2, bbd, cbd, d_skip[l], w_glu[l].astype(BF16),
           w_attn_proj[l].astype(BF16), w_ssm_proj[l].astype(BF16), w_out[l].astype(BF16), g_final)

    yp, kp, vp, kip, hrp, hip = _layer(x_prompt, mod_p, 0, None, prm)
    past = (cache_k[l], cache_v[l], cache_idx_k[l], state_ssm_re[l], state_ssm_im[l])
    ys, ks, vs, kis, hrs, his = _layer(x_sample, mod_s, past_len, past, prm)

    def st(z):
        return z[None]

    return (yp, ys, st(kp), st(vp), st(kip), st(hrp), st(hip),
            st(ks), st(vs), st(kis), st(hrs), st(his))
```

```python
import functools
import math

import jax
import jax.numpy as jnp
from jax import lax
from jax.experimental import pallas as pl
from jax.experimental.pallas import tpu as pltpu

F32 = jnp.float32
BF16 = jnp.bfloat16

LANES = 128
VMEM_LIMIT = 56 * 1024 * 1024

D_MODEL = 1024
CHUNK = 64
QBLK = 128
N_HEADS = 8
N_KV_HEADS = 2
HEAD_DIM = 64
ATTN_WIDTH = N_HEADS * HEAD_DIM
KV_WIDTH = N_KV_HEADS * HEAD_DIM
ROT_DIM = HEAD_DIM // 4
ROPE_THETA = 500000.0
N_IDX_HEADS = 8
IDX_DIM = 64
IDX_WIDTH = N_IDX_HEADS * IDX_DIM
TOPK_MAX = 256
SSM_WIDTH = D_MODEL // 2
SSM_GROUP = 16
N_SSM_GROUPS = SSM_WIDTH // SSM_GROUP
SSM_STATE = 64
SSM_LANES = N_SSM_GROUPS * SSM_STATE
NORM_EPS = 1e-6

_KIWI_PAD = LANES - IDX_DIM - N_IDX_HEADS
_SEG_WIDTHS = (ATTN_WIDTH, 2 * KV_WIDTH, IDX_WIDTH, LANES, ATTN_WIDTH, SSM_WIDTH, SSM_WIDTH,
               D_MODEL, D_MODEL)
_SEG_NAMES = ("q", "kv", "qi", "kiwi", "ga", "u", "gs", "ma", "mb")
_SEG_OFF = {}
_off = 0
for _n, _w in zip(_SEG_NAMES, _SEG_WIDTHS):
    _SEG_OFF[_n] = (_off, _off + _w)
    _off += _w
IN_PACKED = _off
_WI_COL = IDX_DIM

ROW_TILE = 512
PAD_HEAD = 2 * HEAD_DIM
INT_MIN = -2 ** 31


def _cparams(sem):
    return pltpu.CompilerParams(dimension_semantics=sem, vmem_limit_bytes=VMEM_LIMIT)


def _const_spec(shape):
    nd = len(shape)
    return pl.BlockSpec(shape, lambda *_: (0,) * nd)


def _mod_kernel(c_ref, w_ref, b_ref, o_ref):
    s = jax.nn.silu(c_ref[...])
    o_ref[...] = jnp.dot(s.astype(BF16), w_ref[...], preferred_element_type=F32) + b_ref[...]


def _modulation(c, w_mod, b_mod):
    n = c.shape[0]
    return pl.pallas_call(
        _mod_kernel,
        out_shape=jax.ShapeDtypeStruct((n, 3 * D_MODEL), F32),
        compiler_params=pltpu.CompilerParams(vmem_limit_bytes=VMEM_LIMIT),
        name="mod",
    )(c, w_mod.astype(BF16), b_mod.reshape(1, 3 * D_MODEL))


def _disc_kernel(lre_ref, lim_ref, ldt_ref, bre_ref, bim_ref, ar_ref, ai_ref, bbr_ref, bbi_ref):
    dt = jnp.exp(ldt_ref[...])
    lr, li = lre_ref[...], lim_ref[...]
    mag = jnp.exp(lr * dt)
    ar, ai = mag * jnp.cos(li * dt), mag * jnp.sin(li * dt)
    den = lr * lr + li * li
    zr = ((ar - 1.0) * lr + ai * li) / den
    zi = (ai * lr - (ar - 1.0) * li) / den
    br, bi = bre_ref[...], bim_ref[...]
    ar_ref[...] = ar
    ai_ref[...] = ai
    bbr_ref[...] = zr[None] * br - zi[None] * bi
    bbi_ref[...] = zr[None] * bi + zi[None] * br


def _discretize(lambda_re, lambda_im, log_dt, b_re, b_im):
    g, p, c = b_re.shape
    b_re_t = jnp.transpose(b_re, (2, 0, 1))
    b_im_t = jnp.transpose(b_im, (2, 0, 1))
    return pl.pallas_call(
        _disc_kernel,
        out_shape=(jax.ShapeDtypeStruct((g, p), F32), jax.ShapeDtypeStruct((g, p), F32),
                   jax.ShapeDtypeStruct((c, g, p), F32), jax.ShapeDtypeStruct((c, g, p), F32)),
        name="disc",
    )(lambda_re, lambda_im, log_dt.reshape(g, 1), b_re_t, b_im_t)


def _rope_block(z, cos, sa, sb):
    up = pltpu.roll(z, LANES - ROT_DIM // 2, axis=1)
    dn = pltpu.roll(z, ROT_DIM // 2, axis=1)
    return z * cos + up * sa + dn * sb


def _split_heads(z, fill):
    low = lax.broadcasted_iota(jnp.int32, z.shape, 1) < HEAD_DIM
    return jnp.concatenate([jnp.where(low, z, fill),
                            jnp.where(low, pltpu.roll(z, HEAD_DIM, axis=1), fill)], axis=1)


def _inproj_kernel(x_ref, mod_ref, g_ref, w_ref, rope2_ref, rope1_ref,
                   q_ref, k_ref, v_ref, qi_ref, ki_ref, kiwi_ref, kp_ref, vx_ref, kip_ref,
                   ga_ref, u_ref, gs_ref, ma_ref, mb_ref):
    bb, tt, d = x_ref.shape
    rows = bb * tt
    x = x_ref[...]
    y = x * lax.rsqrt(jnp.mean(x * x, axis=-1, keepdims=True) + NORM_EPS) * g_ref[...]
    shift = mod_ref[:, :, 0:d]
    scale = mod_ref[:, :, d:2 * d]
    h = (y * (1.0 + scale) + shift).reshape(rows, d).astype(BF16)

    def seg(name):
        a, b = _SEG_OFF[name]
        return jnp.dot(h, w_ref[:, a:b], preferred_element_type=F32)

    def roped(z, tab_ref):
        cos, sa, sb = tab_ref[0], tab_ref[1], tab_ref[2]
        blocks = [_rope_block(z[:, i:i + LANES], cos, sa, sb) for i in range(0, z.shape[1], LANES)]
        return blocks[0] if len(blocks) == 1 else jnp.concatenate(blocks, axis=1)

    def put(ref, val):
        ref[...] = val.reshape(ref.shape).astype(ref.dtype)

    def split_all(z, fill=0.0):
        return jnp.concatenate([_split_heads(z[:, i:i + LANES], fill)
                                for i in range(0, z.shape[1], LANES)], axis=1)

    def put_heads(ref, z):
        for hd in range(ref.shape[1]):
            pair = z[:, hd // 2 * LANES:(hd // 2 + 1) * LANES]
            if hd % 2:
                pair = pltpu.roll(pair, HEAD_DIM, axis=1)
            ref[:, hd] = pair[:, :HEAD_DIM].reshape(bb, tt, HEAD_DIM).astype(ref.dtype)

    put_heads(q_ref, roped(seg("q"), rope2_ref) * (HEAD_DIM ** -0.5 * math.log2(math.e)))
    kv = seg("kv")
    k = roped(kv[:, :KV_WIDTH], rope2_ref)
    put(k_ref, k)
    put(kp_ref, split_all(k))
    v = kv[:, KV_WIDTH:]
    put(v_ref, v)
    put(vx_ref, split_all(v, 1.0))
    put_heads(qi_ref, roped(seg("qi"), rope2_ref) * (IDX_DIM ** -0.5))
    kiwi = roped(seg("kiwi"), rope1_ref)
    put(kiwi_ref, kiwi)
    put(ki_ref, kiwi[:, 0:IDX_DIM])
    put(kip_ref, kiwi[:, 0:IDX_DIM])
    put(ga_ref, jax.nn.silu(seg("ga")))
    put(u_ref, seg("u"))
    put(gs_ref, jax.nn.silu(seg("gs")))
    put(ma_ref, jax.nn.sigmoid(seg("ma")))
    put(mb_ref, jax.nn.sigmoid(seg("mb")))


def _rope_tables(pos, heads_in_block):
    half = ROT_DIM // 2
    inv = jnp.power(ROPE_THETA, -jnp.arange(half, dtype=F32) * (2.0 / ROT_DIM))
    ang = pos.astype(F32)[:, None] * inv[None, :]
    cos, sin = jnp.cos(ang), jnp.sin(ang)
    t = pos.shape[0]
    ones = jnp.ones((t, HEAD_DIM - ROT_DIM), F32)
    zeros = jnp.zeros((t, HEAD_DIM - ROT_DIM), F32)
    zh = jnp.zeros((t, half), F32)
    c_head = jnp.concatenate([cos, cos, ones], axis=1)
    sa_head = jnp.concatenate([-sin, zh, zeros], axis=1)
    sb_head = jnp.concatenate([zh, sin, zeros], axis=1)
    n_id = LANES // HEAD_DIM - heads_in_block
    ident = [jnp.ones((t, HEAD_DIM), F32)] * n_id
    zero = [jnp.zeros((t, HEAD_DIM), F32)] * n_id
    c = jnp.concatenate([c_head] * heads_in_block + ident, axis=1)
    sa = jnp.concatenate([sa_head] * heads_in_block + zero, axis=1)
    sb = jnp.concatenate([sb_head] * heads_in_block + zero, axis=1)
    return jnp.stack([c, sa, sb])


def _row_blocking(b, t):
    tt = min(t, ROW_TILE)
    bb = ROW_TILE // tt
    assert t % tt == 0 and b % bb == 0 and tt % 16 == 0
    return bb, tt


def _inproj(x, mod, g_norm, w_packed, pos):
    b, t, d = x.shape
    bb, tt = _row_blocking(b, t)
    rows = bb * tt
    rope2 = jnp.tile(_rope_tables(pos, 2), (1, bb, 1)) if bb > 1 else _rope_tables(pos, 2)
    rope1 = jnp.tile(_rope_tables(pos, 1), (1, bb, 1)) if bb > 1 else _rope_tables(pos, 1)

    def tok(width, dtype):
        return (jax.ShapeDtypeStruct((b, t, width), dtype),
                pl.BlockSpec((bb, tt, width), lambda i, j: (i, j, 0)))

    def heads(n):
        return (jax.ShapeDtypeStruct((b, n, t, HEAD_DIM), BF16),
                pl.BlockSpec((bb, n, tt, HEAD_DIM), lambda i, j: (i, 0, j, 0)))

    outs = [heads(N_HEADS), tok(KV_WIDTH, F32), tok(KV_WIDTH, F32),
            heads(N_IDX_HEADS), tok(IDX_DIM, F32), tok(LANES, F32),
            tok(2 * KV_WIDTH, BF16), tok(2 * KV_WIDTH, BF16), tok(IDX_DIM, BF16),
            tok(ATTN_WIDTH, BF16), tok(SSM_WIDTH, BF16),
            tok(SSM_WIDTH, BF16), tok(D_MODEL, BF16), tok(D_MODEL, BF16)]
    return pl.pallas_call(
        _inproj_kernel,
        grid=(b // bb, t // tt),
        in_specs=[
            pl.BlockSpec((bb, tt, d), lambda i, j: (i, j, 0)),
            pl.BlockSpec((bb, 1, 3 * d), lambda i, j: (i, 0, 0)),
            _const_spec((1, 1, d)),
            pl.BlockSpec((d, IN_PACKED), lambda i, j: (0, 0), pipeline_mode=pl.Buffered(1)),
            pl.BlockSpec((3, rows, LANES), lambda i, j: (0, j, 0)),
            pl.BlockSpec((3, rows, LANES), lambda i, j: (0, j, 0)),
        ],
        out_specs=[o[1] for o in outs],
        out_shape=[o[0] for o in outs],
        compiler_params=_cparams(("parallel", "parallel")),
        name="inproj",
    )(x, mod.reshape(b, 1, 3 * d), g_norm.reshape(1, 1, d), w_packed, rope2, rope1)


KEY_BLOCK = 256
ATTEND_ROWS = 512


def _float_of_key(key):
    bits = jnp.where(key >= 0, key, key ^ jnp.int32(0x7FFFFFFF))
    return lax.bitcast_convert_type(bits, F32)


def _count(sc_ref, bi, nkb, pred):
    acc = None
    for blk in range(nkb):
        hit = jnp.where(pred(sc_ref[bi, blk]), 1.0, 0.0)
        part = hit[:, :LANES] + hit[:, LANES:]
        acc = part if acc is None else acc + part
    return jnp.sum(acc, axis=1, keepdims=True)


def _search_thresholds(sc_ref, thr_ref, need_ref, kf, nkb):
    nb, _, rows, _ = sc_ref.shape

    def body(i, keys):
        bit = jnp.left_shift(jnp.int32(1), 31 - i)
        out = []
        for bi, key in enumerate(keys):
            cand = key ^ bit
            cf = _float_of_key(cand)
            cnt = _count(sc_ref, bi, nkb, lambda x, cf=cf: x >= cf)
            out.append(jnp.where(cnt >= kf, cand, key))
        return tuple(out)

    init = tuple(jnp.full((rows, 1), INT_MIN, jnp.int32) for _ in range(nb))
    keys = lax.fori_loop(0, 32, body, init)
    for bi, key in enumerate(keys):
        thr = _float_of_key(key)
        need = kf - _count(sc_ref, bi, nkb, lambda x, thr=thr: x > thr)
        thr_ref[bi] = jnp.broadcast_to(thr, thr_ref.shape[1:])
        need_ref[bi] = jnp.broadcast_to(need, need_ref.shape[1:])


def _attend_kernel(*refs, has_past, variants, search_from, q_pos0, past_len, present_len, n_sel):
    q_ref, qi_ref, kiwi_ref, ga_ref = refs[:4]
    n_in = 10 if has_past else 7
    past = refs[4:7] if has_past else None
    present = refs[n_in - 3:n_in]
    o_ref = refs[n_in]
    sc_ref, thr_ref, need_ref, macc_ref, oacc_ref = refs[n_in + 1:]
    nb, tq = q_ref.shape[0], q_ref.shape[2]
    kb = KEY_BLOCK
    n_past = past_len // kb
    nt = (((1,), (1,)), ((), ()))
    rep = N_HEADS // N_KV_HEADS
    j = pl.program_id(1)

    qpos = q_pos0 + j * tq + lax.broadcasted_iota(jnp.int32, (tq, 1), 0)
    qchunk = qpos // CHUNK
    n_present = jnp.minimum((qchunk + 1) * CHUNK - past_len, present_len)
    kf = jnp.minimum(past_len + n_present, n_sel).astype(F32)
    last_chunk = (q_pos0 + (j + 1) * tq - 1) // CHUNK
    nkb_present = (jnp.minimum((last_chunk + 1) * CHUNK - past_len, present_len) + kb - 1) // kb

    def admissible(i):
        local = i * kb + lax.broadcasted_iota(jnp.int32, (1, kb), 1)
        return (((past_len + local) // CHUNK) <= qchunk) & (local < present_len)

    def for_blocks(fn, carry=0):
        if n_past:
            carry = lax.fori_loop(0, n_past, lambda i, c: fn(past, i, i, c), carry)
        return lax.fori_loop(0, nkb_present, lambda i, c: fn(present, i, n_past + i, c), carry)

    def rows_of(i):
        return pl.ds(pl.multiple_of(i * kb, kb), kb)

    low = lax.broadcasted_iota(jnp.int32, (kb, LANES), 1) < HEAD_DIM

    def keys_values(src, bi, i):
        kblk, vblk = src[1][bi, rows_of(i), :], src[2][bi, rows_of(i), :]
        if src is present:
            return [(kblk[:, g * PAD_HEAD:g * PAD_HEAD + HEAD_DIM],
                     vblk[:, g * PAD_HEAD:(g + 1) * PAD_HEAD]) for g in range(N_KV_HEADS)]
        out = []
        for g in range(N_KV_HEADS):
            kg = kblk if g == 0 else pltpu.roll(kblk, LANES - g * HEAD_DIM, axis=1)
            vg = vblk if g == 0 else pltpu.roll(vblk, LANES - g * HEAD_DIM, axis=1)
            out.append((kg[:, :HEAD_DIM].astype(BF16), jnp.where(low, vg, 1.0).astype(BF16)))
        return out

    def search_path():
        qi_rows = [qi_ref[bi].reshape(N_IDX_HEADS * tq, IDX_DIM) for bi in range(nb)]
        wts = [[jnp.broadcast_to(kiwi_ref[bi, :, _WI_COL + h:_WI_COL + h + 1]
                                 * (N_IDX_HEADS ** -0.5), (tq, kb))
                for h in range(N_IDX_HEADS)] for bi in range(nb)]

        def score_block(src, i, blk, c):
            for bi in range(nb):
                s = lax.dot_general(qi_rows[bi], src[0][bi, rows_of(i), :].astype(BF16), nt,
                                    preferred_element_type=F32)
                acc = None
                for h in range(N_IDX_HEADS):
                    term = wts[bi][h] * jnp.maximum(s[h * tq:(h + 1) * tq], 0.0)
                    acc = term if acc is None else acc + term
                if src is present:
                    acc = jnp.where(admissible(i), acc, -jnp.inf)
                sc_ref[bi, blk] = acc
            return c

        for_blocks(score_block)

        for lo, hi, nkb in variants:
            if len(variants) == 1:
                _search_thresholds(sc_ref, thr_ref, need_ref, kf, nkb)
            else:
                pl.when((j >= lo) & (j < hi))(
                    functools.partial(_search_thresholds, sc_ref, thr_ref, need_ref, kf, nkb))

        tri = (lax.broadcasted_iota(jnp.int32, (kb, kb), 0)
               <= lax.broadcasted_iota(jnp.int32, (kb, kb), 1)).astype(BF16)
        ones = jnp.ones((kb, LANES), BF16)

        def twice(a):
            return jnp.concatenate([a, a], axis=1)

        def bias_block(src, i, blk, seen):
            out = []
            for bi in range(nb):
                x = sc_ref[bi, blk]
                thr = twice(thr_ref[bi])
                eq = x == thr
                e = jnp.where(eq, 1.0, 0.0).astype(BF16)
                rank = jnp.dot(e, tri, preferred_element_type=F32) + twice(seen[bi])
                tie = jnp.where(rank <= twice(need_ref[bi]), 0.0, -jnp.inf)
                sc_ref[bi, blk] = jnp.where(eq, tie, jnp.where(x > thr, 0.0, -jnp.inf))
                out.append(seen[bi] + jnp.dot(e, ones, preferred_element_type=F32))
            return tuple(out)

        for_blocks(bias_block, tuple(jnp.zeros((tq, LANES), F32) for _ in range(nb)))

    def all_admissible_path():
        def bias_block(src, i, blk, c):
            for bi in range(nb):
                if src is present:
                    sc_ref[bi, blk] = jnp.where(admissible(i), 0.0, -jnp.inf)
                else:
                    sc_ref[bi, blk] = jnp.zeros((tq, kb), F32)
            return c

        for_blocks(bias_block)

    if search_from == 0:
        search_path()
    else:
        pl.when(j >= search_from)(search_path)
        pl.when(j < search_from)(all_admissible_path)

    q_rows = [[q_ref[bi, g * rep:(g + 1) * rep].reshape(rep * tq, HEAD_DIM)
               for g in range(N_KV_HEADS)] for bi in range(nb)]

    macc_ref[...] = jnp.full(macc_ref.shape, jnp.finfo(F32).min, F32)
    oacc_ref[...] = jnp.zeros(oacc_ref.shape, F32)

    def pv_block(src, i, blk, c):
        for bi in range(nb):
            bias = sc_ref[bi, blk]
            for g, (kg, vxg) in enumerate(keys_values(src, bi, i)):
                lg = lax.dot_general(q_rows[bi][g], kg, nt, preferred_element_type=F32)
                lg = (lg.reshape(rep, tq, kb) + bias[None]).reshape(rep * tq, kb)
                m_old = macc_ref[bi, g]
                row_max = jnp.max(jnp.maximum(lg[:, :LANES], lg[:, LANES:]), axis=1, keepdims=True)
                m_new = jnp.maximum(m_old, row_max)
                p = jnp.exp2(lg - jnp.concatenate([m_new, m_new], axis=1))
                oacc_ref[bi, g] = (oacc_ref[bi, g] * jnp.exp2(m_old - m_new)
                                   + jnp.dot(p.astype(BF16), vxg, preferred_element_type=F32))
                macc_ref[bi, g] = m_new
        return c

    for_blocks(pv_block)

    low = lax.broadcasted_iota(jnp.int32, (tq, LANES), 1) < HEAD_DIM
    for bi in range(nb):
        outs = []
        for g in range(N_KV_HEADS):
            acc = oacc_ref[bi, g]
            o = acc / pltpu.roll(acc, HEAD_DIM, axis=1)
            for h in range(0, rep, 2):
                outs.append(jnp.where(low, o[h * tq:(h + 1) * tq],
                                      pltpu.roll(o[(h + 1) * tq:(h + 2) * tq], HEAD_DIM, axis=1)))
        o_all = jnp.concatenate(outs, axis=1)
        o_ref[bi] = (o_all * ga_ref[bi].astype(F32)).astype(o_ref.dtype)


def _attend_variants(nq, tq, q_pos0, past_len, present_len, n_sel):
    kb = KEY_BLOCK
    variants, search_from = [], nq
    for j in range(nq):
        n_present = min(((q_pos0 + (j + 1) * tq - 1) // CHUNK + 1) * CHUNK - past_len, present_len)
        if past_len + n_present <= n_sel:
            assert not variants
            continue
        search_from = min(search_from, j)
        nkb = past_len // kb + -(-n_present // kb)
        if variants and variants[-1][2] == nkb:
            variants[-1] = (variants[-1][0], j + 1, nkb)
        else:
            variants.append((j, j + 1, nkb))
    return search_from, tuple(variants)


def _attend(q, qi, kiwi, ga, present, past, q_pos0, past_len, present_len):
    b, _, t, _ = q.shape
    kb = KEY_BLOCK
    tq = QBLK if t % QBLK == 0 else t
    nb = max(1, ATTEND_ROWS // tq)
    l_present = present[0].shape[1]
    assert b % nb == 0 and past_len % kb == 0 and l_present % kb == 0 and q_pos0 >= past_len
    n_sel = min(TOPK_MAX, (past_len + present_len) // 4)
    search_from, variants = _attend_variants(t // tq, tq, q_pos0, past_len, present_len, n_sel)
    nkb_max = past_len // kb + l_present // kb
    kern = functools.partial(_attend_kernel, has_past=past is not None, variants=variants,
                             search_from=search_from, q_pos0=q_pos0, past_len=past_len,
                             present_len=present_len, n_sel=n_sel)

    def qspec(width):
        return pl.BlockSpec((nb, tq, width), lambda i, j: (i, j, 0))

    def hspec(a):
        return pl.BlockSpec((nb, a.shape[1], tq, HEAD_DIM), lambda i, j: (i, 0, j, 0))

    def kspec(a):
        return pl.BlockSpec((nb,) + a.shape[1:], lambda i, j: (i, 0, 0))

    keys = (tuple(past) if past is not None else ()) + tuple(present)
    return pl.pallas_call(
        kern,
        grid=(b // nb, t // tq),
        in_specs=[hspec(q), hspec(qi), qspec(LANES), qspec(ATTN_WIDTH)]
        + [kspec(a) for a in keys],
        out_specs=qspec(ATTN_WIDTH),
        out_shape=jax.ShapeDtypeStruct((b, t, ATTN_WIDTH), BF16),
        scratch_shapes=[pltpu.VMEM((nb, nkb_max, tq, kb), F32),
                        pltpu.VMEM((nb, tq, LANES), F32), pltpu.VMEM((nb, tq, LANES), F32),
                        pltpu.VMEM((nb, N_KV_HEADS, N_HEADS // N_KV_HEADS * tq, LANES), F32),
                        pltpu.VMEM((nb, N_KV_HEADS, N_HEADS // N_KV_HEADS * tq, LANES), F32)],
        compiler_params=_cparams(("parallel", "parallel")),
        name="attend",
    )(q, qi, kiwi, ga, *keys)


SSM_SLAB = LANES


def _ssm_kernel(u_ref, h0r_ref, h0i_ref, a_ref, bbd_ref, cbd_ref, d_ref,
                y_ref, hr_ref, hi_ref, s_ref, st_ref):
    tt, nb, w = u_ref.shape
    rows = tt * nb
    n = SSM_LANES

    @pl.when(pl.program_id(0) == 0)
    def _():
        st_ref[0] = h0r_ref[...]
        st_ref[1] = h0i_ref[...]

    u2 = u_ref[...].reshape(rows, w)
    lanes_per_slab = SSM_SLAB // SSM_GROUP * SSM_STATE
    ys = []
    for q in range(w // SSM_SLAB):
        ch = slice(q * SSM_SLAB, (q + 1) * SSM_SLAB)
        re = slice(q * lanes_per_slab, (q + 1) * lanes_per_slab)
        im = slice(n + q * lanes_per_slab, n + (q + 1) * lanes_per_slab)
        s_ref[:, re] = jnp.dot(u2[:, ch], bbd_ref[ch, re], preferred_element_type=F32)
        s_ref[:, im] = jnp.dot(u2[:, ch], bbd_ref[ch, im], preferred_element_type=F32)
        ar = jnp.broadcast_to(a_ref[0:1, re], (nb, lanes_per_slab))
        ai = jnp.broadcast_to(a_ref[1:2, re], (nb, lanes_per_slab))
        hr, hi = st_ref[0, :, re], st_ref[1, :, re]
        for t in range(tt):
            rws = slice(t * nb, (t + 1) * nb)
            hr, hi = (ar * hr - ai * hi + s_ref[rws, re], ar * hi + ai * hr + s_ref[rws, im])
            s_ref[rws, re] = hr
            s_ref[rws, im] = hi
        st_ref[0, :, re] = hr
        st_ref[1, :, re] = hi
        ys.append(jnp.dot(s_ref[:, re].astype(BF16), cbd_ref[re, ch], preferred_element_type=F32)
                  + jnp.dot(s_ref[:, im].astype(BF16), cbd_ref[im, ch], preferred_element_type=F32))
    y = jnp.concatenate(ys, axis=1)
    y = y + d_ref[...] * u2.astype(F32)
    y_ref[...] = jax.nn.gelu(y).astype(y_ref.dtype).reshape(tt, nb, w)
    hr_ref[...] = st_ref[0]
    hi_ref[...] = st_ref[1]


def _ssm(u_tb, h0r, h0i, a2, bbd, cbd, d_skip):
    t, nb, w = u_tb.shape
    tt = max(1, 512 // nb)
    assert t % tt == 0
    n = SSM_LANES
    return pl.pallas_call(
        _ssm_kernel,
        grid=(t // tt,),
        in_specs=[pl.BlockSpec((tt, nb, w), lambda i: (i, 0, 0)),
                  _const_spec((nb, n)), _const_spec((nb, n)), _const_spec((2, n)),
                  _const_spec((w, 2 * n)), _const_spec((2 * n, w)), _const_spec((1, w))],
        out_specs=[pl.BlockSpec((tt, nb, w), lambda i: (i, 0, 0)),
                   _const_spec((nb, n)), _const_spec((nb, n))],
        out_shape=[jax.ShapeDtypeStruct((t, nb, w), BF16),
                   jax.ShapeDtypeStruct((nb, n), F32), jax.ShapeDtypeStruct((nb, n), F32)],
        scratch_shapes=[pltpu.VMEM((tt * nb, 2 * n), F32), pltpu.VMEM((2, nb, n), F32)],
        compiler_params=_cparams(("arbitrary",)),
        name="ssm",
    )(u_tb, h0r, h0i, a2, bbd, cbd, d_skip.reshape(1, w))


def _outproj_kernel(x_ref, mod_ref, a_ref, yg_ref, gs_ref, ma_ref, mb_ref,
                    wa_ref, wg_ref, ws_ref, wo_ref, gf_ref, y_ref):
    bb, tt, d = x_ref.shape
    rows = bb * tt

    def flat(ref):
        return ref[...].reshape(rows, ref.shape[-1])

    branch_a = jnp.dot(flat(a_ref), wa_ref[...], preferred_element_type=F32)
    g_lin = jnp.dot(flat(yg_ref), wg_ref[...], preferred_element_type=F32)
    y_glu = g_lin[:, :SSM_WIDTH] * jax.nn.sigmoid(g_lin[:, SSM_WIDTH:])
    gated = y_glu * flat(gs_ref).astype(F32)
    branch_b = jnp.dot(gated.astype(BF16), ws_ref[...], preferred_element_type=F32)
    merged = flat(ma_ref).astype(F32) * branch_a + flat(mb_ref).astype(F32) * branch_b
    proj = jnp.dot(merged.astype(BF16), wo_ref[...], preferred_element_type=F32)
    gate = mod_ref[:, :, 2 * d:3 * d]
    xo = x_ref[...] + gate * proj.reshape(bb, tt, d)
    y = xo * lax.rsqrt(jnp.mean(xo * xo, axis=-1, keepdims=True) + NORM_EPS) * gf_ref[...]
    y_ref[...] = y


def _outproj(x, mod, a, yg, gs, ma, mb, wa, wg, ws, wo, g_final):
    b, t, d = x.shape
    bb, tt = _row_blocking(b, t)

    def tok(width):
        return pl.BlockSpec((bb, tt, width), lambda i, j: (i, j, 0))

    return pl.pallas_call(
        _outproj_kernel,
        grid=(b // bb, t // tt),
        in_specs=[tok(d), pl.BlockSpec((bb, 1, 3 * d), lambda i, j: (i, 0, 0)),
                  tok(ATTN_WIDTH), tok(SSM_WIDTH), tok(SSM_WIDTH), tok(d), tok(d),
                  _const_spec(wa.shape), _const_spec(wg.shape), _const_spec(ws.shape),
                  _const_spec(wo.shape), _const_spec((1, 1, d))],
        out_specs=tok(d),
        out_shape=jax.ShapeDtypeStruct((b, t, d), F32),
        compiler_params=_cparams(("parallel", "parallel")),
        name="outproj",
    )(x, mod.reshape(b, 1, 3 * d), a, yg, gs, ma, mb, wa, wg, ws, wo, g_final.reshape(1, 1, d))


def _pad_rows(a, rows):
    return jnp.pad(a, ((0, 0), (0, rows - a.shape[1]), (0, 0)))


def _layer(x, mod, pos0, past, prm):
    (g_norm, w_packed, a2, bbd, cbd, d_skip, wg, wa, ws, wo, g_final) = prm
    b, t, _ = x.shape
    pos = pos0 + jnp.arange(t, dtype=jnp.int32)
    q, k, v, qi, ki, kiwi, kp, vx, kip, ga, u, gs, ma, mb = _inproj(x, mod, g_norm, w_packed, pos)

    t_pad = -(-t // KEY_BLOCK) * KEY_BLOCK
    present = tuple(_pad_rows(z, t_pad) for z in (kip, kp, vx)) if t_pad != t else (kip, kp, vx)
    if past is None:
        past_keys, past_len = None, 0
        h0r = jnp.zeros((b, SSM_LANES), F32)
        h0i = jnp.zeros((b, SSM_LANES), F32)
    else:
        ck, cv, cki, h0r, h0i = past
        past_len = ck.shape[1]
        past_keys = (cki, ck.reshape(b, past_len, KV_WIDTH), cv.reshape(b, past_len, KV_WIDTH))
        h0r = h0r.reshape(b, SSM_LANES)
        h0i = h0i.reshape(b, SSM_LANES)

    a = _attend(q, qi, kiwi, ga, present, past_keys, pos0, past_len, t)

    yg_tb, hr, hi = _ssm(jnp.swapaxes(u, 0, 1), h0r, h0i, a2, bbd, cbd, d_skip)
    yg = jnp.swapaxes(yg_tb, 0, 1)

    y = _outproj(x, mod, a, yg, gs, ma, mb, wa, wg, ws, wo, g_final)
    return (y, k.reshape(b, t, N_KV_HEADS, HEAD_DIM), v.reshape(b, t, N_KV_HEADS, HEAD_DIM), ki,
            hr.reshape(b, N_SSM_GROUPS, SSM_STATE), hi.reshape(b, N_SSM_GROUPS, SSM_STATE))


def kernel(x_prompt, x_sample, cache_k, cache_v, cache_idx_k, state_ssm_re, state_ssm_im,
           c_prompt, c_sample, w_mod, b_mod, g_norm, w_in, lambda_re, lambda_im, log_dt,
           ssm_b_re, ssm_b_im, ssm_c_re, ssm_c_im, d_skip, w_glu, w_attn_proj, w_ssm_proj,
           w_out, g_final):
    depth = w_in.shape[0]
    assert depth == 1, "the final norm is fused into the (single) layer's output kernel"
    nbp = x_prompt.shape[0]
    past_len = cache_k.shape[2]
    l = 0

    mod = _modulation(jnp.concatenate([c_prompt, c_sample], axis=0), w_mod[l], b_mod[l])
    mod_p, mod_s = mod[:nbp], mod[nbp:]

    cut = _SEG_OFF["kiwi"][0] + IDX_DIM + N_IDX_HEADS
    w = w_in[l]
    w_packed = jnp.concatenate(
        [w[:, :cut], jnp.zeros((D_MODEL, _KIWI_PAD), w.dtype), w[:, cut:]], axis=1).astype(BF16)

    ar, ai, bbr, bbi = _discretize(lambda_re[l], lambda_im[l], log_dt[l], ssm_b_re[l], ssm_b_im[l])
    eye = jnp.eye(N_SSM_GROUPS, dtype=F32)
    a2 = jnp.stack([ar.reshape(SSM_LANES), ai.reshape(SSM_LANES)])

    def b_diag(bb_cgp):
        return jnp.einsum("cgp,gh->gchp", bb_cgp, eye).reshape(SSM_WIDTH, SSM_LANES)

    def c_diag(c_gcp):
        return jnp.einsum("gcp,gh->gphc", c_gcp, eye).reshape(SSM_LANES, SSM_WIDTH)

    bbd = jnp.concatenate([b_diag(bbr), b_diag(bbi)], axis=1).astype(BF16)
    cbd = jnp.concatenate([c_diag(ssm_c_re[l]), -c_diag(ssm_c_im[l])], axis=0).astype(BF16)

    prm = (g_norm[l], w_packed, a2, bbd, cbd, d_skip[l], w_glu[l].astype(BF16),
           w_attn_proj[l].astype(BF16), w_ssm_proj[l].astype(BF16), w_out[l].astype(BF16), g_final)

    yp, kp, vp, kip, hrp, hip = _layer(x_prompt, mod_p, 0, None, prm)
    past = (cache_k[l], cache_v[l], cache_idx_k[l], state_ssm_re[l], state_ssm_im[l])
    ys, ks, vs, kis, hrs, his = _layer(x_sample, mod_s, past_len, past, prm)

    def st(z):
        return z[None]

    return (yp, ys, st(kp), st(vp), st(kip), st(hrp), st(hip),
            st(ks), st(vs), st(kis), st(hrs), st(his))
```

```python
import functools
import math

import jax
import jax.numpy as jnp
from jax import lax
from jax.experimental import pallas as pl
from jax.experimental.pallas import tpu as pltpu

F32 = jnp.float32
BF16 = jnp.bfloat16

LANES = 128
VMEM_LIMIT = 56 * 1024 * 1024

D_MODEL = 1024
CHUNK = 64
QBLK = 128
N_HEADS = 8
N_KV_HEADS = 2
HEAD_DIM = 64
ATTN_WIDTH = N_HEADS * HEAD_DIM
KV_WIDTH = N_KV_HEADS * HEAD_DIM
ROT_DIM = HEAD_DIM // 4
ROPE_THETA = 500000.0
N_IDX_HEADS = 8
IDX_DIM = 64
IDX_WIDTH = N_IDX_HEADS * IDX_DIM
TOPK_MAX = 256
SSM_WIDTH = D_MODEL // 2
SSM_GROUP = 16
N_SSM_GROUPS = SSM_WIDTH // SSM_GROUP
SSM_STATE = 64
SSM_LANES = N_SSM_GROUPS * SSM_STATE
NORM_EPS = 1e-6

_KIWI_PAD = LANES - IDX_DIM - N_IDX_HEADS
_SEG_WIDTHS = (ATTN_WIDTH, 2 * KV_WIDTH, IDX_WIDTH, LANES, ATTN_WIDTH, SSM_WIDTH, SSM_WIDTH,
               D_MODEL, D_MODEL)
_SEG_NAMES = ("q", "kv", "qi", "kiwi", "ga", "u", "gs", "ma", "mb")
_SEG_OFF = {}
_off = 0
for _n, _w in zip(_SEG_NAMES, _SEG_WIDTHS):
    _SEG_OFF[_n] = (_off, _off + _w)
    _off += _w
IN_PACKED = _off
_WI_COL = IDX_DIM

ROW_TILE = 512
PAD_HEAD = 2 * HEAD_DIM


def _cparams(sem):
    return pltpu.CompilerParams(dimension_semantics=sem, vmem_limit_bytes=VMEM_LIMIT)


def _const_spec(shape):
    nd = len(shape)
    return pl.BlockSpec(shape, lambda *_: (0,) * nd)


def _mod_kernel(c_ref, w_ref, b_ref, o_ref):
    s = jax.nn.silu(c_ref[...])
    o_ref[...] = jnp.dot(s.astype(BF16), w_ref[...], preferred_element_type=F32) + b_ref[...]


def _modulation(c, w_mod, b_mod):
    n = c.shape[0]
    return pl.pallas_call(
        _mod_kernel,
        out_shape=jax.ShapeDtypeStruct((n, 3 * D_MODEL), F32),
        compiler_params=pltpu.CompilerParams(vmem_limit_bytes=VMEM_LIMIT),
        name="mod",
    )(c, w_mod.astype(BF16), b_mod.reshape(1, 3 * D_MODEL))


def _disc_kernel(lre_ref, lim_ref, ldt_ref, bre_ref, bim_ref, ar_ref, ai_ref, bbr_ref, bbi_ref):
    dt = jnp.exp(ldt_ref[...])
    lr, li = lre_ref[...], lim_ref[...]
    mag = jnp.exp(lr * dt)
    ar, ai = mag * jnp.cos(li * dt), mag * jnp.sin(li * dt)
    den = lr * lr + li * li
    zr = ((ar - 1.0) * lr + ai * li) / den
    zi = (ai * lr - (ar - 1.0) * li) / den
    br, bi = bre_ref[...], bim_ref[...]
    ar_ref[...] = ar
    ai_ref[...] = ai
    bbr_ref[...] = zr[None] * br - zi[None] * bi
    bbi_ref[...] = zr[None] * bi + zi[None] * br


def _discretize(lambda_re, lambda_im, log_dt, b_re, b_im):
    g, p, c = b_re.shape
    b_re_t = jnp.transpose(b_re, (2, 0, 1))
    b_im_t = jnp.transpose(b_im, (2, 0, 1))
    return pl.pallas_call(
        _disc_kernel,
        out_shape=(jax.ShapeDtypeStruct((g, p), F32), jax.ShapeDtypeStruct((g, p), F32),
                   jax.ShapeDtypeStruct((c, g, p), F32), jax.ShapeDtypeStruct((c, g, p), F32)),
        name="disc",
    )(lambda_re, lambda_im, log_dt.reshape(g, 1), b_re_t, b_im_t)


def _rope_block(z, cos, sa, sb):
    up = pltpu.roll(z, LANES - ROT_DIM // 2, axis=1)
    dn = pltpu.roll(z, ROT_DIM // 2, axis=1)
    return z * cos + up * sa + dn * sb


def _split_heads(z, fill):
    low = lax.broadcasted_iota(jnp.int32, z.shape, 1) < HEAD_DIM
    return jnp.concatenate([jnp.where(low, z, fill),
                            jnp.where(low, pltpu.roll(z, HEAD_DIM, axis=1), fill)], axis=1)


def _inproj_kernel(x_ref, mod_ref, g_ref, w_ref, rope2_ref, rope1_ref,
                   q_ref, k_ref, v_ref, qi_ref, ki_ref, kiwi_ref, kp_ref, vx_ref, kip_ref,
                   ga_ref, u_ref, gs_ref, ma_ref, mb_ref):
    bb, tt, d = x_ref.shape
    rows = bb * tt
    x = x_ref[...]
    y = x * lax.rsqrt(jnp.mean(x * x, axis=-1, keepdims=True) + NORM_EPS) * g_ref[...]
    shift = mod_ref[:, :, 0:d]
    scale = mod_ref[:, :, d:2 * d]
    h = (y * (1.0 + scale) + shift).reshape(rows, d).astype(BF16)

    def seg(name):
        a, b = _SEG_OFF[name]
        return jnp.dot(h, w_ref[:, a:b], preferred_element_type=F32)

    def roped(z, tab_ref):
        cos, sa, sb = tab_ref[0], tab_ref[1], tab_ref[2]
        blocks = [_rope_block(z[:, i:i + LANES], cos, sa, sb) for i in range(0, z.shape[1], LANES)]
        return blocks[0] if len(blocks) == 1 else jnp.concatenate(blocks, axis=1)

    def put(ref, val):
        ref[...] = val.reshape(ref.shape).astype(ref.dtype)

    def split_all(z, fill=0.0):
        return jnp.concatenate([_split_heads(z[:, i:i + LANES], fill)
                                for i in range(0, z.shape[1], LANES)], axis=1)

    def put_heads(ref, z):
        for hd in range(ref.shape[1]):
            pair = z[:, hd // 2 * LANES:(hd // 2 + 1) * LANES]
            if hd % 2:
                pair = pltpu.roll(pair, HEAD_DIM, axis=1)
            ref[:, hd] = pair[:, :HEAD_DIM].reshape(bb, tt, HEAD_DIM).astype(ref.dtype)

    put_heads(q_ref, roped(seg("q"), rope2_ref) * (HEAD_DIM ** -0.5 * math.log2(math.e)))
    kv = seg("kv")
    k = roped(kv[:, :KV_WIDTH], rope2_ref)
    put(k_ref, k)
    put(kp_ref, split_all(k))
    v = kv[:, KV_WIDTH:]
    put(v_ref, v)
    put(vx_ref, split_all(v, 1.0))
    put_heads(qi_ref, roped(seg("qi"), rope2_ref) * (IDX_DIM ** -0.5))
    kiwi = roped(seg("kiwi"), rope1_ref)
    put(kiwi_ref, kiwi)
    put(ki_ref, kiwi[:, 0:IDX_DIM])
    put(kip_ref, kiwi[:, 0:IDX_DIM])
    put(ga_ref, jax.nn.silu(seg("ga")))
    put(u_ref, seg("u"))
    put(gs_ref, jax.nn.silu(seg("gs")))
    put(ma_ref, jax.nn.sigmoid(seg("ma")))
    put(mb_ref, jax.nn.sigmoid(seg("mb")))


def _rope_tables(pos, heads_in_block):
    half = ROT_DIM // 2
    inv = jnp.power(ROPE_THETA, -jnp.arange(half, dtype=F32) * (2.0 / ROT_DIM))
    ang = pos.astype(F32)[:, None] * inv[None, :]
    cos, sin = jnp.cos(ang), jnp.sin(ang)
    t = pos.shape[0]
    ones = jnp.ones((t, HEAD_DIM - ROT_DIM), F32)
    zeros = jnp.zeros((t, HEAD_DIM - ROT_DIM), F32)
    zh = jnp.zeros((t, half), F32)
    c_head = jnp.concatenate([cos, cos, ones], axis=1)
    sa_head = jnp.concatenate([-sin, zh, zeros], axis=1)
    sb_head = jnp.concatenate([zh, sin, zeros], axis=1)
    n_id = LANES // HEAD_DIM - heads_in_block
    ident = [jnp.ones((t, HEAD_DIM), F32)] * n_id
    zero = [jnp.zeros((t, HEAD_DIM), F32)] * n_id
    c = jnp.concatenate([c_head] * heads_in_block + ident, axis=1)
    sa = jnp.concatenate([sa_head] * heads_in_block + zero, axis=1)
    sb = jnp.concatenate([sb_head] * heads_in_block + zero, axis=1)
    return jnp.stack([c, sa, sb])


def _row_blocking(b, t):
    tt = min(t, ROW_TILE)
    bb = ROW_TILE // tt
    assert t % tt == 0 and b % bb == 0 and tt % 16 == 0
    return bb, tt


def _inproj(x, mod, g_norm, w_packed, pos):
    b, t, d = x.shape
    bb, tt = _row_blocking(b, t)
    rows = bb * tt
    rope2 = jnp.tile(_rope_tables(pos, 2), (1, bb, 1)) if bb > 1 else _rope_tables(pos, 2)
    rope1 = jnp.tile(_rope_tables(pos, 1), (1, bb, 1)) if bb > 1 else _rope_tables(pos, 1)

    def tok(width, dtype):
        return (jax.ShapeDtypeStruct((b, t, width), dtype),
                pl.BlockSpec((bb, tt, width), lambda i, j: (i, j, 0)))

    def heads(n):
        return (jax.ShapeDtypeStruct((b, n, t, HEAD_DIM), BF16),
                pl.BlockSpec((bb, n, tt, HEAD_DIM), lambda i, j: (i, 0, j, 0)))

    outs = [heads(N_HEADS), tok(KV_WIDTH, F32), tok(KV_WIDTH, F32),
            heads(N_IDX_HEADS), tok(IDX_DIM, F32), tok(LANES, F32),
            tok(2 * KV_WIDTH, BF16), tok(2 * KV_WIDTH, BF16), tok(IDX_DIM, BF16),
            tok(ATTN_WIDTH, BF16), tok(SSM_WIDTH, BF16),
            tok(SSM_WIDTH, BF16), tok(D_MODEL, BF16), tok(D_MODEL, BF16)]
    return pl.pallas_call(
        _inproj_kernel,
        grid=(b // bb, t // tt),
        in_specs=[
            pl.BlockSpec((bb, tt, d), lambda i, j: (i, j, 0)),
            pl.BlockSpec((bb, 1, 3 * d), lambda i, j: (i, 0, 0)),
            _const_spec((1, 1, d)),
            pl.BlockSpec((d, IN_PACKED), lambda i, j: (0, 0), pipeline_mode=pl.Buffered(1)),
            pl.BlockSpec((3, rows, LANES), lambda i, j: (0, j, 0)),
            pl.BlockSpec((3, rows, LANES), lambda i, j: (0, j, 0)),
        ],
        out_specs=[o[1] for o in outs],
        out_shape=[o[0] for o in outs],
        compiler_params=_cparams(("parallel", "parallel")),
        name="inproj",
    )(x, mod.reshape(b, 1, 3 * d), g_norm.reshape(1, 1, d), w_packed, rope2, rope1)


KEY_BLOCK = 256
ATTEND_ROWS = 512


def _float_of_key(key):
    bits = jnp.where(key >= 0, key, key ^ jnp.int32(0x7FFFFFFF))
    return lax.bitcast_convert_type(bits, F32)


def _float_of_key16(key16):
    bits16 = jnp.where(key16 >= 0, key16, key16 ^ jnp.int32(0x7FFF))
    return lax.bitcast_convert_type(jnp.left_shift(bits16, 16), F32)


def _key_of_float(x):
    bits = lax.bitcast_convert_type(x, jnp.int32)
    return jnp.where(bits >= 0, bits, bits ^ jnp.int32(0x7FFFFFFF))


def _count(ref, bi, nkb, pred):
    one, zero = jnp.ones((), ref.dtype), jnp.zeros((), ref.dtype)
    acc = None
    for blk in range(nkb):
        hit = jnp.where(pred(ref[bi, blk]), one, zero)
        part = hit[:, :LANES] + hit[:, LANES:]
        acc = part if acc is None else acc + part
    return jnp.sum(acc.astype(F32), axis=1, keepdims=True)


def _search_thresholds(sc_ref, sc16_ref, thr_ref, need_ref, kf, nkb):
    nb, _, rows, _ = sc_ref.shape
    assert 2 * nkb <= 256

    def coarse(i, keys):
        step = jnp.left_shift(jnp.int32(1), 15 - i)
        out = []
        for bi, key in enumerate(keys):
            cand = key + step
            cf = _float_of_key16(cand).astype(BF16)
            cnt = _count(sc16_ref, bi, nkb, lambda x, cf=cf: x >= cf)
            out.append(jnp.where(cnt >= kf, cand, key))
        return tuple(out)

    keys16 = lax.fori_loop(0, 16, coarse,
                           tuple(jnp.full((rows, 1), -2 ** 15, jnp.int32) for _ in range(nb)))
    lows = [_key_of_float(_float_of_key16(k - 1)) for k in keys16]

    def fine(i, offs):
        step = jnp.left_shift(jnp.int32(1), 16 - i)
        out = []
        for bi, off in enumerate(offs):
            cand = off + step
            cf = _float_of_key(lows[bi] + cand)
            cnt = _count(sc_ref, bi, nkb, lambda x, cf=cf: x >= cf)
            out.append(jnp.where(cnt >= kf, cand, off))
        return tuple(out)

    offs = lax.fori_loop(0, 17, fine, tuple(jnp.zeros((rows, 1), jnp.int32) for _ in range(nb)))
    for bi in range(nb):
        thr = _float_of_key(lows[bi] + offs[bi])
        need = kf - _count(sc_ref, bi, nkb, lambda x, thr=thr: x > thr)
        thr_ref[bi] = jnp.broadcast_to(thr, thr_ref.shape[1:])
        need_ref[bi] = jnp.broadcast_to(need, need_ref.shape[1:])


def _attend_kernel(*refs, has_past, variants, search_from, q_pos0, past_len, present_len, n_sel):
    q_ref, qi_ref, kiwi_ref, ga_ref = refs[:4]
    n_in = 10 if has_past else 7
    past = refs[4:7] if has_past else None
    present = refs[n_in - 3:n_in]
    o_ref = refs[n_in]
    sc_ref, sc16_ref, thr_ref, need_ref, macc_ref, oacc_ref = refs[n_in + 1:]
    nb, tq = q_ref.shape[0], q_ref.shape[2]
    kb = KEY_BLOCK
    n_past = past_len // kb
    nt = (((1,), (1,)), ((), ()))
    rep = N_HEADS // N_KV_HEADS
    j = pl.program_id(1)

    qpos = q_pos0 + j * tq + lax.broadcasted_iota(jnp.int32, (tq, 1), 0)
    qchunk = qpos // CHUNK
    n_present = jnp.minimum((qchunk + 1) * CHUNK - past_len, present_len)
    kf = jnp.minimum(past_len + n_present, n_sel).astype(F32)
    last_chunk = (q_pos0 + (j + 1) * tq - 1) // CHUNK
    nkb_present = (jnp.minimum((last_chunk + 1) * CHUNK - past_len, present_len) + kb - 1) // kb

    def admissible(i):
        local = i * kb + lax.broadcasted_iota(jnp.int32, (1, kb), 1)
        return (((past_len + local) // CHUNK) <= qchunk) & (local < present_len)

    def for_blocks(fn, carry=0):
        if n_past:
            carry = lax.fori_loop(0, n_past, lambda i, c: fn(past, i, i, c), carry)
        return lax.fori_loop(0, nkb_present, lambda i, c: fn(present, i, n_past + i, c), carry)

    def rows_of(i):
        return pl.ds(pl.multiple_of(i * kb, kb), kb)

    low = lax.broadcasted_iota(jnp.int32, (kb, LANES), 1) < HEAD_DIM

    def keys_values(src, bi, i):
        kblk, vblk = src[1][bi, rows_of(i), :], src[2][bi, rows_of(i), :]
        if src is present:
            return [(kblk[:, g * PAD_HEAD:g * PAD_HEAD + HEAD_DIM],
                     vblk[:, g * PAD_HEAD:(g + 1) * PAD_HEAD]) for g in range(N_KV_HEADS)]
        out = []
        for g in range(N_KV_HEADS):
            kg = kblk if g == 0 else pltpu.roll(kblk, LANES - g * HEAD_DIM, axis=1)
            vg = vblk if g == 0 else pltpu.roll(vblk, LANES - g * HEAD_DIM, axis=1)
            out.append((kg[:, :HEAD_DIM].astype(BF16), jnp.where(low, vg, 1.0).astype(BF16)))
        return out

    def search_path():
        qi_rows = [qi_ref[bi].reshape(N_IDX_HEADS * tq, IDX_DIM) for bi in range(nb)]
        wts = [[jnp.broadcast_to(kiwi_ref[bi, :, _WI_COL + h:_WI_COL + h + 1]
                                 * (N_IDX_HEADS ** -0.5), (tq, kb))
                for h in range(N_IDX_HEADS)] for bi in range(nb)]

        def score_block(src, i, blk, c):
            for bi in range(nb):
                s = lax.dot_general(qi_rows[bi], src[0][bi, rows_of(i), :].astype(BF16), nt,
                                    preferred_element_type=F32)
                acc = None
                for h in range(N_IDX_HEADS):
                    term = wts[bi][h] * jnp.maximum(s[h * tq:(h + 1) * tq], 0.0)
                    acc = term if acc is None else acc + term
                if src is present:
                    acc = jnp.where(admissible(i), acc, -jnp.inf)
                sc_ref[bi, blk] = acc
                sc16_ref[bi, blk] = acc.astype(BF16)
            return c

        for_blocks(score_block)

        for lo, hi, nkb in variants:
            if len(variants) == 1:
                _search_thresholds(sc_ref, sc16_ref, thr_ref, need_ref, kf, nkb)
            else:
                pl.when((j >= lo) & (j < hi))(
                    functools.partial(_search_thresholds, sc_ref, sc16_ref, thr_ref, need_ref, kf,
                                      nkb))

        tri = (lax.broadcasted_iota(jnp.int32, (kb, kb), 0)
               <= lax.broadcasted_iota(jnp.int32, (kb, kb), 1)).astype(BF16)
        ones = jnp.ones((kb, LANES), BF16)

        def twice(a):
            return jnp.concatenate([a, a], axis=1)

        def bias_block(src, i, blk, seen):
            out = []
            for bi in range(nb):
                x = sc_ref[bi, blk]
                thr = twice(thr_ref[bi])
                eq = x == thr
                e = jnp.where(eq, 1.0, 0.0).astype(BF16)
                rank = jnp.dot(e, tri, preferred_element_type=F32) + twice(seen[bi])
                tie = jnp.where(rank <= twice(need_ref[bi]), 0.0, -jnp.inf)
                sc_ref[bi, blk] = jnp.where(eq, tie, jnp.where(x > thr, 0.0, -jnp.inf))
                out.append(seen[bi] + jnp.dot(e, ones, preferred_element_type=F32))
            return tuple(out)

        for_blocks(bias_block, tuple(jnp.zeros((tq, LANES), F32) for _ in range(nb)))

    def all_admissible_path():
        def bias_block(src, i, blk, c):
            for bi in range(nb):
                if src is present:
                    sc_ref[bi, blk] = jnp.where(admissible(i), 0.0, -jnp.inf)
                else:
                    sc_ref[bi, blk] = jnp.zeros((tq, kb), F32)
            return c

        for_blocks(bias_block)

    if search_from == 0:
        search_path()
    else:
        pl.when(j >= search_from)(search_path)
        pl.when(j < search_from)(all_admissible_path)

    q_rows = [[q_ref[bi, g * rep:(g + 1) * rep].reshape(rep * tq, HEAD_DIM)
               for g in range(N_KV_HEADS)] for bi in range(nb)]

    macc_ref[...] = jnp.full(macc_ref.shape, jnp.finfo(F32).min, F32)
    oacc_ref[...] = jnp.zeros(oacc_ref.shape, F32)

    def pv_block(src, i, blk, c):
        for bi in range(nb):
            bias = sc_ref[bi, blk]
            for g, (kg, vxg) in enumerate(keys_values(src, bi, i)):
                lg = lax.dot_general(q_rows[bi][g], kg, nt, preferred_element_type=F32)
                lg = (lg.reshape(rep, tq, kb) + bias[None]).reshape(rep * tq, kb)
                m_old = macc_ref[bi, g]
                row_max = jnp.max(jnp.maximum(lg[:, :LANES], lg[:, LANES:]), axis=1, keepdims=True)
                m_new = jnp.maximum(m_old, row_max)
                p = jnp.exp2(lg - jnp.concatenate([m_new, m_new], axis=1))
                oacc_ref[bi, g] = (oacc_ref[bi, g] * jnp.exp2(m_old - m_new)
                                   + jnp.dot(p.astype(BF16), vxg, preferred_element_type=F32))
                macc_ref[bi, g] = m_new
        return c

    for_blocks(pv_block)

    low = lax.broadcasted_iota(jnp.int32, (tq, LANES), 1) < HEAD_DIM
    for bi in range(nb):
        outs = []
        for g in range(N_KV_HEADS):
            acc = oacc_ref[bi, g]
            o = acc / pltpu.roll(acc, HEAD_DIM, axis=1)
            for h in range(0, rep, 2):
                outs.append(jnp.where(low, o[h * tq:(h + 1) * tq],
                                      pltpu.roll(o[(h + 1) * tq:(h + 2) * tq], HEAD_DIM, axis=1)))
        o_all = jnp.concatenate(outs, axis=1)
        o_ref[bi] = (o_all * ga_ref[bi].astype(F32)).astype(o_ref.dtype)


def _attend_variants(nq, tq, q_pos0, past_len, present_len, n_sel):
    kb = KEY_BLOCK
    variants, search_from = [], nq
    for j in range(nq):
        n_present = min(((q_pos0 + (j + 1) * tq - 1) // CHUNK + 1) * CHUNK - past_len, present_len)
        if past_len + n_present <= n_sel:
            assert not variants
            continue
        search_from = min(search_from, j)
        nkb = past_len // kb + -(-n_present // kb)
        if variants and variants[-1][2] == nkb:
            variants[-1] = (variants[-1][0], j + 1, nkb)
        else:
            variants.append((j, j + 1, nkb))
    return search_from, tuple(variants)


def _attend(q, qi, kiwi, ga, present, past, q_pos0, past_len, present_len):
    b, _, t, _ = q.shape
    kb = KEY_BLOCK
    tq = QBLK if t % QBLK == 0 else t
    nb = max(1, ATTEND_ROWS // tq)
    l_present = present[0].shape[1]
    assert b % nb == 0 and past_len % kb == 0 and l_present % kb == 0 and q_pos0 >= past_len
    n_sel = min(TOPK_MAX, (past_len + present_len) // 4)
    search_from, variants = _attend_variants(t // tq, tq, q_pos0, past_len, present_len, n_sel)
    nkb_max = past_len // kb + l_present // kb
    kern = functools.partial(_attend_kernel, has_past=past is not None, variants=variants,
                             search_from=search_from, q_pos0=q_pos0, past_len=past_len,
                             present_len=present_len, n_sel=n_sel)

    def qspec(width):
        return pl.BlockSpec((nb, tq, width), lambda i, j: (i, j, 0))

    def hspec(a):
        return pl.BlockSpec((nb, a.shape[1], tq, HEAD_DIM), lambda i, j: (i, 0, j, 0))

    def kspec(a):
        return pl.BlockSpec((nb,) + a.shape[1:], lambda i, j: (i, 0, 0))

    keys = (tuple(past) if past is not None else ()) + tuple(present)
    return pl.pallas_call(
        kern,
        grid=(b // nb, t // tq),
        in_specs=[hspec(q), hspec(qi), qspec(LANES), qspec(ATTN_WIDTH)]
        + [kspec(a) for a in keys],
        out_specs=qspec(ATTN_WIDTH),
        out_shape=jax.ShapeDtypeStruct((b, t, ATTN_WIDTH), BF16),
        scratch_shapes=[pltpu.VMEM((nb, nkb_max, tq, kb), F32),
                        pltpu.VMEM((nb, nkb_max, tq, kb), BF16),
                        pltpu.VMEM((nb, tq, LANES), F32), pltpu.VMEM((nb, tq, LANES), F32),
                        pltpu.VMEM((nb, N_KV_HEADS, N_HEADS // N_KV_HEADS * tq, LANES), F32),
                        pltpu.VMEM((nb, N_KV_HEADS, N_HEADS // N_KV_HEADS * tq, LANES), F32)],
        compiler_params=_cparams(("parallel", "parallel")),
        name="attend",
    )(q, qi, kiwi, ga, *keys)


SSM_SLAB = LANES


def _ssm_kernel(u_ref, h0r_ref, h0i_ref, a_ref, bbd_ref, cbd_ref, d_ref,
                y_ref, hr_ref, hi_ref, s_ref, st_ref):
    tt, nb, w = u_ref.shape
    rows = tt * nb
    n = SSM_LANES

    @pl.when(pl.program_id(0) == 0)
    def _():
        st_ref[0] = h0r_ref[...]
        st_ref[1] = h0i_ref[...]

    u2 = u_ref[...].reshape(rows, w)
    lanes_per_slab = SSM_SLAB // SSM_GROUP * SSM_STATE
    ys = []
    for q in range(w // SSM_SLAB):
        ch = slice(q * SSM_SLAB, (q + 1) * SSM_SLAB)
        re = slice(q * lanes_per_slab, (q + 1) * lanes_per_slab)
        im = slice(n + q * lanes_per_slab, n + (q + 1) * lanes_per_slab)
        s_ref[:, re] = jnp.dot(u2[:, ch], bbd_ref[ch, re], preferred_element_type=F32)
        s_ref[:, im] = jnp.dot(u2[:, ch], bbd_ref[ch, im], preferred_element_type=F32)
        ar = jnp.broadcast_to(a_ref[0:1, re], (nb, lanes_per_slab))
        ai = jnp.broadcast_to(a_ref[1:2, re], (nb, lanes_per_slab))
        hr, hi = st_ref[0, :, re], st_ref[1, :, re]
        for t in range(tt):
            rws = slice(t * nb, (t + 1) * nb)
            hr, hi = (ar * hr - ai * hi + s_ref[rws, re], ar * hi + ai * hr + s_ref[rws, im])
            s_ref[rws, re] = hr
            s_ref[rws, im] = hi
        st_ref[0, :, re] = hr
        st_ref[1, :, re] = hi
        ys.append(jnp.dot(s_ref[:, re].astype(BF16), cbd_ref[re, ch], preferred_element_type=F32)
                  + jnp.dot(s_ref[:, im].astype(BF16), cbd_ref[im, ch], preferred_element_type=F32))
    y = jnp.concatenate(ys, axis=1)
    y = y + d_ref[...] * u2.astype(F32)
    y_ref[...] = jax.nn.gelu(y).astype(y_ref.dtype).reshape(tt, nb, w)
    hr_ref[...] = st_ref[0]
    hi_ref[...] = st_ref[1]


def _ssm(u_tb, h0r, h0i, a2, bbd, cbd, d_skip):
    t, nb, w = u_tb.shape
    tt = max(1, 512 // nb)
    assert t % tt == 0
    n = SSM_LANES
    return pl.pallas_call(
        _ssm_kernel,
        grid=(t // tt,),
        in_specs=[pl.BlockSpec((tt, nb, w), lambda i: (i, 0, 0)),
                  _const_spec((nb, n)), _const_spec((nb, n)), _const_spec((2, n)),
                  _const_spec((w, 2 * n)), _const_spec((2 * n, w)), _const_spec((1, w))],
        out_specs=[pl.BlockSpec((tt, nb, w), lambda i: (i, 0, 0)),
                   _const_spec((nb, n)), _const_spec((nb, n))],
        out_shape=[jax.ShapeDtypeStruct((t, nb, w), BF16),
                   jax.ShapeDtypeStruct((nb, n), F32), jax.ShapeDtypeStruct((nb, n), F32)],
        scratch_shapes=[pltpu.VMEM((tt * nb, 2 * n), F32), pltpu.VMEM((2, nb, n), F32)],
        compiler_params=_cparams(("arbitrary",)),
        name="ssm",
    )(u_tb, h0r, h0i, a2, bbd, cbd, d_skip.reshape(1, w))


def _outproj_kernel(x_ref, mod_ref, a_ref, yg_ref, gs_ref, ma_ref, mb_ref,
                    wa_ref, wg_ref, ws_ref, wo_ref, gf_ref, y_ref):
    bb, tt, d = x_ref.shape
    rows = bb * tt

    def flat(ref):
        return ref[...].reshape(rows, ref.shape[-1])

    branch_a = jnp.dot(flat(a_ref), wa_ref[...], preferred_element_type=F32)
    g_lin = jnp.dot(flat(yg_ref), wg_ref[...], preferred_element_type=F32)
    y_glu = g_lin[:, :SSM_WIDTH] * jax.nn.sigmoid(g_lin[:, SSM_WIDTH:])
    gated = y_glu * flat(gs_ref).astype(F32)
    branch_b = jnp.dot(gated.astype(BF16), ws_ref[...], preferred_element_type=F32)
    merged = flat(ma_ref).astype(F32) * branch_a + flat(mb_ref).astype(F32) * branch_b
    proj = jnp.dot(merged.astype(BF16), wo_ref[...], preferred_element_type=F32)
    gate = mod_ref[:, :, 2 * d:3 * d]
    xo = x_ref[...] + gate * proj.reshape(bb, tt, d)
    y = xo * lax.rsqrt(jnp.mean(xo * xo, axis=-1, keepdims=True) + NORM_EPS) * gf_ref[...]
    y_ref[...] = y


def _outproj(x, mod, a, yg, gs, ma, mb, wa, wg, ws, wo, g_final):
    b, t, d = x.shape
    bb, tt = _row_blocking(b, t)

    def tok(width):
        return pl.BlockSpec((bb, tt, width), lambda i, j: (i, j, 0))

    return pl.pallas_call(
        _outproj_kernel,
        grid=(b // bb, t // tt),
        in_specs=[tok(d), pl.BlockSpec((bb, 1, 3 * d), lambda i, j: (i, 0, 0)),
                  tok(ATTN_WIDTH), tok(SSM_WIDTH), tok(SSM_WIDTH), tok(d), tok(d),
                  _const_spec(wa.shape), _const_spec(wg.shape), _const_spec(ws.shape),
                  _const_spec(wo.shape), _const_spec((1, 1, d))],
        out_specs=tok(d),
        out_shape=jax.ShapeDtypeStruct((b, t, d), F32),
        compiler_params=_cparams(("parallel", "parallel")),
        name="outproj",
    )(x, mod.reshape(b, 1, 3 * d), a, yg, gs, ma, mb, wa, wg, ws, wo, g_final.reshape(1, 1, d))


def _pad_rows(a, rows):
    return jnp.pad(a, ((0, 0), (0, rows - a.shape[1]), (0, 0)))


def _layer(x, mod, pos0, past, prm):
    (g_norm, w_packed, a2, bbd, cbd, d_skip, wg, wa, ws, wo, g_final) = prm
    b, t, _ = x.shape
    pos = pos0 + jnp.arange(t, dtype=jnp.int32)
    q, k, v, qi, ki, kiwi, kp, vx, kip, ga, u, gs, ma, mb = _inproj(x, mod, g_norm, w_packed, pos)

    t_pad = -(-t // KEY_BLOCK) * KEY_BLOCK
    present = tuple(_pad_rows(z, t_pad) for z in (kip, kp, vx)) if t_pad != t else (kip, kp, vx)
    if past is None:
        past_keys, past_len = None, 0
        h0r = jnp.zeros((b, SSM_LANES), F32)
        h0i = jnp.zeros((b, SSM_LANES), F32)
    else:
        ck, cv, cki, h0r, h0i = past
        past_len = ck.shape[1]
        past_keys = (cki, ck.reshape(b, past_len, KV_WIDTH), cv.reshape(b, past_len, KV_WIDTH))
        h0r = h0r.reshape(b, SSM_LANES)
        h0i = h0i.reshape(b, SSM_LANES)

    a = _attend(q, qi, kiwi, ga, present, past_keys, pos0, past_len, t)

    yg_tb, hr, hi = _ssm(jnp.swapaxes(u, 0, 1), h0r, h0i, a2, bbd, cbd, d_skip)
    yg = jnp.swapaxes(yg_tb, 0, 1)

    y = _outproj(x, mod, a, yg, gs, ma, mb, wa, wg, ws, wo, g_final)
    return (y, k.reshape(b, t, N_KV_HEADS, HEAD_DIM), v.reshape(b, t, N_KV_HEADS, HEAD_DIM), ki,
            hr.reshape(b, N_SSM_GROUPS, SSM_STATE), hi.reshape(b, N_SSM_GROUPS, SSM_STATE))


def kernel(x_prompt, x_sample, cache_k, cache_v, cache_idx_k, state_ssm_re, state_ssm_im,
           c_prompt, c_sample, w_mod, b_mod, g_norm, w_in, lambda_re, lambda_im, log_dt,
           ssm_b_re, ssm_b_im, ssm_c_re, ssm_c_im, d_skip, w_glu, w_attn_proj, w_ssm_proj,
           w_out, g_final):
    depth = w_in.shape[0]
    assert depth == 1, "the final norm is fused into the (single) layer's output kernel"
    nbp = x_prompt.shape[0]
    past_len = cache_k.shape[2]

    def layer0(a):
        return a.reshape(a.shape[1:])

    (w_mod, b_mod, g_norm, w_in, lambda_re, lambda_im, log_dt, ssm_b_re, ssm_b_im, ssm_c_re,
     ssm_c_im, d_skip, w_glu, w_attn_proj, w_ssm_proj, w_out, cache_k, cache_v, cache_idx_k,
     state_ssm_re, state_ssm_im) = map(layer0, (
         w_mod, b_mod, g_norm, w_in, lambda_re, lambda_im, log_dt, ssm_b_re, ssm_b_im, ssm_c_re,
         ssm_c_im, d_skip, w_glu, w_attn_proj, w_ssm_proj, w_out, cache_k, cache_v, cache_idx_k,
         state_ssm_re, state_ssm_im))

    mod = _modulation(jnp.concatenate([c_prompt, c_sample], axis=0), w_mod, b_mod)
    mod_p, mod_s = mod[:nbp], mod[nbp:]

    cut = _SEG_OFF["kiwi"][0] + IDX_DIM + N_IDX_HEADS
    w = w_in
    w_packed = jnp.concatenate([w[:, :cut].astype(BF16), jnp.zeros((D_MODEL, _KIWI_PAD), BF16),
                                w[:, cut:].astype(BF16)], axis=1)

    ar, ai, bbr, bbi = _discretize(lambda_re, lambda_im, log_dt, ssm_b_re, ssm_b_im)
    eye = jnp.eye(N_SSM_GROUPS, dtype=F32)
    a2 = jnp.stack([ar.reshape(SSM_LANES), ai.reshape(SSM_LANES)])

    def b_diag(bb_cgp):
        return jnp.einsum("cgp,gh->gchp", bb_cgp, eye).reshape(SSM_WIDTH, SSM_LANES)

    def c_diag(c_gcp):
        return jnp.einsum("gcp,gh->gphc", c_gcp, eye).reshape(SSM_LANES, SSM_WIDTH)

    bbd = jnp.concatenate([b_diag(bbr), b_diag(bbi)], axis=1).astype(BF16)
    cbd = jnp.concatenate([c_diag(ssm_c_re), -c_diag(ssm_c_im)], axis=0).astype(BF16)

    prm = (g_norm, w_packed, a2, bbd, cbd, d_skip, w_glu.astype(BF16),
           w_attn_proj.astype(BF16), w_ssm_proj.astype(BF16), w_out.astype(BF16), g_final)

    yp, kp, vp, kip, hrp, hip = _layer(x_prompt, mod_p, 0, None, prm)
    past = (cache_k, cache_v, cache_idx_k, state_ssm_re, state_ssm_im)
    ys, ks, vs, kis, hrs, his = _layer(x_sample, mod_s, past_len, past, prm)

    def st(z):
        return z[None]

    return (yp, ys, st(kp), st(vp), st(kip), st(hrp), st(hip),
            st(ks), st(vs), st(kis), st(hrs), st(his))
```

```python
import functools
import math

import jax
import jax.numpy as jnp
from jax import lax
from jax.experimental import pallas as pl
from jax.experimental.pallas import tpu as pltpu

F32 = jnp.float32
BF16 = jnp.bfloat16

LANES = 128
VMEM_LIMIT = 56 * 1024 * 1024

D_MODEL = 1024
CHUNK = 64
QBLK = 128
N_HEADS = 8
N_KV_HEADS = 2
HEAD_DIM = 64
ATTN_WIDTH = N_HEADS * HEAD_DIM
KV_WIDTH = N_KV_HEADS * HEAD_DIM
ROT_DIM = HEAD_DIM // 4
ROPE_THETA = 500000.0
N_IDX_HEADS = 8
IDX_DIM = 64
IDX_WIDTH = N_IDX_HEADS * IDX_DIM
TOPK_MAX = 256
SSM_WIDTH = D_MODEL // 2
SSM_GROUP = 16
N_SSM_GROUPS = SSM_WIDTH // SSM_GROUP
SSM_STATE = 64
SSM_LANES = N_SSM_GROUPS * SSM_STATE
NORM_EPS = 1e-6

_KIWI_PAD = LANES - IDX_DIM - N_IDX_HEADS
_SEG_WIDTHS = (ATTN_WIDTH, 2 * KV_WIDTH, IDX_WIDTH, LANES, ATTN_WIDTH, SSM_WIDTH, SSM_WIDTH,
               D_MODEL, D_MODEL)
_SEG_NAMES = ("q", "kv", "qi", "kiwi", "ga", "u", "gs", "ma", "mb")
_SEG_OFF = {}
_off = 0
for _n, _w in zip(_SEG_NAMES, _SEG_WIDTHS):
    _SEG_OFF[_n] = (_off, _off + _w)
    _off += _w
IN_PACKED = _off
_WI_COL = IDX_DIM

ROW_TILE = 512
PAD_HEAD = 2 * HEAD_DIM
INT_MIN = -2 ** 31


def _cparams(sem):
    return pltpu.CompilerParams(dimension_semantics=sem, vmem_limit_bytes=VMEM_LIMIT)


def _const_spec(shape):
    nd = len(shape)
    return pl.BlockSpec(shape, lambda *_: (0,) * nd)


def _mod_kernel(c_ref, w_ref, b_ref, o_ref):
    s = jax.nn.silu(c_ref[...])
    o_ref[...] = (jnp.dot(s.astype(BF16), w_ref[...].astype(BF16), preferred_element_type=F32)
                  + b_ref[...])


def _modulation(c, w_mod, b_mod):
    n = c.shape[0]
    return pl.pallas_call(
        _mod_kernel,
        out_shape=jax.ShapeDtypeStruct((n, 3 * D_MODEL), F32),
        compiler_params=pltpu.CompilerParams(vmem_limit_bytes=VMEM_LIMIT),
        name="mod",
    )(c, w_mod, b_mod.reshape(1, 3 * D_MODEL))


def _disc_kernel(lre_ref, lim_ref, ldt_ref, bre_ref, bim_ref, cre_ref, cim_ref,
                 a_ref, bbd_ref, cbd_ref):
    n = lre_ref.shape[1]
    dt = jnp.exp(ldt_ref[...])
    lr, li = lre_ref[...], lim_ref[...]
    mag = jnp.exp(lr * dt)
    ar, ai = mag * jnp.cos(li * dt), mag * jnp.sin(li * dt)
    den = lr * lr + li * li
    zr = ((ar - 1.0) * lr + ai * li) / den
    zi = (ai * lr - (ar - 1.0) * li) / den
    a_ref[0:1, :] = ar
    a_ref[1:2, :] = ai

    eye = (lax.broadcasted_iota(jnp.int32, (SSM_GROUP, SSM_GROUP), 0)
           == lax.broadcasted_iota(jnp.int32, (SSM_GROUP, SSM_GROUP), 1)).astype(BF16)

    def transposed(x):
        out, rest = None, x
        for _ in range(3):
            limb = rest.astype(BF16)
            rest = rest - limb.astype(F32)
            t = lax.dot_general(eye, limb, (((1,), (1,)), ((), ())), preferred_element_type=F32)
            out = t if out is None else out + t
        return out

    br, bi = transposed(bre_ref[...]), transposed(bim_ref[...])
    bbar_re = zr * br - zi * bi
    bbar_im = zr * bi + zi * br
    group_of_lane = lax.broadcasted_iota(jnp.int32, (SSM_GROUP, n), 1) // SSM_STATE
    for g in range(n // SSM_STATE):
        rows = slice(g * SSM_GROUP, (g + 1) * SSM_GROUP)
        mine = group_of_lane == g
        bbd_ref[rows, 0:n] = jnp.where(mine, bbar_re, 0.0).astype(bbd_ref.dtype)
        bbd_ref[rows, n:2 * n] = jnp.where(mine, bbar_im, 0.0).astype(bbd_ref.dtype)
    rows_n = cre_ref.shape[0]
    diag = (lax.broadcasted_iota(jnp.int32, (rows_n, n), 0) // SSM_GROUP
            == lax.broadcasted_iota(jnp.int32, (rows_n, n), 1) // SSM_STATE)
    cbd_ref[:, 0:n] = jnp.where(diag, cre_ref[...], 0.0).astype(cbd_ref.dtype)
    cbd_ref[:, n:2 * n] = jnp.where(diag, -cim_ref[...], 0.0).astype(cbd_ref.dtype)


def _discretize(lambda_re, lambda_im, log_dt, b_re, b_im, c_re, c_im):
    g, p, c = b_re.shape
    n = g * p

    def tiled(z):
        return jnp.tile(z.reshape(g * c, p), (1, g))

    return pl.pallas_call(
        _disc_kernel,
        out_shape=(jax.ShapeDtypeStruct((2, n), F32), jax.ShapeDtypeStruct((g * c, 2 * n), BF16),
                   jax.ShapeDtypeStruct((g * c, 2 * n), BF16)),
        compiler_params=pltpu.CompilerParams(vmem_limit_bytes=VMEM_LIMIT),
        name="disc",
    )(lambda_re.reshape(1, n), lambda_im.reshape(1, n), jnp.repeat(log_dt, p).reshape(1, n),
      b_re.reshape(n, c), b_im.reshape(n, c), tiled(c_re), tiled(c_im))


def _rope_block(z, cos, sa, sb):
    up = pltpu.roll(z, LANES - ROT_DIM // 2, axis=1)
    dn = pltpu.roll(z, ROT_DIM // 2, axis=1)
    return z * cos + up * sa + dn * sb


def _split_heads(z, fill):
    low = lax.broadcasted_iota(jnp.int32, z.shape, 1) < HEAD_DIM
    return jnp.concatenate([jnp.where(low, z, fill),
                            jnp.where(low, pltpu.roll(z, HEAD_DIM, axis=1), fill)], axis=1)


def _inproj_kernel(x_ref, mod_ref, g_ref, w_ref, rope2_ref, rope1_ref,
                   q_ref, k_ref, v_ref, qi_ref, ki_ref, kiwi_ref, kp_ref, vx_ref, kip_ref,
                   ga_ref, u_ref, gs_ref, ma_ref, mb_ref):
    bb, tt, d = x_ref.shape
    rows = bb * tt
    x = x_ref[...]
    y = x * lax.rsqrt(jnp.mean(x * x, axis=-1, keepdims=True) + NORM_EPS) * g_ref[...]
    shift = mod_ref[:, :, 0:d]
    scale = mod_ref[:, :, d:2 * d]
    h = (y * (1.0 + scale) + shift).reshape(rows, d).astype(BF16)

    def seg(name):
        a, b = _SEG_OFF[name]
        return jnp.dot(h, w_ref[:, a:b], preferred_element_type=F32)

    def roped(z, tab_ref):
        cos, sa, sb = tab_ref[0], tab_ref[1], tab_ref[2]
        blocks = [_rope_block(z[:, i:i + LANES], cos, sa, sb) for i in range(0, z.shape[1], LANES)]
        return blocks[0] if len(blocks) == 1 else jnp.concatenate(blocks, axis=1)

    def put(ref, val):
        ref[...] = val.reshape(ref.shape).astype(ref.dtype)

    def split_all(z, fill=0.0):
        return jnp.concatenate([_split_heads(z[:, i:i + LANES], fill)
                                for i in range(0, z.shape[1], LANES)], axis=1)

    def put_heads(ref, z):
        for hd in range(ref.shape[1]):
            pair = z[:, hd // 2 * LANES:(hd // 2 + 1) * LANES]
            if hd % 2:
                pair = pltpu.roll(pair, HEAD_DIM, axis=1)
            ref[:, hd] = pair[:, :HEAD_DIM].reshape(bb, tt, HEAD_DIM).astype(ref.dtype)

    put_heads(q_ref, roped(seg("q"), rope2_ref) * (HEAD_DIM ** -0.5 * math.log2(math.e)))
    kv = seg("kv")
    k = roped(kv[:, :KV_WIDTH], rope2_ref)
    put(k_ref, k)
    put(kp_ref, split_all(k))
    v = kv[:, KV_WIDTH:]
    put(v_ref, v)
    put(vx_ref, split_all(v, 1.0))
    put_heads(qi_ref, roped(seg("qi"), rope2_ref) * (IDX_DIM ** -0.5))
    kiwi = roped(seg("kiwi"), rope1_ref)
    put(kiwi_ref, kiwi)
    put(ki_ref, kiwi[:, 0:IDX_DIM])
    put(kip_ref, kiwi[:, 0:IDX_DIM])
    put(ga_ref, jax.nn.silu(seg("ga")))
    put(u_ref, seg("u"))
    put(gs_ref, jax.nn.silu(seg("gs")))
    put(ma_ref, jax.nn.sigmoid(seg("ma")))
    put(mb_ref, jax.nn.sigmoid(seg("mb")))


def _rope_tables(pos, heads_in_block):
    half = ROT_DIM // 2
    inv = jnp.power(ROPE_THETA, -jnp.arange(half, dtype=F32) * (2.0 / ROT_DIM))
    ang = pos.astype(F32)[:, None] * inv[None, :]
    cos, sin = jnp.cos(ang), jnp.sin(ang)
    t = pos.shape[0]
    ones = jnp.ones((t, HEAD_DIM - ROT_DIM), F32)
    zeros = jnp.zeros((t, HEAD_DIM - ROT_DIM), F32)
    zh = jnp.zeros((t, half), F32)
    c_head = jnp.concatenate([cos, cos, ones], axis=1)
    sa_head = jnp.concatenate([-sin, zh, zeros], axis=1)
    sb_head = jnp.concatenate([zh, sin, zeros], axis=1)
    n_id = LANES // HEAD_DIM - heads_in_block
    ident = [jnp.ones((t, HEAD_DIM), F32)] * n_id
    zero = [jnp.zeros((t, HEAD_DIM), F32)] * n_id
    c = jnp.concatenate([c_head] * heads_in_block + ident, axis=1)
    sa = jnp.concatenate([sa_head] * heads_in_block + zero, axis=1)
    sb = jnp.concatenate([sb_head] * heads_in_block + zero, axis=1)
    return jnp.stack([c, sa, sb])


def _row_blocking(b, t):
    tt = min(t, ROW_TILE)
    bb = ROW_TILE // tt
    assert t % tt == 0 and b % bb == 0 and tt % 16 == 0
    return bb, tt


def _inproj(x, mod, g_norm, w_packed, pos):
    b, t, d = x.shape
    bb, tt = _row_blocking(b, t)
    rows = bb * tt
    rope2 = jnp.tile(_rope_tables(pos, 2), (1, bb, 1)) if bb > 1 else _rope_tables(pos, 2)
    rope1 = jnp.tile(_rope_tables(pos, 1), (1, bb, 1)) if bb > 1 else _rope_tables(pos, 1)

    def tok(width, dtype):
        return (jax.ShapeDtypeStruct((b, t, width), dtype),
                pl.BlockSpec((bb, tt, width), lambda i, j: (i, j, 0)))

    def heads(n):
        return (jax.ShapeDtypeStruct((b, n, t, HEAD_DIM), BF16),
                pl.BlockSpec((bb, n, tt, HEAD_DIM), lambda i, j: (i, 0, j, 0)))

    outs = [heads(N_HEADS), tok(KV_WIDTH, F32), tok(KV_WIDTH, F32),
            heads(N_IDX_HEADS), tok(IDX_DIM, F32), tok(LANES, F32),
            tok(2 * KV_WIDTH, BF16), tok(2 * KV_WIDTH, BF16), tok(IDX_DIM, BF16),
            tok(ATTN_WIDTH, BF16), tok(SSM_WIDTH, BF16),
            tok(SSM_WIDTH, BF16), tok(D_MODEL, BF16), tok(D_MODEL, BF16)]
    return pl.pallas_call(
        _inproj_kernel,
        grid=(b // bb, t // tt),
        in_specs=[
            pl.BlockSpec((bb, tt, d), lambda i, j: (i, j, 0)),
            pl.BlockSpec((bb, 1, 3 * d), lambda i, j: (i, 0, 0)),
            _const_spec((1, 1, d)),
            pl.BlockSpec((d, IN_PACKED), lambda i, j: (0, 0), pipeline_mode=pl.Buffered(1)),
            pl.BlockSpec((3, rows, LANES), lambda i, j: (0, j, 0)),
            pl.BlockSpec((3, rows, LANES), lambda i, j: (0, j, 0)),
        ],
        out_specs=[o[1] for o in outs],
        out_shape=[o[0] for o in outs],
        compiler_params=_cparams(("parallel", "parallel")),
        name="inproj",
    )(x, mod.reshape(b, 1, 3 * d), g_norm.reshape(1, 1, d), w_packed, rope2, rope1)


KEY_BLOCK = 256
ATTEND_ROWS = 512


def _float_of_key(key):
    bits = jnp.where(key >= 0, key, key ^ jnp.int32(0x7FFFFFFF))
    return lax.bitcast_convert_type(bits, F32)


def _count(sc_ref, bi, nkb, pred):
    acc = None
    for blk in range(nkb):
        hit = jnp.where(pred(sc_ref[bi, blk]), 1.0, 0.0)
        part = hit[:, :LANES] + hit[:, LANES:]
        acc = part if acc is None else acc + part
    return jnp.sum(acc, axis=1, keepdims=True)


def _search_thresholds(sc_ref, thr_ref, need_ref, kf, nkb):
    nb, _, rows, _ = sc_ref.shape

    def body(i, keys):
        bit = jnp.left_shift(jnp.int32(1), 31 - i)
        out = []
        for bi, key in enumerate(keys):
            cand = key ^ bit
            cf = _float_of_key(cand)
            cnt = _count(sc_ref, bi, nkb, lambda x, cf=cf: x >= cf)
            out.append(jnp.where(cnt >= kf, cand, key))
        return tuple(out)

    init = tuple(jnp.full((rows, 1), INT_MIN, jnp.int32) for _ in range(nb))
    keys = lax.fori_loop(0, 32, body, init)
    for bi, key in enumerate(keys):
        thr = _float_of_key(key)
        need = kf - _count(sc_ref, bi, nkb, lambda x, thr=thr: x > thr)
        thr_ref[bi] = jnp.broadcast_to(thr, thr_ref.shape[1:])
        need_ref[bi] = jnp.broadcast_to(need, need_ref.shape[1:])


def _attend_kernel(*refs, has_past, variants, search_from, q_pos0, past_len, present_len, n_sel):
    q_ref, qi_ref, kiwi_ref, ga_ref = refs[:4]
    n_in = 10 if has_past else 7
    past = refs[4:7] if has_past else None
    present = refs[n_in - 3:n_in]
    o_ref = refs[n_in]
    sc_ref, thr_ref, need_ref, macc_ref, oacc_ref = refs[n_in + 1:]
    nb, tq = q_ref.shape[0], q_ref.shape[2]
    kb = KEY_BLOCK
    n_past = past_len // kb
    nt = (((1,), (1,)), ((), ()))
    rep = N_HEADS // N_KV_HEADS
    j = pl.program_id(1)

    qpos = q_pos0 + j * tq + lax.broadcasted_iota(jnp.int32, (tq, 1), 0)
    qchunk = qpos // CHUNK
    n_present = jnp.minimum((qchunk + 1) * CHUNK - past_len, present_len)
    kf = jnp.minimum(past_len + n_present, n_sel).astype(F32)
    last_chunk = (q_pos0 + (j + 1) * tq - 1) // CHUNK
    nkb_present = (jnp.minimum((last_chunk + 1) * CHUNK - past_len, present_len) + kb - 1) // kb

    def admissible(i):
        local = i * kb + lax.broadcasted_iota(jnp.int32, (1, kb), 1)
        return (((past_len + local) // CHUNK) <= qchunk) & (local < present_len)

    def for_blocks(fn, carry=0):
        if n_past:
            carry = lax.fori_loop(0, n_past, lambda i, c: fn(past, i, i, c), carry)
        return lax.fori_loop(0, nkb_present, lambda i, c: fn(present, i, n_past + i, c), carry)

    def rows_of(i):
        return pl.ds(pl.multiple_of(i * kb, kb), kb)

    low = lax.broadcasted_iota(jnp.int32, (kb, LANES), 1) < HEAD_DIM

    def keys_values(src, bi, i):
        kblk, vblk = src[1][bi, rows_of(i), :], src[2][bi, rows_of(i), :]
        if src is present:
            return [(kblk[:, g * PAD_HEAD:g * PAD_HEAD + HEAD_DIM],
                     vblk[:, g * PAD_HEAD:(g + 1) * PAD_HEAD]) for g in range(N_KV_HEADS)]
        out = []
        for g in range(N_KV_HEADS):
            kg = kblk if g == 0 else pltpu.roll(kblk, LANES - g * HEAD_DIM, axis=1)
            vg = vblk if g == 0 else pltpu.roll(vblk, LANES - g * HEAD_DIM, axis=1)
            out.append((kg[:, :HEAD_DIM].astype(BF16), jnp.where(low, vg, 1.0).astype(BF16)))
        return out

    def search_path():
        qi_rows = [qi_ref[bi].reshape(N_IDX_HEADS * tq, IDX_DIM) for bi in range(nb)]
        wts = [[jnp.broadcast_to(kiwi_ref[bi, :, _WI_COL + h:_WI_COL + h + 1]
                                 * (N_IDX_HEADS ** -0.5), (tq, kb))
                for h in range(N_IDX_HEADS)] for bi in range(nb)]

        def score_block(src, i, blk, c):
            for bi in range(nb):
                s = lax.dot_general(qi_rows[bi], src[0][bi, rows_of(i), :].astype(BF16), nt,
                                    preferred_element_type=F32)
                acc = None
                for h in range(N_IDX_HEADS):
                    term = wts[bi][h] * jnp.maximum(s[h * tq:(h + 1) * tq], 0.0)
                    acc = term if acc is None else acc + term
                if src is present:
                    acc = jnp.where(admissible(i), acc, -jnp.inf)
                sc_ref[bi, blk] = acc
            return c

        for_blocks(score_block)

        for lo, hi, nkb in variants:
            if len(variants) == 1:
                _search_thresholds(sc_ref, thr_ref, need_ref, kf, nkb)
            else:
                pl.when((j >= lo) & (j < hi))(
                    functools.partial(_search_thresholds, sc_ref, thr_ref, need_ref, kf, nkb))

        tri = (lax.broadcasted_iota(jnp.int32, (kb, kb), 0)
               <= lax.broadcasted_iota(jnp.int32, (kb, kb), 1)).astype(BF16)
        ones = jnp.ones((kb, LANES), BF16)

        def twice(a):
            return jnp.concatenate([a, a], axis=1)

        def bias_block(src, i, blk, seen):
            out = []
            for bi in range(nb):
                x = sc_ref[bi, blk]
                thr = twice(thr_ref[bi])
                eq = x == thr
                e = jnp.where(eq, 1.0, 0.0).astype(BF16)
                rank = jnp.dot(e, tri, preferred_element_type=F32) + twice(seen[bi])
                tie = jnp.where(rank <= twice(need_ref[bi]), 0.0, -jnp.inf)
                sc_ref[bi, blk] = jnp.where(eq, tie, jnp.where(x > thr, 0.0, -jnp.inf))
                out.append(seen[bi] + jnp.dot(e, ones, preferred_element_type=F32))
            return tuple(out)

        for_blocks(bias_block, tuple(jnp.zeros((tq, LANES), F32) for _ in range(nb)))

    def all_admissible_path():
        def bias_block(src, i, blk, c):
            for bi in range(nb):
                if src is present:
                    sc_ref[bi, blk] = jnp.where(admissible(i), 0.0, -jnp.inf)
                else:
                    sc_ref[bi, blk] = jnp.zeros((tq, kb), F32)
            return c

        for_blocks(bias_block)

    if search_from == 0:
        search_path()
    else:
        pl.when(j >= search_from)(search_path)
        pl.when(j < search_from)(all_admissible_path)

    q_rows = [[q_ref[bi, g * rep:(g + 1) * rep].reshape(rep * tq, HEAD_DIM)
               for g in range(N_KV_HEADS)] for bi in range(nb)]

    macc_ref[...] = jnp.full(macc_ref.shape, jnp.finfo(F32).min, F32)
    oacc_ref[...] = jnp.zeros(oacc_ref.shape, F32)

    def pv_block(src, i, blk, c):
        for bi in range(nb):
            bias = sc_ref[bi, blk]
            for g, (kg, vxg) in enumerate(keys_values(src, bi, i)):
                lg = lax.dot_general(q_rows[bi][g], kg, nt, preferred_element_type=F32)
                lg = (lg.reshape(rep, tq, kb) + bias[None]).reshape(rep * tq, kb)
                m_old = macc_ref[bi, g]
                row_max = jnp.max(jnp.maximum(lg[:, :LANES], lg[:, LANES:]), axis=1, keepdims=True)
                m_new = jnp.maximum(m_old, row_max)
                p = jnp.exp2(lg - jnp.concatenate([m_new, m_new], axis=1))
                oacc_ref[bi, g] = (oacc_ref[bi, g] * jnp.exp2(m_old - m_new)
                                   + jnp.dot(p.astype(BF16), vxg, preferred_element_type=F32))
                macc_ref[bi, g] = m_new
        return c

    for_blocks(pv_block)

    low = lax.broadcasted_iota(jnp.int32, (tq, LANES), 1) < HEAD_DIM
    for bi in range(nb):
        outs = []
        for g in range(N_KV_HEADS):
            acc = oacc_ref[bi, g]
            o = acc / pltpu.roll(acc, HEAD_DIM, axis=1)
            for h in range(0, rep, 2):
                outs.append(jnp.where(low, o[h * tq:(h + 1) * tq],
                                      pltpu.roll(o[(h + 1) * tq:(h + 2) * tq], HEAD_DIM, axis=1)))
        o_all = jnp.concatenate(outs, axis=1)
        o_ref[bi] = (o_all * ga_ref[bi].astype(F32)).astype(o_ref.dtype)


def _attend_variants(nq, tq, q_pos0, past_len, present_len, n_sel):
    kb = KEY_BLOCK
    variants, search_from = [], nq
    for j in range(nq):
        n_present = min(((q_pos0 + (j + 1) * tq - 1) // CHUNK + 1) * CHUNK - past_len, present_len)
        if past_len + n_present <= n_sel:
            assert not variants
            continue
        search_from = min(search_from, j)
        nkb = past_len // kb + -(-n_present // kb)
        if variants and variants[-1][2] == nkb:
            variants[-1] = (variants[-1][0], j + 1, nkb)
        else:
            variants.append((j, j + 1, nkb))
    return search_from, tuple(variants)


def _attend(q, qi, kiwi, ga, present, past, q_pos0, past_len, present_len):
    b, _, t, _ = q.shape
    kb = KEY_BLOCK
    tq = QBLK if t % QBLK == 0 else t
    nb = max(1, ATTEND_ROWS // tq)
    l_present = present[0].shape[1]
    assert b % nb == 0 and past_len % kb == 0 and l_present % kb == 0 and q_pos0 >= past_len
    n_sel = min(TOPK_MAX, (past_len + present_len) // 4)
    search_from, variants = _attend_variants(t // tq, tq, q_pos0, past_len, present_len, n_sel)
    nkb_max = past_len // kb + l_present // kb
    kern = functools.partial(_attend_kernel, has_past=past is not None, variants=variants,
                             search_from=search_from, q_pos0=q_pos0, past_len=past_len,
                             present_len=present_len, n_sel=n_sel)

    def qspec(width):
        return pl.BlockSpec((nb, tq, width), lambda i, j: (i, j, 0))

    def hspec(a):
        return pl.BlockSpec((nb, a.shape[1], tq, HEAD_DIM), lambda i, j: (i, 0, j, 0))

    def kspec(a):
        return pl.BlockSpec((nb,) + a.shape[1:], lambda i, j: (i, 0, 0))

    keys = (tuple(past) if past is not None else ()) + tuple(present)
    return pl.pallas_call(
        kern,
        grid=(b // nb, t // tq),
        in_specs=[hspec(q), hspec(qi), qspec(LANES), qspec(ATTN_WIDTH)]
        + [kspec(a) for a in keys],
        out_specs=qspec(ATTN_WIDTH),
        out_shape=jax.ShapeDtypeStruct((b, t, ATTN_WIDTH), BF16),
        scratch_shapes=[pltpu.VMEM((nb, nkb_max, tq, kb), F32),
                        pltpu.VMEM((nb, tq, LANES), F32), pltpu.VMEM((nb, tq, LANES), F32),
                        pltpu.VMEM((nb, N_KV_HEADS, N_HEADS // N_KV_HEADS * tq, LANES), F32),
                        pltpu.VMEM((nb, N_KV_HEADS, N_HEADS // N_KV_HEADS * tq, LANES), F32)],
        compiler_params=_cparams(("parallel", "parallel")),
        name="attend",
    )(q, qi, kiwi, ga, *keys)


SSM_SLAB = LANES


def _ssm_kernel(u_ref, h0r_ref, h0i_ref, a_ref, bbd_ref, cbd_ref, d_ref,
                y_ref, hr_ref, hi_ref, s_ref, st_ref):
    tt, nb, w = u_ref.shape
    rows = tt * nb
    n = SSM_LANES

    @pl.when(pl.program_id(0) == 0)
    def _():
        st_ref[0] = h0r_ref[...]
        st_ref[1] = h0i_ref[...]

    u2 = u_ref[...].reshape(rows, w)
    lanes_per_slab = SSM_SLAB // SSM_GROUP * SSM_STATE
    ys = []
    for q in range(w // SSM_SLAB):
        ch = slice(q * SSM_SLAB, (q + 1) * SSM_SLAB)
        re = slice(q * lanes_per_slab, (q + 1) * lanes_per_slab)
        im = slice(n + q * lanes_per_slab, n + (q + 1) * lanes_per_slab)
        s_ref[:, re] = jnp.dot(u2[:, ch], bbd_ref[ch, re], preferred_element_type=F32)
        s_ref[:, im] = jnp.dot(u2[:, ch], bbd_ref[ch, im], preferred_element_type=F32)
        ar = jnp.broadcast_to(a_ref[0:1, re], (nb, lanes_per_slab))
        ai = jnp.broadcast_to(a_ref[1:2, re], (nb, lanes_per_slab))
        hr, hi = st_ref[0, :, re], st_ref[1, :, re]
        for t in range(tt):
            rws = slice(t * nb, (t + 1) * nb)
            hr, hi = (ar * hr - ai * hi + s_ref[rws, re], ar * hi + ai * hr + s_ref[rws, im])
            s_ref[rws, re] = hr
            s_ref[rws, im] = hi
        st_ref[0, :, re] = hr
        st_ref[1, :, re] = hi
        nt = (((1,), (1,)), ((), ()))
        ys.append(lax.dot_general(s_ref[:, re].astype(BF16), cbd_ref[ch, re], nt,
                                  preferred_element_type=F32)
                  + lax.dot_general(s_ref[:, im].astype(BF16), cbd_ref[ch, im], nt,
                                    preferred_element_type=F32))
    y = jnp.concatenate(ys, axis=1)
    y = y + d_ref[...] * u2.astype(F32)
    y_ref[...] = jax.nn.gelu(y).astype(y_ref.dtype).reshape(tt, nb, w)
    hr_ref[...] = st_ref[0]
    hi_ref[...] = st_ref[1]


def _ssm(u_tb, h0r, h0i, a2, bbd, cbd, d_skip):
    t, nb, w = u_tb.shape
    tt = max(1, 512 // nb)
    assert t % tt == 0
    n = SSM_LANES
    return pl.pallas_call(
        _ssm_kernel,
        grid=(t // tt,),
        in_specs=[pl.BlockSpec((tt, nb, w), lambda i: (i, 0, 0)),
                  _const_spec((nb, n)), _const_spec((nb, n)), _const_spec((2, n)),
                  _const_spec((w, 2 * n)), _const_spec((w, 2 * n)), _const_spec((1, w))],
        out_specs=[pl.BlockSpec((tt, nb, w), lambda i: (i, 0, 0)),
                   _const_spec((nb, n)), _const_spec((nb, n))],
        out_shape=[jax.ShapeDtypeStruct((t, nb, w), BF16),
                   jax.ShapeDtypeStruct((nb, n), F32), jax.ShapeDtypeStruct((nb, n), F32)],
        scratch_shapes=[pltpu.VMEM((tt * nb, 2 * n), F32), pltpu.VMEM((2, nb, n), F32)],
        compiler_params=_cparams(("arbitrary",)),
        name="ssm",
    )(u_tb, h0r, h0i, a2, bbd, cbd, d_skip.reshape(1, w))


def _outproj_kernel(x_ref, mod_ref, a_ref, yg_ref, gs_ref, ma_ref, mb_ref,
                    wa_ref, wg_ref, ws_ref, wo_ref, gf_ref, y_ref):
    bb, tt, d = x_ref.shape
    rows = bb * tt

    def flat(ref):
        return ref[...].reshape(rows, ref.shape[-1])

    branch_a = jnp.dot(flat(a_ref), wa_ref[...], preferred_element_type=F32)
    g_lin = jnp.dot(flat(yg_ref), wg_ref[...], preferred_element_type=F32)
    y_glu = g_lin[:, :SSM_WIDTH] * jax.nn.sigmoid(g_lin[:, SSM_WIDTH:])
    gated = y_glu * flat(gs_ref).astype(F32)
    branch_b = jnp.dot(gated.astype(BF16), ws_ref[...], preferred_element_type=F32)
    merged = flat(ma_ref).astype(F32) * branch_a + flat(mb_ref).astype(F32) * branch_b
    proj = jnp.dot(merged.astype(BF16), wo_ref[...], preferred_element_type=F32)
    gate = mod_ref[:, :, 2 * d:3 * d]
    xo = x_ref[...] + gate * proj.reshape(bb, tt, d)
    y = xo * lax.rsqrt(jnp.mean(xo * xo, axis=-1, keepdims=True) + NORM_EPS) * gf_ref[...]
    y_ref[...] = y


def _outproj(x, mod, a, yg, gs, ma, mb, wa, wg, ws, wo, g_final):
    b, t, d = x.shape
    bb, tt = _row_blocking(b, t)

    def tok(width):
        return pl.BlockSpec((bb, tt, width), lambda i, j: (i, j, 0))

    return pl.pallas_call(
        _outproj_kernel,
        grid=(b // bb, t // tt),
        in_specs=[tok(d), pl.BlockSpec((bb, 1, 3 * d), lambda i, j: (i, 0, 0)),
                  tok(ATTN_WIDTH), tok(SSM_WIDTH), tok(SSM_WIDTH), tok(d), tok(d),
                  _const_spec(wa.shape), _const_spec(wg.shape), _const_spec(ws.shape),
                  _const_spec(wo.shape), _const_spec((1, 1, d))],
        out_specs=tok(d),
        out_shape=jax.ShapeDtypeStruct((b, t, d), F32),
        compiler_params=_cparams(("parallel", "parallel")),
        name="outproj",
    )(x, mod.reshape(b, 1, 3 * d), a, yg, gs, ma, mb, wa, wg, ws, wo, g_final.reshape(1, 1, d))


def _pad_rows(a, rows):
    return jnp.pad(a, ((0, 0), (0, rows - a.shape[1]), (0, 0)))


def _layer(x, mod, pos0, past, prm):
    (g_norm, w_packed, a2, bbd, cbd, d_skip, wg, wa, ws, wo, g_final) = prm
    b, t, _ = x.shape
    pos = pos0 + jnp.arange(t, dtype=jnp.int32)
    q, k, v, qi, ki, kiwi, kp, vx, kip, ga, u, gs, ma, mb = _inproj(x, mod, g_norm, w_packed, pos)

    t_pad = -(-t // KEY_BLOCK) * KEY_BLOCK
    present = tuple(_pad_rows(z, t_pad) for z in (kip, kp, vx)) if t_pad != t else (kip, kp, vx)
    if past is None:
        past_keys, past_len = None, 0
        h0r = jnp.zeros((b, SSM_LANES), F32)
        h0i = jnp.zeros((b, SSM_LANES), F32)
    else:
        ck, cv, cki, h0r, h0i = past
        past_len = ck.shape[1]
        past_keys = (cki, ck.reshape(b, past_len, KV_WIDTH), cv.reshape(b, past_len, KV_WIDTH))
        h0r = h0r.reshape(b, SSM_LANES)
        h0i = h0i.reshape(b, SSM_LANES)

    a = _attend(q, qi, kiwi, ga, present, past_keys, pos0, past_len, t)

    yg_tb, hr, hi = _ssm(jnp.swapaxes(u, 0, 1), h0r, h0i, a2, bbd, cbd, d_skip)
    yg = jnp.swapaxes(yg_tb, 0, 1)

    y = _outproj(x, mod, a, yg, gs, ma, mb, wa, wg, ws, wo, g_final)
    return (y, k.reshape(b, t, N_KV_HEADS, HEAD_DIM), v.reshape(b, t, N_KV_HEADS, HEAD_DIM), ki,
            hr.reshape(b, N_SSM_GROUPS, SSM_STATE), hi.reshape(b, N_SSM_GROUPS, SSM_STATE))


def kernel(x_prompt, x_sample, cache_k, cache_v, cache_idx_k, state_ssm_re, state_ssm_im,
           c_prompt, c_sample, w_mod, b_mod, g_norm, w_in, lambda_re, lambda_im, log_dt,
           ssm_b_re, ssm_b_im, ssm_c_re, ssm_c_im, d_skip, w_glu, w_attn_proj, w_ssm_proj,
           w_out, g_final):
    depth = w_in.shape[0]
    assert depth == 1, "the final norm is fused into the (single) layer's output kernel"
    nbp = x_prompt.shape[0]
    past_len = cache_k.shape[2]

    def layer0(a):
        return a.reshape(a.shape[1:])

    mod = _modulation(jnp.concatenate([c_prompt, c_sample], axis=0), layer0(w_mod), layer0(b_mod))
    mod_p, mod_s = mod[:nbp], mod[nbp:]

    cut = _SEG_OFF["kiwi"][0] + IDX_DIM + N_IDX_HEADS
    w = layer0(w_in)
    col = lax.broadcasted_iota(jnp.int32, (1, IN_PACKED), 1)
    w_packed = jnp.where(col < cut, jnp.pad(w, ((0, 0), (0, _KIWI_PAD))),
                         jnp.where(col < cut + _KIWI_PAD, 0.0,
                                   jnp.pad(w, ((0, 0), (_KIWI_PAD, 0))))).astype(BF16)

    a2, bbd, cbd = _discretize(*map(layer0, (lambda_re, lambda_im, log_dt, ssm_b_re, ssm_b_im,
                                             ssm_c_re, ssm_c_im)))

    prm = (layer0(g_norm), w_packed, a2, bbd, cbd, layer0(d_skip), layer0(w_glu).astype(BF16),
           layer0(w_attn_proj).astype(BF16), layer0(w_ssm_proj).astype(BF16),
           layer0(w_out).astype(BF16), g_final)

    yp, kp, vp, kip, hrp, hip = _layer(x_prompt, mod_p, 0, None, prm)
    past = tuple(map(layer0, (cache_k, cache_v, cache_idx_k, state_ssm_re, state_ssm_im)))
    ys, ks, vs, kis, hrs, his = _layer(x_sample, mod_s, past_len, past, prm)

    def st(z):
        return z[None]

    return (yp, ys, st(kp), st(vp), st(kip), st(hrp), st(hip),
            st(ks), st(vs), st(kis), st(hrs), st(his))
```

```python
import functools
import math

import jax
import jax.numpy as jnp
from jax import lax
from jax.experimental import pallas as pl
from jax.experimental.pallas import tpu as pltpu

F32 = jnp.float32
BF16 = jnp.bfloat16

LANES = 128
VMEM_LIMIT = 56 * 1024 * 1024

D_MODEL = 1024
CHUNK = 64
QBLK = 128
N_HEADS = 8
N_KV_HEADS = 2
HEAD_DIM = 64
ATTN_WIDTH = N_HEADS * HEAD_DIM
KV_WIDTH = N_KV_HEADS * HEAD_DIM
ROT_DIM = HEAD_DIM // 4
ROPE_THETA = 500000.0
N_IDX_HEADS = 8
IDX_DIM = 64
IDX_WIDTH = N_IDX_HEADS * IDX_DIM
TOPK_MAX = 256
SSM_WIDTH = D_MODEL // 2
SSM_GROUP = 16
N_SSM_GROUPS = SSM_WIDTH // SSM_GROUP
SSM_STATE = 64
SSM_LANES = N_SSM_GROUPS * SSM_STATE
NORM_EPS = 1e-6

_KIWI_PAD = LANES - IDX_DIM - N_IDX_HEADS
_SEG_WIDTHS = (ATTN_WIDTH, 2 * KV_WIDTH, IDX_WIDTH, LANES, ATTN_WIDTH, SSM_WIDTH, SSM_WIDTH,
               D_MODEL, D_MODEL)
_SEG_NAMES = ("q", "kv", "qi", "kiwi", "ga", "u", "gs", "ma", "mb")
_SEG_OFF = {}
_off = 0
for _n, _w in zip(_SEG_NAMES, _SEG_WIDTHS):
    _SEG_OFF[_n] = (_off, _off + _w)
    _off += _w
IN_PACKED = _off
_WI_COL = IDX_DIM

ROW_TILE = 512
PAD_HEAD = 2 * HEAD_DIM
KEY_BLOCK = 256
VX_ROWS = HEAD_DIM + 16
INT_MIN = -2 ** 31


def _cparams(sem):
    return pltpu.CompilerParams(dimension_semantics=sem, vmem_limit_bytes=VMEM_LIMIT)


def _const_spec(shape):
    nd = len(shape)
    return pl.BlockSpec(shape, lambda *_: (0,) * nd)


def _mod_kernel(c_ref, w_ref, b_ref, o_ref):
    s = jax.nn.silu(c_ref[...])
    o_ref[...] = (jnp.dot(s.astype(BF16), w_ref[...].astype(BF16), preferred_element_type=F32)
                  + b_ref[...])


def _modulation(c, w_mod, b_mod):
    n = c.shape[0]
    return pl.pallas_call(
        _mod_kernel,
        out_shape=jax.ShapeDtypeStruct((n, 3 * D_MODEL), F32),
        compiler_params=pltpu.CompilerParams(vmem_limit_bytes=VMEM_LIMIT),
        name="mod",
    )(c, w_mod, b_mod.reshape(1, 3 * D_MODEL))


def _disc_kernel(lre_ref, lim_ref, ldt_ref, bre_ref, bim_ref, cre_ref, cim_ref,
                 a_ref, bbd_ref, cbd_ref):
    n = lre_ref.shape[1]
    dt = jnp.exp(ldt_ref[...])
    lr, li = lre_ref[...], lim_ref[...]
    mag = jnp.exp(lr * dt)
    ar, ai = mag * jnp.cos(li * dt), mag * jnp.sin(li * dt)
    den = lr * lr + li * li
    zr = ((ar - 1.0) * lr + ai * li) / den
    zi = (ai * lr - (ar - 1.0) * li) / den
    a_ref[0:1, :] = ar
    a_ref[1:2, :] = ai

    eye = (lax.broadcasted_iota(jnp.int32, (SSM_GROUP, SSM_GROUP), 0)
           == lax.broadcasted_iota(jnp.int32, (SSM_GROUP, SSM_GROUP), 1)).astype(BF16)

    def transposed(x):
        out, rest = None, x
        for _ in range(3):
            limb = rest.astype(BF16)
            rest = rest - limb.astype(F32)
            t = lax.dot_general(eye, limb, (((1,), (1,)), ((), ())), preferred_element_type=F32)
            out = t if out is None else out + t
        return out

    br, bi = transposed(bre_ref[...]), transposed(bim_ref[...])
    bbar_re = zr * br - zi * bi
    bbar_im = zr * bi + zi * br
    group_of_lane = lax.broadcasted_iota(jnp.int32, (SSM_GROUP, n), 1) // SSM_STATE
    for g in range(n // SSM_STATE):
        rows = slice(g * SSM_GROUP, (g + 1) * SSM_GROUP)
        mine = group_of_lane == g
        bbd_ref[rows, 0:n] = jnp.where(mine, bbar_re, 0.0).astype(bbd_ref.dtype)
        bbd_ref[rows, n:2 * n] = jnp.where(mine, bbar_im, 0.0).astype(bbd_ref.dtype)
    rows_n = cre_ref.shape[0]
    diag = (lax.broadcasted_iota(jnp.int32, (rows_n, n), 0) // SSM_GROUP
            == lax.broadcasted_iota(jnp.int32, (rows_n, n), 1) // SSM_STATE)
    cbd_ref[:, 0:n] = jnp.where(diag, cre_ref[...], 0.0).astype(cbd_ref.dtype)
    cbd_ref[:, n:2 * n] = jnp.where(diag, -cim_ref[...], 0.0).astype(cbd_ref.dtype)


def _discretize(lambda_re, lambda_im, log_dt, b_re, b_im, c_re, c_im):
    g, p, c = b_re.shape
    n = g * p

    def tiled(z):
        return jnp.tile(z.reshape(g * c, p), (1, g))

    return pl.pallas_call(
        _disc_kernel,
        out_shape=(jax.ShapeDtypeStruct((2, n), F32), jax.ShapeDtypeStruct((g * c, 2 * n), BF16),
                   jax.ShapeDtypeStruct((g * c, 2 * n), BF16)),
        compiler_params=pltpu.CompilerParams(vmem_limit_bytes=VMEM_LIMIT),
        name="disc",
    )(lambda_re.reshape(1, n), lambda_im.reshape(1, n), jnp.repeat(log_dt, p).reshape(1, n),
      b_re.reshape(n, c), b_im.reshape(n, c), tiled(c_re), tiled(c_im))


def _rope_block(z, cos, sa, sb):
    up = pltpu.roll(z, LANES - ROT_DIM // 2, axis=1)
    dn = pltpu.roll(z, ROT_DIM // 2, axis=1)
    return z * cos + up * sa + dn * sb


def _split_heads(z, fill):
    low = lax.broadcasted_iota(jnp.int32, z.shape, 1) < HEAD_DIM
    return jnp.concatenate([jnp.where(low, z, fill),
                            jnp.where(low, pltpu.roll(z, HEAD_DIM, axis=1), fill)], axis=1)


def _inproj_kernel(x_ref, mod_ref, g_ref, w_ref, rope2_ref, rope1_ref,
                   q_ref, k_ref, v_ref, qi_ref, ki_ref, wt_ref, kp_ref, vx_ref, kip_ref,
                   ga_ref, u_ref, gs_ref, ma_ref, mb_ref):
    bb, tt, d = x_ref.shape
    rows = bb * tt
    x = x_ref[...]
    y = x * lax.rsqrt(jnp.mean(x * x, axis=-1, keepdims=True) + NORM_EPS) * g_ref[...]
    shift = mod_ref[:, :, 0:d]
    scale = mod_ref[:, :, d:2 * d]
    h = (y * (1.0 + scale) + shift).reshape(rows, d).astype(BF16)

    def seg(name):
        a, b = _SEG_OFF[name]
        return jnp.dot(h, w_ref[:, a:b], preferred_element_type=F32)

    def roped(z, tab_ref):
        cos, sa, sb = tab_ref[0], tab_ref[1], tab_ref[2]
        blocks = [_rope_block(z[:, i:i + LANES], cos, sa, sb) for i in range(0, z.shape[1], LANES)]
        return blocks[0] if len(blocks) == 1 else jnp.concatenate(blocks, axis=1)

    def put(ref, val):
        ref[...] = val.reshape(ref.shape).astype(ref.dtype)

    def split_all(z, fill=0.0):
        return jnp.concatenate([_split_heads(z[:, i:i + LANES], fill)
                                for i in range(0, z.shape[1], LANES)], axis=1)

    def token_lanes(z, b, lanes):
        blk = z[b * tt:(b + 1) * tt]
        if lanes > tt:
            blk = jnp.concatenate([blk, jnp.zeros((lanes - tt, LANES), F32)], axis=0)
        return blk.T

    def put_heads_t(ref, z):
        for b in range(bb):
            for pair in range(ref.shape[1] // 2):
                t = token_lanes(z[:, pair * LANES:(pair + 1) * LANES], b, ref.shape[3])
                ref[b, 2 * pair] = t[:HEAD_DIM].astype(ref.dtype)
                ref[b, 2 * pair + 1] = t[HEAD_DIM:].astype(ref.dtype)

    put_heads_t(q_ref, roped(seg("q"), rope2_ref) * (HEAD_DIM ** -0.5 * math.log2(math.e)))
    kv = seg("kv")
    k = roped(kv[:, :KV_WIDTH], rope2_ref)
    put(k_ref, k)
    put(kp_ref, split_all(k))
    v = kv[:, KV_WIDTH:]
    put(v_ref, v)
    n_kb = vx_ref.shape[2]
    ones = jnp.ones((VX_ROWS - HEAD_DIM, KEY_BLOCK), F32)
    for b in range(bb):
        vt = token_lanes(v, b, n_kb * KEY_BLOCK)
        for g in range(N_KV_HEADS):
            for kb_i in range(n_kb):
                blk = vt[g * HEAD_DIM:(g + 1) * HEAD_DIM, kb_i * KEY_BLOCK:(kb_i + 1) * KEY_BLOCK]
                vx_ref[b, g, kb_i] = jnp.concatenate([blk, ones], axis=0).astype(vx_ref.dtype)
    put_heads_t(qi_ref, roped(seg("qi"), rope2_ref) * (IDX_DIM ** -0.5))
    kiwi = roped(seg("kiwi"), rope1_ref)
    for b in range(bb):
        wt_ref[b] = token_lanes(kiwi, b, wt_ref.shape[2])[_WI_COL:_WI_COL + N_IDX_HEADS]
    put(ki_ref, kiwi[:, 0:IDX_DIM])
    put(kip_ref, kiwi[:, 0:IDX_DIM])
    put(ga_ref, jax.nn.silu(seg("ga")))
    put(u_ref, seg("u"))
    put(gs_ref, jax.nn.silu(seg("gs")))
    put(ma_ref, jax.nn.sigmoid(seg("ma")))
    put(mb_ref, jax.nn.sigmoid(seg("mb")))


def _rope_tables(pos, heads_in_block):
    half = ROT_DIM // 2
    inv = jnp.power(ROPE_THETA, -jnp.arange(half, dtype=F32) * (2.0 / ROT_DIM))
    ang = pos.astype(F32)[:, None] * inv[None, :]
    cos, sin = jnp.cos(ang), jnp.sin(ang)
    t = pos.shape[0]
    ones = jnp.ones((t, HEAD_DIM - ROT_DIM), F32)
    zeros = jnp.zeros((t, HEAD_DIM - ROT_DIM), F32)
    zh = jnp.zeros((t, half), F32)
    c_head = jnp.concatenate([cos, cos, ones], axis=1)
    sa_head = jnp.concatenate([-sin, zh, zeros], axis=1)
    sb_head = jnp.concatenate([zh, sin, zeros], axis=1)
    n_id = LANES // HEAD_DIM - heads_in_block
    ident = [jnp.ones((t, HEAD_DIM), F32)] * n_id
    zero = [jnp.zeros((t, HEAD_DIM), F32)] * n_id
    c = jnp.concatenate([c_head] * heads_in_block + ident, axis=1)
    sa = jnp.concatenate([sa_head] * heads_in_block + zero, axis=1)
    sb = jnp.concatenate([sb_head] * heads_in_block + zero, axis=1)
    return jnp.stack([c, sa, sb])


def _row_blocking(b, t):
    tt = min(t, ROW_TILE)
    bb = ROW_TILE // tt
    assert t % tt == 0 and b % bb == 0 and tt % 16 == 0
    return bb, tt


def _inproj(x, mod, g_norm, w_packed, pos):
    b, t, d = x.shape
    bb, tt = _row_blocking(b, t)
    rows = bb * tt
    rope2 = jnp.tile(_rope_tables(pos, 2), (1, bb, 1)) if bb > 1 else _rope_tables(pos, 2)
    rope1 = jnp.tile(_rope_tables(pos, 1), (1, bb, 1)) if bb > 1 else _rope_tables(pos, 1)

    def tok(width, dtype):
        return (jax.ShapeDtypeStruct((b, t, width), dtype),
                pl.BlockSpec((bb, tt, width), lambda i, j: (i, j, 0)))

    tq = max(tt, LANES)
    nkb = -(-tt // KEY_BLOCK)

    def heads_t(n):
        return (jax.ShapeDtypeStruct((b, n, HEAD_DIM, t // tt * tq), BF16),
                pl.BlockSpec((bb, n, HEAD_DIM, tq), lambda i, j: (i, 0, 0, j)))

    w_t = (jax.ShapeDtypeStruct((b, N_IDX_HEADS, t // tt * tq), F32),
           pl.BlockSpec((bb, N_IDX_HEADS, tq), lambda i, j: (i, 0, j)))
    vx_t = (jax.ShapeDtypeStruct((b, N_KV_HEADS, t // tt * nkb, VX_ROWS, KEY_BLOCK), BF16),
            pl.BlockSpec((bb, N_KV_HEADS, nkb, VX_ROWS, KEY_BLOCK), lambda i, j: (i, 0, j, 0, 0)))

    outs = [heads_t(N_HEADS), tok(KV_WIDTH, F32), tok(KV_WIDTH, F32),
            heads_t(N_IDX_HEADS), tok(IDX_DIM, F32), w_t,
            tok(2 * KV_WIDTH, BF16), vx_t, tok(IDX_DIM, BF16),
            tok(ATTN_WIDTH, BF16), tok(SSM_WIDTH, BF16),
            tok(SSM_WIDTH, BF16), tok(D_MODEL, BF16), tok(D_MODEL, BF16)]
    return pl.pallas_call(
        _inproj_kernel,
        grid=(b // bb, t // tt),
        in_specs=[
            pl.BlockSpec((bb, tt, d), lambda i, j: (i, j, 0)),
            pl.BlockSpec((bb, 1, 3 * d), lambda i, j: (i, 0, 0)),
            _const_spec((1, 1, d)),
            pl.BlockSpec((d, IN_PACKED), lambda i, j: (0, 0), pipeline_mode=pl.Buffered(1)),
            pl.BlockSpec((3, rows, LANES), lambda i, j: (0, j, 0)),
            pl.BlockSpec((3, rows, LANES), lambda i, j: (0, j, 0)),
        ],
        out_specs=[o[1] for o in outs],
        out_shape=[o[0] for o in outs],
        compiler_params=_cparams(("parallel", "parallel")),
        name="inproj",
    )(x, mod.reshape(b, 1, 3 * d), g_norm.reshape(1, 1, d), w_packed, rope2, rope1)


ATTEND_ROWS = 512


def _float_of_key(key):
    bits = jnp.where(key >= 0, key, key ^ jnp.int32(0x7FFFFFFF))
    return lax.bitcast_convert_type(bits, F32)


def _count(sc_ref, bi, nkb, pred):
    acc = None
    for blk in range(nkb):
        hit = jnp.where(pred(sc_ref[bi, blk]), 1.0, 0.0)
        part = jnp.sum(hit.reshape(-1, 8, hit.shape[1]), axis=0)
        acc = part if acc is None else acc + part
    return jnp.sum(acc, axis=0, keepdims=True)


def _search_thresholds(sc_ref, thr_ref, need_ref, kf, nkb):
    nb, _, _, tq = sc_ref.shape

    def body(i, keys):
        bit = jnp.left_shift(jnp.int32(1), 31 - i)
        out = []
        for bi, key in enumerate(keys):
            cand = key ^ bit
            cf = _float_of_key(cand)
            cnt = _count(sc_ref, bi, nkb, lambda x, cf=cf: x >= cf)
            out.append(jnp.where(cnt >= kf, cand, key))
        return tuple(out)

    init = tuple(jnp.full((1, tq), INT_MIN, jnp.int32) for _ in range(nb))
    keys = lax.fori_loop(0, 32, body, init)
    for bi, key in enumerate(keys):
        thr = _float_of_key(key)
        need = kf - _count(sc_ref, bi, nkb, lambda x, thr=thr: x > thr)
        thr_ref[bi] = jnp.broadcast_to(thr, thr_ref.shape[1:])
        need_ref[bi] = jnp.broadcast_to(need, need_ref.shape[1:])


def _attend_kernel(*refs, has_past, variants, search_from, q_pos0, past_len, present_len, n_sel):
    q_ref, qi_ref, wt_ref, ga_ref = refs[:4]
    n_in = 10 if has_past else 7
    past = refs[4:7] if has_past else None
    present = refs[n_in - 3:n_in]
    o_ref = refs[n_in]
    sc_ref, thr_ref, need_ref, macc_ref, oacc_ref = refs[n_in + 1:]
    nb, tq = q_ref.shape[0], q_ref.shape[3]
    kb = KEY_BLOCK
    n_past = past_len // kb
    rep = N_HEADS // N_KV_HEADS
    j = pl.program_id(1)

    qpos = q_pos0 + j * tq + lax.broadcasted_iota(jnp.int32, (1, tq), 1)
    qchunk = qpos // CHUNK
    n_present = jnp.minimum((qchunk + 1) * CHUNK - past_len, present_len)
    kf = jnp.minimum(past_len + n_present, n_sel).astype(F32)
    last_chunk = (q_pos0 + (j + 1) * tq - 1) // CHUNK
    nkb_present = (jnp.minimum((last_chunk + 1) * CHUNK - past_len, present_len) + kb - 1) // kb

    def admissible(i):
        local = i * kb + lax.broadcasted_iota(jnp.int32, (kb, 1), 0)
        return (((past_len + local) // CHUNK) <= qchunk) & (local < present_len)

    def for_blocks(fn, carry=0):
        if n_past:
            carry = lax.fori_loop(0, n_past, lambda i, c: fn(past, i, i, c), carry)
        return lax.fori_loop(0, nkb_present, lambda i, c: fn(present, i, n_past + i, c), carry)

    def rows_of(i):
        return pl.ds(pl.multiple_of(i * kb, kb), kb)

    def keys_values(src, bi, i):
        kblk = src[1][bi, rows_of(i), :]
        if src is present:
            return [(kblk[:, g * PAD_HEAD:g * PAD_HEAD + HEAD_DIM], src[2][bi, g, i])
                    for g in range(N_KV_HEADS)]
        vt = src[2][bi, rows_of(i), :].T
        ones = jnp.ones((VX_ROWS - HEAD_DIM, kb), F32)
        out = []
        for g in range(N_KV_HEADS):
            kg = kblk if g == 0 else pltpu.roll(kblk, LANES - g * HEAD_DIM, axis=1)
            vg = jnp.concatenate([vt[g * HEAD_DIM:(g + 1) * HEAD_DIM], ones], axis=0)
            out.append((kg[:, :HEAD_DIM].astype(BF16), vg.astype(BF16)))
        return out

    def search_path():
        qi_all = [jnp.concatenate([qi_ref[bi, h] for h in range(N_IDX_HEADS)], axis=1)
                  for bi in range(nb)]
        wts = [[wt_ref[bi, h:h + 1, :] * (N_IDX_HEADS ** -0.5) for h in range(N_IDX_HEADS)]
               for bi in range(nb)]

        def score_block(src, i, blk, c):
            for bi in range(nb):
                s = jnp.dot(src[0][bi, rows_of(i), :].astype(BF16), qi_all[bi],
                            preferred_element_type=F32)
                acc = None
                for h in range(N_IDX_HEADS):
                    term = wts[bi][h] * jnp.maximum(s[:, h * tq:(h + 1) * tq], 0.0)
                    acc = term if acc is None else acc + term
                if src is present:
                    acc = jnp.where(admissible(i), acc, -jnp.inf)
                sc_ref[bi, blk] = acc
            return c

        for_blocks(score_block)

        for lo, hi, nkb in variants:
            if len(variants) == 1:
                _search_thresholds(sc_ref, thr_ref, need_ref, kf, nkb)
            else:
                pl.when((j >= lo) & (j < hi))(
                    functools.partial(_search_thresholds, sc_ref, thr_ref, need_ref, kf, nkb))

        tri = (lax.broadcasted_iota(jnp.int32, (kb, kb), 1)
               <= lax.broadcasted_iota(jnp.int32, (kb, kb), 0)).astype(BF16)

        def bias_block(src, i, blk, seen):
            out = []
            for bi in range(nb):
                x = sc_ref[bi, blk]
                thr = thr_ref[bi, 0:1, :]
                eq = x == thr
                e = jnp.where(eq, 1.0, 0.0).astype(BF16)
                inblock = jnp.dot(tri, e, preferred_element_type=F32)
                tie = jnp.where(inblock + seen[bi] <= need_ref[bi, 0:1, :], 0.0, -jnp.inf)
                sc_ref[bi, blk] = jnp.where(eq, tie, jnp.where(x > thr, 0.0, -jnp.inf))
                out.append(seen[bi] + inblock[kb - 1:kb, :])
            return tuple(out)

        for_blocks(bias_block, tuple(jnp.zeros((1, tq), F32) for _ in range(nb)))

    def all_admissible_path():
        def bias_block(src, i, blk, c):
            for bi in range(nb):
                if src is present:
                    sc_ref[bi, blk] = jnp.where(admissible(i), 0.0, -jnp.inf)
                else:
                    sc_ref[bi, blk] = jnp.zeros((kb, tq), F32)
            return c

        for_blocks(bias_block)

    if search_from == 0:
        search_path()
    else:
        pl.when(j >= search_from)(search_path)
        pl.when(j < search_from)(all_admissible_path)

    q_all = [[jnp.concatenate([q_ref[bi, g * rep + h] for h in range(rep)], axis=1)
              for g in range(N_KV_HEADS)] for bi in range(nb)]

    macc_ref[...] = jnp.full(macc_ref.shape, jnp.finfo(F32).min, F32)
    oacc_ref[...] = jnp.zeros(oacc_ref.shape, F32)

    def pv_block(src, i, blk, c):
        units = [(bi, g, kg, vxg) for bi in range(nb)
                 for g, (kg, vxg) in enumerate(keys_values(src, bi, i))]
        biases = [jnp.concatenate([sc_ref[bi, blk]] * rep, axis=1) for bi in range(nb)]
        logits = [jnp.dot(kg, q_all[bi][g], preferred_element_type=F32) + biases[bi]
                  for bi, g, kg, _ in units]
        probs, alphas = [], []
        for (bi, g, _, _), lg in zip(units, logits):
            m_old = macc_ref[bi, g, 0:1, :]
            blk_max = jnp.max(jnp.max(lg.reshape(-1, 8, lg.shape[1]), axis=0), axis=0,
                              keepdims=True)
            m_new = jnp.maximum(m_old, blk_max)
            probs.append(jnp.exp2(lg - m_new).astype(BF16))
            alphas.append(jnp.exp2(m_old - m_new))
            macc_ref[bi, g] = jnp.broadcast_to(m_new, macc_ref.shape[2:])
        for (bi, g, _, vxg), p, alpha in zip(units, probs, alphas):
            oacc_ref[bi, g] = (oacc_ref[bi, g] * alpha
                               + jnp.dot(vxg, p, preferred_element_type=F32))
        return c

    for_blocks(pv_block)

    t_out = ga_ref.shape[1]
    for bi in range(nb):
        outs = []
        for g in range(N_KV_HEADS):
            acc = oacc_ref[bi, g]
            o = acc[:HEAD_DIM] / acc[HEAD_DIM:HEAD_DIM + 1]
            for h in range(0, rep, 2):
                pair = jnp.concatenate([o[:, h * tq:(h + 1) * tq], o[:, (h + 1) * tq:(h + 2) * tq]],
                                       axis=0)
                outs.append(pair.T)
        o_all = jnp.concatenate(outs, axis=1)[:t_out]
        o_ref[bi] = (o_all * ga_ref[bi].astype(F32)).astype(o_ref.dtype)


def _attend_variants(nq, tq, q_pos0, past_len, present_len, n_sel):
    kb = KEY_BLOCK
    variants, search_from = [], nq
    for j in range(nq):
        n_present = min(((q_pos0 + (j + 1) * tq - 1) // CHUNK + 1) * CHUNK - past_len, present_len)
        if past_len + n_present <= n_sel:
            assert not variants
            continue
        search_from = min(search_from, j)
        nkb = past_len // kb + -(-n_present // kb)
        if variants and variants[-1][2] == nkb:
            variants[-1] = (variants[-1][0], j + 1, nkb)
        else:
            variants.append((j, j + 1, nkb))
    return search_from, tuple(variants)


def _attend(q, qi, wt, ga, present, past, q_pos0, past_len, present_len):
    b, _, _, t_pad = q.shape
    t = ga.shape[1]
    kb = KEY_BLOCK
    tq = LANES
    nq = t_pad // tq
    tr = t // nq
    nb = max(1, ATTEND_ROWS // tq)
    l_present = present[0].shape[1]
    assert b % nb == 0 and past_len % kb == 0 and l_present % kb == 0 and q_pos0 >= past_len
    assert t % nq == 0 and (nq == 1 or tr == tq)
    n_sel = min(TOPK_MAX, (past_len + present_len) // 4)
    search_from, variants = _attend_variants(nq, tq, q_pos0, past_len, present_len, n_sel)
    nkb_max = past_len // kb + l_present // kb
    kern = functools.partial(_attend_kernel, has_past=past is not None, variants=variants,
                             search_from=search_from, q_pos0=q_pos0, past_len=past_len,
                             present_len=present_len, n_sel=n_sel)

    def kspec(a):
        nd = a.ndim
        return pl.BlockSpec((nb,) + a.shape[1:], lambda i, j: (i,) + (0,) * (nd - 1))

    keys = (tuple(past) if past is not None else ()) + tuple(present)
    return pl.pallas_call(
        kern,
        grid=(b // nb, nq),
        in_specs=[pl.BlockSpec((nb, N_HEADS, HEAD_DIM, tq), lambda i, j: (i, 0, 0, j)),
                  pl.BlockSpec((nb, N_IDX_HEADS, IDX_DIM, tq), lambda i, j: (i, 0, 0, j)),
                  pl.BlockSpec((nb, N_IDX_HEADS, tq), lambda i, j: (i, 0, j)),
                  pl.BlockSpec((nb, tr, ATTN_WIDTH), lambda i, j: (i, j, 0))]
        + [kspec(a) for a in keys],
        out_specs=pl.BlockSpec((nb, tr, ATTN_WIDTH), lambda i, j: (i, j, 0)),
        out_shape=jax.ShapeDtypeStruct((b, t, ATTN_WIDTH), BF16),
        scratch_shapes=[pltpu.VMEM((nb, nkb_max, kb, tq), F32),
                        pltpu.VMEM((nb, 8, tq), F32), pltpu.VMEM((nb, 8, tq), F32),
                        pltpu.VMEM((nb, N_KV_HEADS, 8, N_HEADS // N_KV_HEADS * tq), F32),
                        pltpu.VMEM((nb, N_KV_HEADS, VX_ROWS, N_HEADS // N_KV_HEADS * tq), F32)],
        compiler_params=_cparams(("parallel", "parallel")),
        name="attend",
    )(q, qi, wt, ga, *keys)


SSM_SLAB = LANES


def _ssm_kernel(u_ref, h0r_ref, h0i_ref, a_ref, bbd_ref, cbd_ref, d_ref,
                y_ref, hr_ref, hi_ref, s_ref, st_ref):
    tt, nb, w = u_ref.shape
    rows = tt * nb
    n = SSM_LANES

    @pl.when(pl.program_id(0) == 0)
    def _():
        st_ref[0] = h0r_ref[...]
        st_ref[1] = h0i_ref[...]

    u2 = u_ref[...].reshape(rows, w)
    lanes_per_slab = SSM_SLAB // SSM_GROUP * SSM_STATE
    ys = []
    for q in range(w // SSM_SLAB):
        ch = slice(q * SSM_SLAB, (q + 1) * SSM_SLAB)
        re = slice(q * lanes_per_slab, (q + 1) * lanes_per_slab)
        im = slice(n + q * lanes_per_slab, n + (q + 1) * lanes_per_slab)
        s_ref[:, re] = jnp.dot(u2[:, ch], bbd_ref[ch, re], preferred_element_type=F32)
        s_ref[:, im] = jnp.dot(u2[:, ch], bbd_ref[ch, im], preferred_element_type=F32)
        ar = jnp.broadcast_to(a_ref[0:1, re], (nb, lanes_per_slab))
        ai = jnp.broadcast_to(a_ref[1:2, re], (nb, lanes_per_slab))
        hr, hi = st_ref[0, :, re], st_ref[1, :, re]
        for t in range(tt):
            rws = slice(t * nb, (t + 1) * nb)
            hr, hi = (ar * hr - ai * hi + s_ref[rws, re], ar * hi + ai * hr + s_ref[rws, im])
            s_ref[rws, re] = hr
            s_ref[rws, im] = hi
        st_ref[0, :, re] = hr
        st_ref[1, :, re] = hi
        nt = (((1,), (1,)), ((), ()))
        ys.append(lax.dot_general(s_ref[:, re].astype(BF16), cbd_ref[ch, re], nt,
                                  preferred_element_type=F32)
                  + lax.dot_general(s_ref[:, im].astype(BF16), cbd_ref[ch, im], nt,
                                    preferred_element_type=F32))
    y = jnp.concatenate(ys, axis=1)
    y = y + d_ref[...] * u2.astype(F32)
    y_ref[...] = jax.nn.gelu(y).astype(y_ref.dtype).reshape(tt, nb, w)
    hr_ref[...] = st_ref[0]
    hi_ref[...] = st_ref[1]


def _ssm(u_tb, h0r, h0i, a2, bbd, cbd, d_skip):
    t, nb, w = u_tb.shape
    tt = max(1, 512 // nb)
    assert t % tt == 0
    n = SSM_LANES
    return pl.pallas_call(
        _ssm_kernel,
        grid=(t // tt,),
        in_specs=[pl.BlockSpec((tt, nb, w), lambda i: (i, 0, 0)),
                  _const_spec((nb, n)), _const_spec((nb, n)), _const_spec((2, n)),
                  _const_spec((w, 2 * n)), _const_spec((w, 2 * n)), _const_spec((1, w))],
        out_specs=[pl.BlockSpec((tt, nb, w), lambda i: (i, 0, 0)),
                   _const_spec((nb, n)), _const_spec((nb, n))],
        out_shape=[jax.ShapeDtypeStruct((t, nb, w), BF16),
                   jax.ShapeDtypeStruct((nb, n), F32), jax.ShapeDtypeStruct((nb, n), F32)],
        scratch_shapes=[pltpu.VMEM((tt * nb, 2 * n), F32), pltpu.VMEM((2, nb, n), F32)],
        compiler_params=_cparams(("arbitrary",)),
        name="ssm",
    )(u_tb, h0r, h0i, a2, bbd, cbd, d_skip.reshape(1, w))


def _outproj_kernel(x_ref, mod_ref, a_ref, yg_ref, gs_ref, ma_ref, mb_ref,
                    wa_ref, wg_ref, ws_ref, wo_ref, gf_ref, y_ref):
    bb, tt, d = x_ref.shape
    rows = bb * tt

    def flat(ref):
        return ref[...].reshape(rows, ref.shape[-1])

    branch_a = jnp.dot(flat(a_ref), wa_ref[...], preferred_element_type=F32)
    g_lin = jnp.dot(flat(yg_ref), wg_ref[...], preferred_element_type=F32)
    y_glu = g_lin[:, :SSM_WIDTH] * jax.nn.sigmoid(g_lin[:, SSM_WIDTH:])
    gated = y_glu * flat(gs_ref).astype(F32)
    branch_b = jnp.dot(gated.astype(BF16), ws_ref[...], preferred_element_type=F32)
    merged = flat(ma_ref).astype(F32) * branch_a + flat(mb_ref).astype(F32) * branch_b
    proj = jnp.dot(merged.astype(BF16), wo_ref[...], preferred_element_type=F32)
    gate = mod_ref[:, :, 2 * d:3 * d]
    xo = x_ref[...] + gate * proj.reshape(bb, tt, d)
    y = xo * lax.rsqrt(jnp.mean(xo * xo, axis=-1, keepdims=True) + NORM_EPS) * gf_ref[...]
    y_ref[...] = y


def _outproj(x, mod, a, yg, gs, ma, mb, wa, wg, ws, wo, g_final):
    b, t, d = x.shape
    bb, tt = _row_blocking(b, t)

    def tok(width):
        return pl.BlockSpec((bb, tt, width), lambda i, j: (i, j, 0))

    return pl.pallas_call(
        _outproj_kernel,
        grid=(b // bb, t // tt),
        in_specs=[tok(d), pl.BlockSpec((bb, 1, 3 * d), lambda i, j: (i, 0, 0)),
                  tok(ATTN_WIDTH), tok(SSM_WIDTH), tok(SSM_WIDTH), tok(d), tok(d),
                  _const_spec(wa.shape), _const_spec(wg.shape), _const_spec(ws.shape),
                  _const_spec(wo.shape), _const_spec((1, 1, d))],
        out_specs=tok(d),
        out_shape=jax.ShapeDtypeStruct((b, t, d), F32),
        compiler_params=_cparams(("parallel", "parallel")),
        name="outproj",
    )(x, mod.reshape(b, 1, 3 * d), a, yg, gs, ma, mb, wa, wg, ws, wo, g_final.reshape(1, 1, d))


def _pad_rows(a, rows):
    return jnp.pad(a, ((0, 0), (0, rows - a.shape[1]), (0, 0)))


def _layer(x, mod, pos0, past, prm):
    (g_norm, w_packed, a2, bbd, cbd, d_skip, wg, wa, ws, wo, g_final) = prm
    b, t, _ = x.shape
    pos = pos0 + jnp.arange(t, dtype=jnp.int32)
    q, k, v, qi, ki, wt, kp, vx, kip, ga, u, gs, ma, mb = _inproj(x, mod, g_norm, w_packed, pos)

    t_pad = -(-t // KEY_BLOCK) * KEY_BLOCK
    present = ((_pad_rows(kip, t_pad), _pad_rows(kp, t_pad)) if t_pad != t else (kip, kp)) + (vx,)
    if past is None:
        past_keys, past_len = None, 0
        h0r = jnp.zeros((b, SSM_LANES), F32)
        h0i = jnp.zeros((b, SSM_LANES), F32)
    else:
        ck, cv, cki, h0r, h0i = past
        past_len = ck.shape[1]
        past_keys = (cki, ck.reshape(b, past_len, KV_WIDTH), cv.reshape(b, past_len, KV_WIDTH))
        h0r = h0r.reshape(b, SSM_LANES)
        h0i = h0i.reshape(b, SSM_LANES)

    a = _attend(q, qi, wt, ga, present, past_keys, pos0, past_len, t)

    yg_tb, hr, hi = _ssm(jnp.swapaxes(u, 0, 1), h0r, h0i, a2, bbd, cbd, d_skip)
    yg = jnp.swapaxes(yg_tb, 0, 1)

    y = _outproj(x, mod, a, yg, gs, ma, mb, wa, wg, ws, wo, g_final)
    return (y, k.reshape(b, t, N_KV_HEADS, HEAD_DIM), v.reshape(b, t, N_KV_HEADS, HEAD_DIM), ki,
            hr.reshape(b, N_SSM_GROUPS, SSM_STATE), hi.reshape(b, N_SSM_GROUPS, SSM_STATE))


def kernel(x_prompt, x_sample, cache_k, cache_v, cache_idx_k, state_ssm_re, state_ssm_im,
           c_prompt, c_sample, w_mod, b_mod, g_norm, w_in, lambda_re, lambda_im, log_dt,
           ssm_b_re, ssm_b_im, ssm_c_re, ssm_c_im, d_skip, w_glu, w_attn_proj, w_ssm_proj,
           w_out, g_final):
    depth = w_in.shape[0]
    assert depth == 1, "the final norm is fused into the (single) layer's output kernel"
    nbp = x_prompt.shape[0]
    past_len = cache_k.shape[2]

    def layer0(a):
        return a.reshape(a.shape[1:])

    mod = _modulation(jnp.concatenate([c_prompt, c_sample], axis=0), layer0(w_mod), layer0(b_mod))
    mod_p, mod_s = mod[:nbp], mod[nbp:]

    cut = _SEG_OFF["kiwi"][0] + IDX_DIM + N_IDX_HEADS
    w = layer0(w_in)
    col = lax.broadcasted_iota(jnp.int32, (1, IN_PACKED), 1)
    w_packed = jnp.where(col < cut, jnp.pad(w, ((0, 0), (0, _KIWI_PAD))),
                         jnp.where(col < cut + _KIWI_PAD, 0.0,
                                   jnp.pad(w, ((0, 0), (_KIWI_PAD, 0))))).astype(BF16)

    a2, bbd, cbd = _discretize(*map(layer0, (lambda_re, lambda_im, log_dt, ssm_b_re, ssm_b_im,
                                             ssm_c_re, ssm_c_im)))

    prm = (layer0(g_norm), w_packed, a2, bbd, cbd, layer0(d_skip), layer0(w_glu).astype(BF16),
           layer0(w_attn_proj).astype(BF16), layer0(w_ssm_proj).astype(BF16),
           layer0(w_out).astype(BF16), g_final)

    yp, kp, vp, kip, hrp, hip = _layer(x_prompt, mod_p, 0, None, prm)
    past = tuple(map(layer0, (cache_k, cache_v, cache_idx_k, state_ssm_re, state_ssm_im)))
    ys, ks, vs, kis, hrs, his = _layer(x_sample, mod_s, past_len, past, prm)

    def st(z):
        return z[None]

    return (yp, ys, st(kp), st(vp), st(kip), st(hrp), st(hip),
            st(ks), st(vs), st(kis), st(hrs), st(his))
```

```python
import functools
import math

import jax
import jax.numpy as jnp
from jax import lax
from jax.experimental import pallas as pl
from jax.experimental.pallas import tpu as pltpu

F32 = jnp.float32
BF16 = jnp.bfloat16

LANES = 128
VMEM_LIMIT = 56 * 1024 * 1024

D_MODEL = 1024
CHUNK = 64
QBLK = 128
N_HEADS = 8
N_KV_HEADS = 2
HEAD_DIM = 64
ATTN_WIDTH = N_HEADS * HEAD_DIM
KV_WIDTH = N_KV_HEADS * HEAD_DIM
ROT_DIM = HEAD_DIM // 4
ROPE_THETA = 500000.0
N_IDX_HEADS = 8
IDX_DIM = 64
IDX_WIDTH = N_IDX_HEADS * IDX_DIM
TOPK_MAX = 256
SSM_WIDTH = D_MODEL // 2
SSM_GROUP = 16
N_SSM_GROUPS = SSM_WIDTH // SSM_GROUP
SSM_STATE = 64
SSM_LANES = N_SSM_GROUPS * SSM_STATE
NORM_EPS = 1e-6

_KIWI_PAD = LANES - IDX_DIM - N_IDX_HEADS
_SEG_WIDTHS = (ATTN_WIDTH, 2 * KV_WIDTH, IDX_WIDTH, LANES, ATTN_WIDTH, SSM_WIDTH, SSM_WIDTH,
               D_MODEL, D_MODEL)
_SEG_NAMES = ("q", "kv", "qi", "kiwi", "ga", "u", "gs", "ma", "mb")
_SEG_OFF = {}
_off = 0
for _n, _w in zip(_SEG_NAMES, _SEG_WIDTHS):
    _SEG_OFF[_n] = (_off, _off + _w)
    _off += _w
IN_PACKED = _off
_WI_COL = IDX_DIM

ROW_TILE = 512
PAD_HEAD = 2 * HEAD_DIM
KEY_BLOCK = 256
VX_ROWS = HEAD_DIM + 16
INT_MIN = -2 ** 31


def _cparams(sem):
    return pltpu.CompilerParams(dimension_semantics=sem, vmem_limit_bytes=VMEM_LIMIT)


def _const_spec(shape):
    nd = len(shape)
    return pl.BlockSpec(shape, lambda *_: (0,) * nd)


def _mod_kernel(c_ref, w_ref, b_ref, o_ref):
    s = jax.nn.silu(c_ref[...])
    o_ref[...] = (jnp.dot(s.astype(BF16), w_ref[...].astype(BF16), preferred_element_type=F32)
                  + b_ref[...])


def _modulation(c, w_mod, b_mod):
    n = c.shape[0]
    return pl.pallas_call(
        _mod_kernel,
        out_shape=jax.ShapeDtypeStruct((n, 3 * D_MODEL), F32),
        compiler_params=pltpu.CompilerParams(vmem_limit_bytes=VMEM_LIMIT),
        name="mod",
    )(c, w_mod, b_mod.reshape(1, 3 * D_MODEL))


def _disc_kernel(lre_ref, lim_ref, ldt_ref, bre_ref, bim_ref, cre_ref, cim_ref,
                 a_ref, bbd_ref, cbd_ref):
    n = lre_ref.shape[1]
    dt = jnp.exp(ldt_ref[...])
    lr, li = lre_ref[...], lim_ref[...]
    mag = jnp.exp(lr * dt)
    ar, ai = mag * jnp.cos(li * dt), mag * jnp.sin(li * dt)
    den = lr * lr + li * li
    zr = ((ar - 1.0) * lr + ai * li) / den
    zi = (ai * lr - (ar - 1.0) * li) / den
    a_ref[0:1, :] = ar
    a_ref[1:2, :] = ai

    eye = (lax.broadcasted_iota(jnp.int32, (SSM_GROUP, SSM_GROUP), 0)
           == lax.broadcasted_iota(jnp.int32, (SSM_GROUP, SSM_GROUP), 1)).astype(BF16)

    def transposed(x):
        out, rest = None, x
        for _ in range(3):
            limb = rest.astype(BF16)
            rest = rest - limb.astype(F32)
            t = lax.dot_general(eye, limb, (((1,), (1,)), ((), ())), preferred_element_type=F32)
            out = t if out is None else out + t
        return out

    br, bi = transposed(bre_ref[...]), transposed(bim_ref[...])
    bbar_re = zr * br - zi * bi
    bbar_im = zr * bi + zi * br
    group_of_lane = lax.broadcasted_iota(jnp.int32, (SSM_GROUP, n), 1) // SSM_STATE
    for g in range(n // SSM_STATE):
        rows = slice(g * SSM_GROUP, (g + 1) * SSM_GROUP)
        mine = group_of_lane == g
        bbd_ref[rows, 0:n] = jnp.where(mine, bbar_re, 0.0).astype(bbd_ref.dtype)
        bbd_ref[rows, n:2 * n] = jnp.where(mine, bbar_im, 0.0).astype(bbd_ref.dtype)
    rows_n = cre_ref.shape[0]
    diag = (lax.broadcasted_iota(jnp.int32, (rows_n, n), 0) // SSM_GROUP
            == lax.broadcasted_iota(jnp.int32, (rows_n, n), 1) // SSM_STATE)
    cbd_ref[:, 0:n] = jnp.where(diag, cre_ref[...], 0.0).astype(cbd_ref.dtype)
    cbd_ref[:, n:2 * n] = jnp.where(diag, -cim_ref[...], 0.0).astype(cbd_ref.dtype)


def _discretize(lambda_re, lambda_im, log_dt, b_re, b_im, c_re, c_im):
    g, p, c = b_re.shape
    n = g * p

    def tiled(z):
        return jnp.tile(z.reshape(g * c, p), (1, g))

    return pl.pallas_call(
        _disc_kernel,
        out_shape=(jax.ShapeDtypeStruct((2, n), F32), jax.ShapeDtypeStruct((g * c, 2 * n), BF16),
                   jax.ShapeDtypeStruct((g * c, 2 * n), BF16)),
        compiler_params=pltpu.CompilerParams(vmem_limit_bytes=VMEM_LIMIT),
        name="disc",
    )(lambda_re.reshape(1, n), lambda_im.reshape(1, n), jnp.repeat(log_dt, p).reshape(1, n),
      b_re.reshape(n, c), b_im.reshape(n, c), tiled(c_re), tiled(c_im))


def _rope_block(z, cos, sa, sb):
    up = pltpu.roll(z, LANES - ROT_DIM // 2, axis=1)
    dn = pltpu.roll(z, ROT_DIM // 2, axis=1)
    return z * cos + up * sa + dn * sb


def _split_heads(z, fill):
    low = lax.broadcasted_iota(jnp.int32, z.shape, 1) < HEAD_DIM
    return jnp.concatenate([jnp.where(low, z, fill),
                            jnp.where(low, pltpu.roll(z, HEAD_DIM, axis=1), fill)], axis=1)


def _inproj_kernel(x_ref, mod_ref, g_ref, w_ref, rope2_ref, rope1_ref,
                   q_ref, k_ref, v_ref, qi_ref, ki_ref, wt_ref, kp_ref, vx_ref, kip_ref,
                   ga_ref, u_ref, gs_ref, ma_ref, mb_ref):
    bb, tt, d = x_ref.shape
    rows = bb * tt
    x = x_ref[...]
    y = x * lax.rsqrt(jnp.mean(x * x, axis=-1, keepdims=True) + NORM_EPS) * g_ref[...]
    shift = mod_ref[:, :, 0:d]
    scale = mod_ref[:, :, d:2 * d]
    h = (y * (1.0 + scale) + shift).reshape(rows, d).astype(BF16)

    def seg(name):
        a, b = _SEG_OFF[name]
        return jnp.dot(h, w_ref[:, a:b], preferred_element_type=F32)

    def roped(z, tab_ref):
        cos, sa, sb = tab_ref[0], tab_ref[1], tab_ref[2]
        blocks = [_rope_block(z[:, i:i + LANES], cos, sa, sb) for i in range(0, z.shape[1], LANES)]
        return blocks[0] if len(blocks) == 1 else jnp.concatenate(blocks, axis=1)

    def put(ref, val):
        ref[...] = val.reshape(ref.shape).astype(ref.dtype)

    def split_all(z, fill=0.0):
        return jnp.concatenate([_split_heads(z[:, i:i + LANES], fill)
                                for i in range(0, z.shape[1], LANES)], axis=1)

    def token_lanes(z, b, lanes):
        blk = z[b * tt:(b + 1) * tt]
        if lanes > tt:
            blk = jnp.concatenate([blk, jnp.zeros((lanes - tt, LANES), F32)], axis=0)
        return blk.T

    def put_heads_t(ref, z):
        for b in range(bb):
            for pair in range(ref.shape[1] // 2):
                t = token_lanes(z[:, pair * LANES:(pair + 1) * LANES], b, ref.shape[3])
                ref[b, 2 * pair] = t[:HEAD_DIM].astype(ref.dtype)
                ref[b, 2 * pair + 1] = t[HEAD_DIM:].astype(ref.dtype)

    put_heads_t(q_ref, roped(seg("q"), rope2_ref) * (HEAD_DIM ** -0.5 * math.log2(math.e)))
    kv = seg("kv")
    k = roped(kv[:, :KV_WIDTH], rope2_ref)
    put(k_ref, k)
    put(kp_ref, split_all(k))
    v = kv[:, KV_WIDTH:]
    put(v_ref, v)
    n_kb = vx_ref.shape[2]
    ones = jnp.ones((VX_ROWS - HEAD_DIM, KEY_BLOCK), F32)
    for b in range(bb):
        vt = token_lanes(v, b, n_kb * KEY_BLOCK)
        for g in range(N_KV_HEADS):
            for kb_i in range(n_kb):
                blk = vt[g * HEAD_DIM:(g + 1) * HEAD_DIM, kb_i * KEY_BLOCK:(kb_i + 1) * KEY_BLOCK]
                vx_ref[b, g, kb_i] = jnp.concatenate([blk, ones], axis=0).astype(vx_ref.dtype)
    put_heads_t(qi_ref, roped(seg("qi"), rope2_ref) * (IDX_DIM ** -0.5))
    kiwi = roped(seg("kiwi"), rope1_ref)
    for b in range(bb):
        wt_ref[b] = token_lanes(kiwi, b, wt_ref.shape[2])[_WI_COL:_WI_COL + N_IDX_HEADS]
    put(ki_ref, kiwi[:, 0:IDX_DIM])
    put(kip_ref, kiwi[:, 0:IDX_DIM])
    put(ga_ref, jax.nn.silu(seg("ga")))
    put(u_ref, seg("u"))
    put(gs_ref, jax.nn.silu(seg("gs")))
    put(ma_ref, jax.nn.sigmoid(seg("ma")))
    put(mb_ref, jax.nn.sigmoid(seg("mb")))


def _rope_tables(pos, heads_in_block):
    half = ROT_DIM // 2
    inv = jnp.power(ROPE_THETA, -jnp.arange(half, dtype=F32) * (2.0 / ROT_DIM))
    ang = pos.astype(F32)[:, None] * inv[None, :]
    cos, sin = jnp.cos(ang), jnp.sin(ang)
    t = pos.shape[0]
    ones = jnp.ones((t, HEAD_DIM - ROT_DIM), F32)
    zeros = jnp.zeros((t, HEAD_DIM - ROT_DIM), F32)
    zh = jnp.zeros((t, half), F32)
    c_head = jnp.concatenate([cos, cos, ones], axis=1)
    sa_head = jnp.concatenate([-sin, zh, zeros], axis=1)
    sb_head = jnp.concatenate([zh, sin, zeros], axis=1)
    n_id = LANES // HEAD_DIM - heads_in_block
    ident = [jnp.ones((t, HEAD_DIM), F32)] * n_id
    zero = [jnp.zeros((t, HEAD_DIM), F32)] * n_id
    c = jnp.concatenate([c_head] * heads_in_block + ident, axis=1)
    sa = jnp.concatenate([sa_head] * heads_in_block + zero, axis=1)
    sb = jnp.concatenate([sb_head] * heads_in_block + zero, axis=1)
    return jnp.stack([c, sa, sb])


def _row_blocking(b, t):
    tt = min(t, ROW_TILE)
    bb = ROW_TILE // tt
    assert t % tt == 0 and b % bb == 0 and tt % 16 == 0
    return bb, tt


def _inproj(x, mod, g_norm, w_packed, pos):
    b, t, d = x.shape
    bb, tt = _row_blocking(b, t)
    rows = bb * tt
    rope2 = jnp.tile(_rope_tables(pos, 2), (1, bb, 1)) if bb > 1 else _rope_tables(pos, 2)
    rope1 = jnp.tile(_rope_tables(pos, 1), (1, bb, 1)) if bb > 1 else _rope_tables(pos, 1)

    def tok(width, dtype):
        return (jax.ShapeDtypeStruct((b, t, width), dtype),
                pl.BlockSpec((bb, tt, width), lambda i, j: (i, j, 0)))

    tq = max(tt, LANES)
    nkb = -(-tt // KEY_BLOCK)

    def heads_t(n):
        return (jax.ShapeDtypeStruct((b, n, HEAD_DIM, t // tt * tq), BF16),
                pl.BlockSpec((bb, n, HEAD_DIM, tq), lambda i, j: (i, 0, 0, j)))

    w_t = (jax.ShapeDtypeStruct((b, N_IDX_HEADS, t // tt * tq), F32),
           pl.BlockSpec((bb, N_IDX_HEADS, tq), lambda i, j: (i, 0, j)))
    vx_t = (jax.ShapeDtypeStruct((b, N_KV_HEADS, t // tt * nkb, VX_ROWS, KEY_BLOCK), BF16),
            pl.BlockSpec((bb, N_KV_HEADS, nkb, VX_ROWS, KEY_BLOCK), lambda i, j: (i, 0, j, 0, 0)))

    outs = [heads_t(N_HEADS), tok(KV_WIDTH, F32), tok(KV_WIDTH, F32),
            heads_t(N_IDX_HEADS), tok(IDX_DIM, F32), w_t,
            tok(2 * KV_WIDTH, BF16), vx_t, tok(IDX_DIM, BF16),
            tok(ATTN_WIDTH, BF16), tok(SSM_WIDTH, BF16),
            tok(SSM_WIDTH, BF16), tok(D_MODEL, BF16), tok(D_MODEL, BF16)]
    return pl.pallas_call(
        _inproj_kernel,
        grid=(b // bb, t // tt),
        in_specs=[
            pl.BlockSpec((bb, tt, d), lambda i, j: (i, j, 0)),
            pl.BlockSpec((bb, 1, 3 * d), lambda i, j: (i, 0, 0)),
            _const_spec((1, 1, d)),
            pl.BlockSpec((d, IN_PACKED), lambda i, j: (0, 0), pipeline_mode=pl.Buffered(1)),
            pl.BlockSpec((3, rows, LANES), lambda i, j: (0, j, 0)),
            pl.BlockSpec((3, rows, LANES), lambda i, j: (0, j, 0)),
        ],
        out_specs=[o[1] for o in outs],
        out_shape=[o[0] for o in outs],
        compiler_params=_cparams(("parallel", "parallel")),
        name="inproj",
    )(x, mod.reshape(b, 1, 3 * d), g_norm.reshape(1, 1, d), w_packed, rope2, rope1)


ATTEND_ROWS = 1024


def _float_of_key(key):
    bits = jnp.where(key >= 0, key, key ^ jnp.int32(0x7FFFFFFF))
    return lax.bitcast_convert_type(bits, F32)


def _count(sc_ref, bi, n_half, pred):
    half = sc_ref.shape[2] // 2
    acc = None
    for blk in range(-(-n_half // 2)):
        x = sc_ref[bi, blk] if 2 * blk + 1 < n_half else sc_ref[bi, blk, 0:half, :]
        hit = jnp.where(pred(x), 1.0, 0.0)
        part = jnp.sum(hit.reshape(-1, 8, hit.shape[1]), axis=0)
        acc = part if acc is None else acc + part
    return jnp.sum(acc, axis=0, keepdims=True)


def _search_thresholds(sc_ref, thr_ref, need_ref, kf, nkb):
    nb, _, _, tq = sc_ref.shape

    def body(i, keys):
        bit = jnp.left_shift(jnp.int32(1), 31 - i)
        out = []
        for bi, key in enumerate(keys):
            cand = key ^ bit
            cf = _float_of_key(cand)
            cnt = _count(sc_ref, bi, nkb, lambda x, cf=cf: x >= cf)
            out.append(jnp.where(cnt >= kf, cand, key))
        return tuple(out)

    init = tuple(jnp.full((1, tq), INT_MIN, jnp.int32) for _ in range(nb))
    keys = lax.fori_loop(0, 32, body, init)
    for bi, key in enumerate(keys):
        thr = _float_of_key(key)
        need = kf - _count(sc_ref, bi, nkb, lambda x, thr=thr: x > thr)
        thr_ref[bi] = jnp.broadcast_to(thr, thr_ref.shape[1:])
        need_ref[bi] = jnp.broadcast_to(need, need_ref.shape[1:])


def _attend_kernel(*refs, has_past, variants, search_from, q_pos0, past_len, present_len, n_sel):
    q_ref, qi_ref, wt_ref, ga_ref = refs[:4]
    n_in = 10 if has_past else 7
    past = refs[4:7] if has_past else None
    present = refs[n_in - 3:n_in]
    o_ref = refs[n_in]
    sc_ref, thr_ref, need_ref, macc_ref, oacc_ref = refs[n_in + 1:]
    nb, tq = q_ref.shape[0], q_ref.shape[3]
    kb = KEY_BLOCK
    n_past = past_len // kb
    rep = N_HEADS // N_KV_HEADS
    j = pl.program_id(1)

    qpos = q_pos0 + j * tq + lax.broadcasted_iota(jnp.int32, (1, tq), 1)
    qchunk = qpos // CHUNK
    n_present = jnp.minimum((qchunk + 1) * CHUNK - past_len, present_len)
    kf = jnp.minimum(past_len + n_present, n_sel).astype(F32)
    last_chunk = (q_pos0 + (j + 1) * tq - 1) // CHUNK
    nkb_present = (jnp.minimum((last_chunk + 1) * CHUNK - past_len, present_len) + kb - 1) // kb

    def admissible(i):
        local = i * kb + lax.broadcasted_iota(jnp.int32, (kb, 1), 0)
        return (((past_len + local) // CHUNK) <= qchunk) & (local < present_len)

    def for_blocks(fn, carry=0):
        if n_past:
            carry = lax.fori_loop(0, n_past, lambda i, c: fn(past, i, i, c), carry)
        return lax.fori_loop(0, nkb_present, lambda i, c: fn(present, i, n_past + i, c), carry)

    def rows_of(i):
        return pl.ds(pl.multiple_of(i * kb, kb), kb)

    def keys_values(src, bi, i):
        kblk = src[1][bi, rows_of(i), :]
        if src is present:
            return [(kblk[:, g * PAD_HEAD:g * PAD_HEAD + HEAD_DIM], src[2][bi, g, i])
                    for g in range(N_KV_HEADS)]
        vt = src[2][bi, rows_of(i), :].T
        ones = jnp.ones((VX_ROWS - HEAD_DIM, kb), F32)
        out = []
        for g in range(N_KV_HEADS):
            kg = kblk if g == 0 else pltpu.roll(kblk, LANES - g * HEAD_DIM, axis=1)
            vg = jnp.concatenate([vt[g * HEAD_DIM:(g + 1) * HEAD_DIM], ones], axis=0)
            out.append((kg[:, :HEAD_DIM].astype(BF16), vg.astype(BF16)))
        return out

    def search_path():
        qi_all = [jnp.concatenate([qi_ref[bi, h] for h in range(N_IDX_HEADS)], axis=1)
                  for bi in range(nb)]
        wts = [[wt_ref[bi, h:h + 1, :] * (N_IDX_HEADS ** -0.5) for h in range(N_IDX_HEADS)]
               for bi in range(nb)]

        def score_block(src, i, blk, c):
            for bi in range(nb):
                s = jnp.dot(src[0][bi, rows_of(i), :].astype(BF16), qi_all[bi],
                            preferred_element_type=F32)
                acc = None
                for h in range(N_IDX_HEADS):
                    term = wts[bi][h] * jnp.maximum(s[:, h * tq:(h + 1) * tq], 0.0)
                    acc = term if acc is None else acc + term
                if src is present:
                    acc = jnp.where(admissible(i), acc, -jnp.inf)
                sc_ref[bi, blk] = acc
            return c

        for_blocks(score_block)

        for lo, hi, nkb in variants:
            if len(variants) == 1:
                _search_thresholds(sc_ref, thr_ref, need_ref, kf, nkb)
            else:
                pl.when((j >= lo) & (j < hi))(
                    functools.partial(_search_thresholds, sc_ref, thr_ref, need_ref, kf, nkb))

        tri = (lax.broadcasted_iota(jnp.int32, (kb, kb), 1)
               <= lax.broadcasted_iota(jnp.int32, (kb, kb), 0)).astype(BF16)

        def bias_block(src, i, blk, seen):
            out = []
            for bi in range(nb):
                x = sc_ref[bi, blk]
                thr = thr_ref[bi, 0:1, :]
                eq = x == thr
                e = jnp.where(eq, 1.0, 0.0).astype(BF16)
                inblock = jnp.dot(tri, e, preferred_element_type=F32)
                tie = jnp.where(inblock + seen[bi] <= need_ref[bi, 0:1, :], 0.0, -jnp.inf)
                sc_ref[bi, blk] = jnp.where(eq, tie, jnp.where(x > thr, 0.0, -jnp.inf))
                out.append(seen[bi] + inblock[kb - 1:kb, :])
            return tuple(out)

        for_blocks(bias_block, tuple(jnp.zeros((1, tq), F32) for _ in range(nb)))

    def all_admissible_path():
        def bias_block(src, i, blk, c):
            for bi in range(nb):
                if src is present:
                    sc_ref[bi, blk] = jnp.where(admissible(i), 0.0, -jnp.inf)
                else:
                    sc_ref[bi, blk] = jnp.zeros((kb, tq), F32)
            return c

        for_blocks(bias_block)

    if search_from == 0:
        search_path()
    else:
        pl.when(j >= search_from)(search_path)
        pl.when(j < search_from)(all_admissible_path)

    q_all = [[jnp.concatenate([q_ref[bi, g * rep + h] for h in range(rep)], axis=1)
              for g in range(N_KV_HEADS)] for bi in range(nb)]

    macc_ref[...] = jnp.full(macc_ref.shape, jnp.finfo(F32).min, F32)
    oacc_ref[...] = jnp.zeros(oacc_ref.shape, F32)

    def pv_block(src, i, blk, c):
        units = [(bi, g, kg, vxg) for bi in range(nb)
                 for g, (kg, vxg) in enumerate(keys_values(src, bi, i))]
        biases = [jnp.concatenate([sc_ref[bi, blk]] * rep, axis=1) for bi in range(nb)]
        logits = [jnp.dot(kg, q_all[bi][g], preferred_element_type=F32) + biases[bi]
                  for bi, g, kg, _ in units]
        probs, alphas = [], []
        for (bi, g, _, _), lg in zip(units, logits):
            m_old = macc_ref[bi, g, 0:1, :]
            blk_max = jnp.max(jnp.max(lg.reshape(-1, 8, lg.shape[1]), axis=0), axis=0,
                              keepdims=True)
            m_new = jnp.maximum(m_old, blk_max)
            probs.append(jnp.exp2(lg - m_new).astype(BF16))
            alphas.append(jnp.exp2(m_old - m_new))
            macc_ref[bi, g] = jnp.broadcast_to(m_new, macc_ref.shape[2:])
        for (bi, g, _, vxg), p, alpha in zip(units, probs, alphas):
            oacc_ref[bi, g] = (oacc_ref[bi, g] * alpha
                               + jnp.dot(vxg, p, preferred_element_type=F32))
        return c

    for_blocks(pv_block)

    t_out = ga_ref.shape[1]
    for bi in range(nb):
        outs = []
        for g in range(N_KV_HEADS):
            acc = oacc_ref[bi, g]
            o = acc[:HEAD_DIM] / acc[HEAD_DIM:HEAD_DIM + 1]
            for h in range(0, rep, 2):
                pair = jnp.concatenate([o[:, h * tq:(h + 1) * tq], o[:, (h + 1) * tq:(h + 2) * tq]],
                                       axis=0)
                outs.append(pair.T)
        o_all = jnp.concatenate(outs, axis=1)[:t_out]
        o_ref[bi] = (o_all * ga_ref[bi].astype(F32)).astype(o_ref.dtype)


def _attend_variants(nq, tq, q_pos0, past_len, present_len, n_sel):
    kb = KEY_BLOCK // 2
    variants, search_from = [], nq
    for j in range(nq):
        n_present = min(((q_pos0 + (j + 1) * tq - 1) // CHUNK + 1) * CHUNK - past_len, present_len)
        if past_len + n_present <= n_sel:
            assert not variants
            continue
        search_from = min(search_from, j)
        nkb = past_len // kb + -(-n_present // kb)
        if variants and variants[-1][2] == nkb:
            variants[-1] = (variants[-1][0], j + 1, nkb)
        else:
            variants.append((j, j + 1, nkb))
    return search_from, tuple(variants)


def _attend(q, qi, wt, ga, present, past, q_pos0, past_len, present_len):
    b, _, _, t_pad = q.shape
    t = ga.shape[1]
    kb = KEY_BLOCK
    tq = LANES
    nq = t_pad // tq
    tr = t // nq
    nb = max(1, ATTEND_ROWS // tq)
    l_present = present[0].shape[1]
    assert b % nb == 0 and past_len % kb == 0 and l_present % kb == 0 and q_pos0 >= past_len
    assert t % nq == 0 and (nq == 1 or tr == tq)
    n_sel = min(TOPK_MAX, (past_len + present_len) // 4)
    search_from, variants = _attend_variants(nq, tq, q_pos0, past_len, present_len, n_sel)
    nkb_max = past_len // kb + l_present // kb
    kern = functools.partial(_attend_kernel, has_past=past is not None, variants=variants,
                             search_from=search_from, q_pos0=q_pos0, past_len=past_len,
                             present_len=present_len, n_sel=n_sel)

    def kspec(a):
        nd = a.ndim
        return pl.BlockSpec((nb,) + a.shape[1:], lambda i, j: (i,) + (0,) * (nd - 1),
                            pipeline_mode=pl.Buffered(1))

    keys = (tuple(past) if past is not None else ()) + tuple(present)
    return pl.pallas_call(
        kern,
        grid=(b // nb, nq),
        in_specs=[pl.BlockSpec((nb, N_HEADS, HEAD_DIM, tq), lambda i, j: (i, 0, 0, j)),
                  pl.BlockSpec((nb, N_IDX_HEADS, IDX_DIM, tq), lambda i, j: (i, 0, 0, j)),
                  pl.BlockSpec((nb, N_IDX_HEADS, tq), lambda i, j: (i, 0, j)),
                  pl.BlockSpec((nb, tr, ATTN_WIDTH), lambda i, j: (i, j, 0))]
        + [kspec(a) for a in keys],
        out_specs=pl.BlockSpec((nb, tr, ATTN_WIDTH), lambda i, j: (i, j, 0)),
        out_shape=jax.ShapeDtypeStruct((b, t, ATTN_WIDTH), BF16),
        scratch_shapes=[pltpu.VMEM((nb, nkb_max, kb, tq), F32),
                        pltpu.VMEM((nb, 8, tq), F32), pltpu.VMEM((nb, 8, tq), F32),
                        pltpu.VMEM((nb, N_KV_HEADS, 8, N_HEADS // N_KV_HEADS * tq), F32),
                        pltpu.VMEM((nb, N_KV_HEADS, VX_ROWS, N_HEADS // N_KV_HEADS * tq), F32)],
        compiler_params=_cparams(("parallel", "parallel")),
        name="attend",
    )(q, qi, wt, ga, *keys)


SSM_SLAB = LANES


def _ssm_kernel(u_ref, h0r_ref, h0i_ref, a_ref, bbd_ref, cbd_ref, d_ref,
                y_ref, hr_ref, hi_ref, s_ref, st_ref):
    tt, nb, w = u_ref.shape
    rows = tt * nb
    n = SSM_LANES

    @pl.when(pl.program_id(0) == 0)
    def _():
        st_ref[0] = h0r_ref[...]
        st_ref[1] = h0i_ref[...]

    u2 = u_ref[...].reshape(rows, w)
    lanes_per_slab = SSM_SLAB // SSM_GROUP * SSM_STATE
    ys = []
    for q in range(w // SSM_SLAB):
        ch = slice(q * SSM_SLAB, (q + 1) * SSM_SLAB)
        re = slice(q * lanes_per_slab, (q + 1) * lanes_per_slab)
        im = slice(n + q * lanes_per_slab, n + (q + 1) * lanes_per_slab)
        s_ref[:, re] = jnp.dot(u2[:, ch], bbd_ref[ch, re], preferred_element_type=F32)
        s_ref[:, im] = jnp.dot(u2[:, ch], bbd_ref[ch, im], preferred_element_type=F32)
        ar = jnp.broadcast_to(a_ref[0:1, re], (nb, lanes_per_slab))
        ai = jnp.broadcast_to(a_ref[1:2, re], (nb, lanes_per_slab))
        hr, hi = st_ref[0, :, re], st_ref[1, :, re]
        for t in range(tt):
            rws = slice(t * nb, (t + 1) * nb)
            hr, hi = (ar * hr - ai * hi + s_ref[rws, re], ar * hi + ai * hr + s_ref[rws, im])
            s_ref[rws, re] = hr
            s_ref[rws, im] = hi
        st_ref[0, :, re] = hr
        st_ref[1, :, re] = hi
        nt = (((1,), (1,)), ((), ()))
        ys.append(lax.dot_general(s_ref[:, re].astype(BF16), cbd_ref[ch, re], nt,
                                  preferred_element_type=F32)
                  + lax.dot_general(s_ref[:, im].astype(BF16), cbd_ref[ch, im], nt,
                                    preferred_element_type=F32))
    y = jnp.concatenate(ys, axis=1)
    y = y + d_ref[...] * u2.astype(F32)
    y_ref[...] = jax.nn.gelu(y).astype(y_ref.dtype).reshape(tt, nb, w)
    hr_ref[...] = st_ref[0]
    hi_ref[...] = st_ref[1]


def _ssm(u_tb, h0r, h0i, a2, bbd, cbd, d_skip):
    t, nb, w = u_tb.shape
    tt = max(1, 512 // nb)
    assert t % tt == 0
    n = SSM_LANES
    return pl.pallas_call(
        _ssm_kernel,
        grid=(t // tt,),
        in_specs=[pl.BlockSpec((tt, nb, w), lambda i: (i, 0, 0)),
                  _const_spec((nb, n)), _const_spec((nb, n)), _const_spec((2, n)),
                  _const_spec((w, 2 * n)), _const_spec((w, 2 * n)), _const_spec((1, w))],
        out_specs=[pl.BlockSpec((tt, nb, w), lambda i: (i, 0, 0)),
                   _const_spec((nb, n)), _const_spec((nb, n))],
        out_shape=[jax.ShapeDtypeStruct((t, nb, w), BF16),
                   jax.ShapeDtypeStruct((nb, n), F32), jax.ShapeDtypeStruct((nb, n), F32)],
        scratch_shapes=[pltpu.VMEM((tt * nb, 2 * n), F32), pltpu.VMEM((2, nb, n), F32)],
        compiler_params=_cparams(("arbitrary",)),
        name="ssm",
    )(u_tb, h0r, h0i, a2, bbd, cbd, d_skip.reshape(1, w))


def _outproj_kernel(x_ref, mod_ref, a_ref, yg_ref, gs_ref, ma_ref, mb_ref,
                    wa_ref, wg_ref, ws_ref, wo_ref, gf_ref, y_ref):
    bb, tt, d = x_ref.shape
    rows = bb * tt

    def flat(ref):
        return ref[...].reshape(rows, ref.shape[-1])

    branch_a = jnp.dot(flat(a_ref), wa_ref[...], preferred_element_type=F32)
    g_lin = jnp.dot(flat(yg_ref), wg_ref[...], preferred_element_type=F32)
    y_glu = g_lin[:, :SSM_WIDTH] * jax.nn.sigmoid(g_lin[:, SSM_WIDTH:])
    gated = y_glu * flat(gs_ref).astype(F32)
    branch_b = jnp.dot(gated.astype(BF16), ws_ref[...], preferred_element_type=F32)
    merged = flat(ma_ref).astype(F32) * branch_a + flat(mb_ref).astype(F32) * branch_b
    proj = jnp.dot(merged.astype(BF16), wo_ref[...], preferred_element_type=F32)
    gate = mod_ref[:, :, 2 * d:3 * d]
    xo = x_ref[...] + gate * proj.reshape(bb, tt, d)
    y = xo * lax.rsqrt(jnp.mean(xo * xo, axis=-1, keepdims=True) + NORM_EPS) * gf_ref[...]
    y_ref[...] = y


def _outproj(x, mod, a, yg, gs, ma, mb, wa, wg, ws, wo, g_final):
    b, t, d = x.shape
    bb, tt = _row_blocking(b, t)

    def tok(width):
        return pl.BlockSpec((bb, tt, width), lambda i, j: (i, j, 0))

    return pl.pallas_call(
        _outproj_kernel,
        grid=(b // bb, t // tt),
        in_specs=[tok(d), pl.BlockSpec((bb, 1, 3 * d), lambda i, j: (i, 0, 0)),
                  tok(ATTN_WIDTH), tok(SSM_WIDTH), tok(SSM_WIDTH), tok(d), tok(d),
                  _const_spec(wa.shape), _const_spec(wg.shape), _const_spec(ws.shape),
                  _const_spec(wo.shape), _const_spec((1, 1, d))],
        out_specs=tok(d),
        out_shape=jax.ShapeDtypeStruct((b, t, d), F32),
        compiler_params=_cparams(("parallel", "parallel")),
        name="outproj",
    )(x, mod.reshape(b, 1, 3 * d), a, yg, gs, ma, mb, wa, wg, ws, wo, g_final.reshape(1, 1, d))


def _pad_rows(a, rows):
    return jnp.pad(a, ((0, 0), (0, rows - a.shape[1]), (0, 0)))


def _layer(x, mod, pos0, past, prm):
    (g_norm, w_packed, a2, bbd, cbd, d_skip, wg, wa, ws, wo, g_final) = prm
    b, t, _ = x.shape
    pos = pos0 + jnp.arange(t, dtype=jnp.int32)
    q, k, v, qi, ki, wt, kp, vx, kip, ga, u, gs, ma, mb = _inproj(x, mod, g_norm, w_packed, pos)

    t_pad = -(-t // KEY_BLOCK) * KEY_BLOCK
    present = ((_pad_rows(kip, t_pad), _pad_rows(kp, t_pad)) if t_pad != t else (kip, kp)) + (vx,)
    if past is None:
        past_keys, past_len = None, 0
        h0r = jnp.zeros((b, SSM_LANES), F32)
        h0i = jnp.zeros((b, SSM_LANES), F32)
    else:
        ck, cv, cki, h0r, h0i = past
        past_len = ck.shape[1]
        past_keys = (cki, ck.reshape(b, past_len, KV_WIDTH), cv.reshape(b, past_len, KV_WIDTH))
        h0r = h0r.reshape(b, SSM_LANES)
        h0i = h0i.reshape(b, SSM_LANES)

    a = _attend(q, qi, wt, ga, present, past_keys, pos0, past_len, t)

    yg_tb, hr, hi = _ssm(jnp.swapaxes(u, 0, 1), h0r, h0i, a2, bbd, cbd, d_skip)
    yg = jnp.swapaxes(yg_tb, 0, 1)

    y = _outproj(x, mod, a, yg, gs, ma, mb, wa, wg, ws, wo, g_final)
    return (y, k.reshape(b, t, N_KV_HEADS, HEAD_DIM), v.reshape(b, t, N_KV_HEADS, HEAD_DIM), ki,
            hr.reshape(b, N_SSM_GROUPS, SSM_STATE), hi.reshape(b, N_SSM_GROUPS, SSM_STATE))


def kernel(x_prompt, x_sample, cache_k, cache_v, cache_idx_k, state_ssm_re, state_ssm_im,
           c_prompt, c_sample, w_mod, b_mod, g_norm, w_in, lambda_re, lambda_im, log_dt,
           ssm_b_re, ssm_b_im, ssm_c_re, ssm_c_im, d_skip, w_glu, w_attn_proj, w_ssm_proj,
           w_out, g_final):
    depth = w_in.shape[0]
    assert depth == 1, "the final norm is fused into the (single) layer's output kernel"
    nbp = x_prompt.shape[0]
    past_len = cache_k.shape[2]

    def layer0(a):
        return a.reshape(a.shape[1:])

    mod = _modulation(jnp.concatenate([c_prompt, c_sample], axis=0), layer0(w_mod), layer0(b_mod))
    mod_p, mod_s = mod[:nbp], mod[nbp:]

    cut = _SEG_OFF["kiwi"][0] + IDX_DIM + N_IDX_HEADS
    w = layer0(w_in)
    col = lax.broadcasted_iota(jnp.int32, (1, IN_PACKED), 1)
    w_packed = jnp.where(col < cut, jnp.pad(w, ((0, 0), (0, _KIWI_PAD))),
                         jnp.where(col < cut + _KIWI_PAD, 0.0,
                                   jnp.pad(w, ((0, 0), (_KIWI_PAD, 0))))).astype(BF16)

    a2, bbd, cbd = _discretize(*map(layer0, (lambda_re, lambda_im, log_dt, ssm_b_re, ssm_b_im,
                                             ssm_c_re, ssm_c_im)))

    prm = (layer0(g_norm), w_packed, a2, bbd, cbd, layer0(d_skip), layer0(w_glu).astype(BF16),
           layer0(w_attn_proj).astype(BF16), layer0(w_ssm_proj).astype(BF16),
           layer0(w_out).astype(BF16), g_final)

    yp, kp, vp, kip, hrp, hip = _layer(x_prompt, mod_p, 0, None, prm)
    past = tuple(map(layer0, (cache_k, cache_v, cache_idx_k, state_ssm_re, state_ssm_im)))
    ys, ks, vs, kis, hrs, his = _layer(x_sample, mod_s, past_len, past, prm)

    def st(z):
        return z[None]

    return (yp, ys, st(kp), st(vp), st(kip), st(hrp), st(hip),
            st(ks), st(vs), st(kis), st(hrs), st(his))
```

```python
import functools
import math

import jax
import jax.numpy as jnp
from jax import lax
from jax.experimental import pallas as pl
from jax.experimental.pallas import tpu as pltpu

F32 = jnp.float32
BF16 = jnp.bfloat16

LANES = 128
VMEM_LIMIT = 56 * 1024 * 1024

D_MODEL = 1024
CHUNK = 64
QBLK = 128
N_HEADS = 8
N_KV_HEADS = 2
HEAD_DIM = 64
ATTN_WIDTH = N_HEADS * HEAD_DIM
KV_WIDTH = N_KV_HEADS * HEAD_DIM
ROT_DIM = HEAD_DIM // 4
ROPE_THETA = 500000.0
N_IDX_HEADS = 8
IDX_DIM = 64
IDX_WIDTH = N_IDX_HEADS * IDX_DIM
TOPK_MAX = 256
SSM_WIDTH = D_MODEL // 2
SSM_GROUP = 16
N_SSM_GROUPS = SSM_WIDTH // SSM_GROUP
SSM_STATE = 64
SSM_LANES = N_SSM_GROUPS * SSM_STATE
NORM_EPS = 1e-6

_KIWI_PAD = LANES - IDX_DIM - N_IDX_HEADS
_SEG_WIDTHS = (ATTN_WIDTH, 2 * KV_WIDTH, IDX_WIDTH, LANES, ATTN_WIDTH, SSM_WIDTH, SSM_WIDTH,
               D_MODEL, D_MODEL)
_SEG_NAMES = ("q", "kv", "qi", "kiwi", "ga", "u", "gs", "ma", "mb")
_SEG_OFF = {}
_off = 0
for _n, _w in zip(_SEG_NAMES, _SEG_WIDTHS):
    _SEG_OFF[_n] = (_off, _off + _w)
    _off += _w
IN_PACKED = _off
_WI_COL = IDX_DIM

ROW_TILE = 512
PAD_HEAD = 2 * HEAD_DIM
KEY_BLOCK = 256
VX_ROWS = HEAD_DIM + 16
INT_MIN = -2 ** 31


def _cparams(sem):
    return pltpu.CompilerParams(dimension_semantics=sem, vmem_limit_bytes=VMEM_LIMIT)


def _const_spec(shape):
    nd = len(shape)
    return pl.BlockSpec(shape, lambda *_: (0,) * nd)


def _mod_kernel(c_ref, w_ref, b_ref, o_ref):
    s = jax.nn.silu(c_ref[...])
    o_ref[...] = (jnp.dot(s.astype(BF16), w_ref[...].astype(BF16), preferred_element_type=F32)
                  + b_ref[...])


def _modulation(c, w_mod, b_mod):
    n = c.shape[0]
    return pl.pallas_call(
        _mod_kernel,
        out_shape=jax.ShapeDtypeStruct((n, 3 * D_MODEL), F32),
        compiler_params=pltpu.CompilerParams(vmem_limit_bytes=VMEM_LIMIT),
        name="mod",
    )(c, w_mod, b_mod.reshape(1, 3 * D_MODEL))


def _disc_kernel(lre_ref, lim_ref, ldt_ref, bre_ref, bim_ref, cre_ref, cim_ref,
                 a_ref, bbd_ref, cbd_ref):
    n = lre_ref.shape[1]
    dt = jnp.exp(ldt_ref[...])
    lr, li = lre_ref[...], lim_ref[...]
    mag = jnp.exp(lr * dt)
    ar, ai = mag * jnp.cos(li * dt), mag * jnp.sin(li * dt)
    den = lr * lr + li * li
    zr = ((ar - 1.0) * lr + ai * li) / den
    zi = (ai * lr - (ar - 1.0) * li) / den
    a_ref[0:1, :] = ar
    a_ref[1:2, :] = ai

    eye = (lax.broadcasted_iota(jnp.int32, (SSM_GROUP, SSM_GROUP), 0)
           == lax.broadcasted_iota(jnp.int32, (SSM_GROUP, SSM_GROUP), 1)).astype(BF16)

    def transposed(x):
        out, rest = None, x
        for _ in range(3):
            limb = rest.astype(BF16)
            rest = rest - limb.astype(F32)
            t = lax.dot_general(eye, limb, (((1,), (1,)), ((), ())), preferred_element_type=F32)
            out = t if out is None else out + t
        return out

    br, bi = transposed(bre_ref[...]), transposed(bim_ref[...])
    bbar_re = zr * br - zi * bi
    bbar_im = zr * bi + zi * br
    group_of_lane = lax.broadcasted_iota(jnp.int32, (SSM_GROUP, n), 1) // SSM_STATE
    for g in range(n // SSM_STATE):
        rows = slice(g * SSM_GROUP, (g + 1) * SSM_GROUP)
        mine = group_of_lane == g
        bbd_ref[rows, 0:n] = jnp.where(mine, bbar_re, 0.0).astype(bbd_ref.dtype)
        bbd_ref[rows, n:2 * n] = jnp.where(mine, bbar_im, 0.0).astype(bbd_ref.dtype)
    rows_n = cre_ref.shape[0]
    diag = (lax.broadcasted_iota(jnp.int32, (rows_n, n), 0) // SSM_GROUP
            == lax.broadcasted_iota(jnp.int32, (rows_n, n), 1) // SSM_STATE)
    cbd_ref[:, 0:n] = jnp.where(diag, cre_ref[...], 0.0).astype(cbd_ref.dtype)
    cbd_ref[:, n:2 * n] = jnp.where(diag, -cim_ref[...], 0.0).astype(cbd_ref.dtype)


def _discretize(lambda_re, lambda_im, log_dt, b_re, b_im, c_re, c_im):
    g, p, c = b_re.shape
    n = g * p

    def tiled(z):
        return jnp.tile(z.reshape(g * c, p), (1, g))

    return pl.pallas_call(
        _disc_kernel,
        out_shape=(jax.ShapeDtypeStruct((2, n), F32), jax.ShapeDtypeStruct((g * c, 2 * n), BF16),
                   jax.ShapeDtypeStruct((g * c, 2 * n), BF16)),
        compiler_params=pltpu.CompilerParams(vmem_limit_bytes=VMEM_LIMIT),
        name="disc",
    )(lambda_re.reshape(1, n), lambda_im.reshape(1, n), jnp.repeat(log_dt, p).reshape(1, n),
      b_re.reshape(n, c), b_im.reshape(n, c), tiled(c_re), tiled(c_im))


def _rope_block(z, cos, sa, sb):
    up = pltpu.roll(z, LANES - ROT_DIM // 2, axis=1)
    dn = pltpu.roll(z, ROT_DIM // 2, axis=1)
    return z * cos + up * sa + dn * sb


def _split_heads(z, fill):
    low = lax.broadcasted_iota(jnp.int32, z.shape, 1) < HEAD_DIM
    return jnp.concatenate([jnp.where(low, z, fill),
                            jnp.where(low, pltpu.roll(z, HEAD_DIM, axis=1), fill)], axis=1)


def _inproj_kernel(x_ref, mod_ref, g_ref, w_ref, rope2_ref, rope1_ref,
                   q_ref, k_ref, v_ref, qi_ref, ki_ref, wt_ref, kp_ref, vx_ref, kip_ref,
                   ga_ref, u_ref, gs_ref, ma_ref, mb_ref):
    bb, tt, d = x_ref.shape
    rows = bb * tt
    x = x_ref[...]
    y = x * lax.rsqrt(jnp.mean(x * x, axis=-1, keepdims=True) + NORM_EPS) * g_ref[...]
    shift = mod_ref[:, :, 0:d]
    scale = mod_ref[:, :, d:2 * d]
    h = (y * (1.0 + scale) + shift).reshape(rows, d).astype(BF16)

    def seg(name):
        a, b = _SEG_OFF[name]
        return jnp.dot(h, w_ref[:, a:b], preferred_element_type=F32)

    def roped(z, tab_ref):
        cos, sa, sb = tab_ref[0], tab_ref[1], tab_ref[2]
        blocks = [_rope_block(z[:, i:i + LANES], cos, sa, sb) for i in range(0, z.shape[1], LANES)]
        return blocks[0] if len(blocks) == 1 else jnp.concatenate(blocks, axis=1)

    def put(ref, val):
        ref[...] = val.reshape(ref.shape).astype(ref.dtype)

    def split_all(z, fill=0.0):
        return jnp.concatenate([_split_heads(z[:, i:i + LANES], fill)
                                for i in range(0, z.shape[1], LANES)], axis=1)

    def token_lanes(z, b, lanes):
        blk = z[b * tt:(b + 1) * tt]
        if lanes > tt:
            blk = jnp.concatenate([blk, jnp.zeros((lanes - tt, LANES), F32)], axis=0)
        return blk.T

    def put_heads_t(ref, z):
        for b in range(bb):
            for pair in range(ref.shape[1] // 2):
                t = token_lanes(z[:, pair * LANES:(pair + 1) * LANES], b, ref.shape[3])
                ref[b, 2 * pair] = t[:HEAD_DIM].astype(ref.dtype)
                ref[b, 2 * pair + 1] = t[HEAD_DIM:].astype(ref.dtype)

    put_heads_t(q_ref, roped(seg("q"), rope2_ref) * (HEAD_DIM ** -0.5 * math.log2(math.e)))
    kv = seg("kv")
    k = roped(kv[:, :KV_WIDTH], rope2_ref)
    put(k_ref, k)
    put(kp_ref, split_all(k))
    v = kv[:, KV_WIDTH:]
    put(v_ref, v)
    n_kb = vx_ref.shape[2]
    ones = jnp.ones((VX_ROWS - HEAD_DIM, KEY_BLOCK), F32)
    for b in range(bb):
        vt = token_lanes(v, b, n_kb * KEY_BLOCK)
        for g in range(N_KV_HEADS):
            for kb_i in range(n_kb):
                blk = vt[g * HEAD_DIM:(g + 1) * HEAD_DIM, kb_i * KEY_BLOCK:(kb_i + 1) * KEY_BLOCK]
                vx_ref[b, g, kb_i] = jnp.concatenate([blk, ones], axis=0).astype(vx_ref.dtype)
    put_heads_t(qi_ref, roped(seg("qi"), rope2_ref) * (IDX_DIM ** -0.5))
    kiwi = roped(seg("kiwi"), rope1_ref)
    for b in range(bb):
        wt_ref[b] = token_lanes(kiwi, b, wt_ref.shape[2])[_WI_COL:_WI_COL + N_IDX_HEADS]
    put(ki_ref, kiwi[:, 0:IDX_DIM])
    put(kip_ref, kiwi[:, 0:IDX_DIM])
    put(ga_ref, jax.nn.silu(seg("ga")))
    put(u_ref, seg("u"))
    put(gs_ref, jax.nn.silu(seg("gs")))
    put(ma_ref, jax.nn.sigmoid(seg("ma")))
    put(mb_ref, jax.nn.sigmoid(seg("mb")))


def _rope_tables(pos, heads_in_block):
    half = ROT_DIM // 2
    inv = jnp.power(ROPE_THETA, -jnp.arange(half, dtype=F32) * (2.0 / ROT_DIM))
    ang = pos.astype(F32)[:, None] * inv[None, :]
    cos, sin = jnp.cos(ang), jnp.sin(ang)
    t = pos.shape[0]
    ones = jnp.ones((t, HEAD_DIM - ROT_DIM), F32)
    zeros = jnp.zeros((t, HEAD_DIM - ROT_DIM), F32)
    zh = jnp.zeros((t, half), F32)
    c_head = jnp.concatenate([cos, cos, ones], axis=1)
    sa_head = jnp.concatenate([-sin, zh, zeros], axis=1)
    sb_head = jnp.concatenate([zh, sin, zeros], axis=1)
    n_id = LANES // HEAD_DIM - heads_in_block
    ident = [jnp.ones((t, HEAD_DIM), F32)] * n_id
    zero = [jnp.zeros((t, HEAD_DIM), F32)] * n_id
    c = jnp.concatenate([c_head] * heads_in_block + ident, axis=1)
    sa = jnp.concatenate([sa_head] * heads_in_block + zero, axis=1)
    sb = jnp.concatenate([sb_head] * heads_in_block + zero, axis=1)
    return jnp.stack([c, sa, sb])


def _row_blocking(b, t):
    tt = min(t, ROW_TILE)
    bb = ROW_TILE // tt
    assert t % tt == 0 and b % bb == 0 and tt % 16 == 0
    return bb, tt


def _inproj(x, mod, g_norm, w_packed, pos):
    b, t, d = x.shape
    bb, tt = _row_blocking(b, t)
    rows = bb * tt
    rope2 = jnp.tile(_rope_tables(pos, 2), (1, bb, 1)) if bb > 1 else _rope_tables(pos, 2)
    rope1 = jnp.tile(_rope_tables(pos, 1), (1, bb, 1)) if bb > 1 else _rope_tables(pos, 1)

    def tok(width, dtype):
        return (jax.ShapeDtypeStruct((b, t, width), dtype),
                pl.BlockSpec((bb, tt, width), lambda i, j: (i, j, 0)))

    tq = max(tt, LANES)
    nkb = -(-tt // KEY_BLOCK)

    def heads_t(n):
        return (jax.ShapeDtypeStruct((b, n, HEAD_DIM, t // tt * tq), BF16),
                pl.BlockSpec((bb, n, HEAD_DIM, tq), lambda i, j: (i, 0, 0, j)))

    w_t = (jax.ShapeDtypeStruct((b, N_IDX_HEADS, t // tt * tq), F32),
           pl.BlockSpec((bb, N_IDX_HEADS, tq), lambda i, j: (i, 0, j)))
    vx_t = (jax.ShapeDtypeStruct((b, N_KV_HEADS, t // tt * nkb, VX_ROWS, KEY_BLOCK), BF16),
            pl.BlockSpec((bb, N_KV_HEADS, nkb, VX_ROWS, KEY_BLOCK), lambda i, j: (i, 0, j, 0, 0)))

    outs = [heads_t(N_HEADS), tok(KV_WIDTH, F32), tok(KV_WIDTH, F32),
            heads_t(N_IDX_HEADS), tok(IDX_DIM, F32), w_t,
            tok(2 * KV_WIDTH, BF16), vx_t, tok(IDX_DIM, BF16),
            tok(ATTN_WIDTH, BF16), tok(SSM_WIDTH, BF16),
            tok(SSM_WIDTH, BF16), tok(D_MODEL, BF16), tok(D_MODEL, BF16)]
    return pl.pallas_call(
        _inproj_kernel,
        grid=(b // bb, t // tt),
        in_specs=[
            pl.BlockSpec((bb, tt, d), lambda i, j: (i, j, 0)),
            pl.BlockSpec((bb, 1, 3 * d), lambda i, j: (i, 0, 0)),
            _const_spec((1, 1, d)),
            pl.BlockSpec((d, IN_PACKED), lambda i, j: (0, 0), pipeline_mode=pl.Buffered(1)),
            pl.BlockSpec((3, rows, LANES), lambda i, j: (0, j, 0)),
            pl.BlockSpec((3, rows, LANES), lambda i, j: (0, j, 0)),
        ],
        out_specs=[o[1] for o in outs],
        out_shape=[o[0] for o in outs],
        compiler_params=_cparams(("parallel", "parallel")),
        name="inproj",
    )(x, mod.reshape(b, 1, 3 * d), g_norm.reshape(1, 1, d), w_packed, rope2, rope1)


ATTEND_ROWS = 1024


def _float_of_key(key):
    bits = jnp.where(key >= 0, key, key ^ jnp.int32(0x7FFFFFFF))
    return lax.bitcast_convert_type(bits, F32)


def _count(sc_ref, bi, n_half, pred):
    half = sc_ref.shape[2] // 2
    acc = None
    for blk in range(-(-n_half // 2)):
        x = sc_ref[bi, blk] if 2 * blk + 1 < n_half else sc_ref[bi, blk, 0:half, :]
        hit = jnp.where(pred(x), 1.0, 0.0)
        part = jnp.sum(hit.reshape(-1, 8, hit.shape[1]), axis=0)
        acc = part if acc is None else acc + part
    return jnp.sum(acc, axis=0, keepdims=True)


def _search_thresholds(sc_ref, thr_ref, need_ref, kf, nkb):
    nb, _, _, tq = sc_ref.shape

    def body(i, keys):
        bit = jnp.left_shift(jnp.int32(1), 31 - i)
        out = []
        for bi, key in enumerate(keys):
            cand = key ^ bit
            cf = _float_of_key(cand)
            cnt = _count(sc_ref, bi, nkb, lambda x, cf=cf: x >= cf)
            out.append(jnp.where(cnt >= kf, cand, key))
        return tuple(out)

    init = tuple(jnp.full((1, tq), INT_MIN, jnp.int32) for _ in range(nb))
    keys = lax.fori_loop(0, 32, body, init)
    for bi, key in enumerate(keys):
        thr = _float_of_key(key)
        need = kf - _count(sc_ref, bi, nkb, lambda x, thr=thr: x > thr)
        thr_ref[bi] = jnp.broadcast_to(thr, thr_ref.shape[1:])
        need_ref[bi] = jnp.broadcast_to(need, need_ref.shape[1:])


def _attend_kernel(*refs, has_past, variants, search_from, q_pos0, past_len, present_len, n_sel):
    q_ref, qi_ref, wt_ref, ga_ref = refs[:4]
    n_in = 10 if has_past else 7
    past = refs[4:7] if has_past else None
    present = refs[n_in - 3:n_in]
    o_ref = refs[n_in]
    sc_ref, thr_ref, need_ref, macc_ref, oacc_ref = refs[n_in + 1:]
    nb, tq = q_ref.shape[0], q_ref.shape[3]
    kb = KEY_BLOCK
    n_past = past_len // kb
    rep = N_HEADS // N_KV_HEADS
    j = pl.program_id(1)

    qpos = q_pos0 + j * tq + lax.broadcasted_iota(jnp.int32, (1, tq), 1)
    qchunk = qpos // CHUNK
    n_present = jnp.minimum((qchunk + 1) * CHUNK - past_len, present_len)
    kf = jnp.minimum(past_len + n_present, n_sel).astype(F32)
    last_chunk = (q_pos0 + (j + 1) * tq - 1) // CHUNK
    nkb_present = (jnp.minimum((last_chunk + 1) * CHUNK - past_len, present_len) + kb - 1) // kb

    def admissible(i):
        local = i * kb + lax.broadcasted_iota(jnp.int32, (kb, 1), 0)
        return (((past_len + local) // CHUNK) <= qchunk) & (local < present_len)

    def for_blocks(fn, carry=0):
        if n_past:
            carry = lax.fori_loop(0, n_past, lambda i, c: fn(past, i, i, c), carry)
        return lax.fori_loop(0, nkb_present, lambda i, c: fn(present, i, n_past + i, c), carry)

    def rows_of(i):
        return pl.ds(pl.multiple_of(i * kb, kb), kb)

    def keys_values(src, bi, i):
        kblk = src[1][bi, rows_of(i), :]
        if src is present:
            return [(kblk[:, g * PAD_HEAD:g * PAD_HEAD + HEAD_DIM], src[2][bi, g, i])
                    for g in range(N_KV_HEADS)]
        vt = src[2][bi, rows_of(i), :].T
        ones = jnp.ones((VX_ROWS - HEAD_DIM, kb), F32)
        out = []
        for g in range(N_KV_HEADS):
            kg = kblk if g == 0 else pltpu.roll(kblk, LANES - g * HEAD_DIM, axis=1)
            vg = jnp.concatenate([vt[g * HEAD_DIM:(g + 1) * HEAD_DIM], ones], axis=0)
            out.append((kg[:, :HEAD_DIM].astype(BF16), vg.astype(BF16)))
        return out

    def search_path():
        qi_all = [jnp.concatenate([qi_ref[bi, h] for h in range(N_IDX_HEADS)], axis=1)
                  for bi in range(nb)]
        wts = [[wt_ref[bi, h:h + 1, :] * (N_IDX_HEADS ** -0.5) for h in range(N_IDX_HEADS)]
               for bi in range(nb)]

        def score_block(src, i, blk, c):
            for bi in range(nb):
                s = jnp.dot(src[0][bi, rows_of(i), :].astype(BF16), qi_all[bi],
                            preferred_element_type=F32)
                acc = None
                for h in range(N_IDX_HEADS):
                    term = wts[bi][h] * jnp.maximum(s[:, h * tq:(h + 1) * tq], 0.0)
                    acc = term if acc is None else acc + term
                if src is present:
                    acc = jnp.where(admissible(i), acc, -jnp.inf)
                sc_ref[bi, blk] = acc
            return c

        for_blocks(score_block)

        for lo, hi, nkb in variants:
            if len(variants) == 1:
                _search_thresholds(sc_ref, thr_ref, need_ref, kf, nkb)
            else:
                pl.when((j >= lo) & (j < hi))(
                    functools.partial(_search_thresholds, sc_ref, thr_ref, need_ref, kf, nkb))

        tri = (lax.broadcasted_iota(jnp.int32, (kb, kb), 1)
               <= lax.broadcasted_iota(jnp.int32, (kb, kb), 0)).astype(BF16)

        def bias_block(src, i, blk, seen):
            out = []
            for bi in range(nb):
                x = sc_ref[bi, blk]
                thr = thr_ref[bi, 0:1, :]
                eq = x == thr
                e = jnp.where(eq, 1.0, 0.0).astype(BF16)
                inblock = jnp.dot(tri, e, preferred_element_type=F32)
                tie = jnp.where(inblock + seen[bi] <= need_ref[bi, 0:1, :], 0.0, -jnp.inf)
                sc_ref[bi, blk] = jnp.where(eq, tie, jnp.where(x > thr, 0.0, -jnp.inf))
                out.append(seen[bi] + inblock[kb - 1:kb, :])
            return tuple(out)

        for_blocks(bias_block, tuple(jnp.zeros((1, tq), F32) for _ in range(nb)))

    def all_admissible_path():
        def bias_block(src, i, blk, c):
            for bi in range(nb):
                if src is present:
                    sc_ref[bi, blk] = jnp.where(admissible(i), 0.0, -jnp.inf)
                else:
                    sc_ref[bi, blk] = jnp.zeros((kb, tq), F32)
            return c

        for_blocks(bias_block)

    if search_from == 0:
        search_path()
    else:
        pl.when(j >= search_from)(search_path)
        pl.when(j < search_from)(all_admissible_path)

    q_all = [[jnp.concatenate([q_ref[bi, g * rep + h] for h in range(rep)], axis=1)
              for g in range(N_KV_HEADS)] for bi in range(nb)]

    macc_ref[...] = jnp.full(macc_ref.shape, jnp.finfo(F32).min, F32)
    oacc_ref[...] = jnp.zeros(oacc_ref.shape, F32)

    def pv_block(src, i, blk, c):
        units = [(bi, g, kg, vxg) for bi in range(nb)
                 for g, (kg, vxg) in enumerate(keys_values(src, bi, i))]
        biases = [jnp.concatenate([sc_ref[bi, blk]] * rep, axis=1) for bi in range(nb)]
        logits = [jnp.dot(kg, q_all[bi][g], preferred_element_type=F32) + biases[bi]
                  for bi, g, kg, _ in units]
        probs, alphas = [], []
        for (bi, g, _, _), lg in zip(units, logits):
            m_old = macc_ref[bi, g, 0:1, :]
            blk_max = jnp.max(jnp.max(lg.reshape(-1, 8, lg.shape[1]), axis=0), axis=0,
                              keepdims=True)
            m_new = jnp.maximum(m_old, blk_max)
            probs.append(jnp.exp2(lg - m_new).astype(BF16))
            alphas.append(jnp.exp2(m_old - m_new))
            macc_ref[bi, g] = jnp.broadcast_to(m_new, macc_ref.shape[2:])
        for (bi, g, _, vxg), p, alpha in zip(units, probs, alphas):
            oacc_ref[bi, g] = (oacc_ref[bi, g] * alpha
                               + jnp.dot(vxg, p, preferred_element_type=F32))
        return c

    for_blocks(pv_block)

    t_out = ga_ref.shape[1]
    for bi in range(nb):
        outs = []
        for g in range(N_KV_HEADS):
            acc = oacc_ref[bi, g]
            o = acc[:HEAD_DIM] / acc[HEAD_DIM:HEAD_DIM + 1]
            for h in range(0, rep, 2):
                pair = jnp.concatenate([o[:, h * tq:(h + 1) * tq], o[:, (h + 1) * tq:(h + 2) * tq]],
                                       axis=0)
                outs.append(pair.T)
        o_all = jnp.concatenate(outs, axis=1)[:t_out]
        o_ref[bi] = (o_all * ga_ref[bi].astype(F32)).astype(o_ref.dtype)


def _attend_variants(nq, tq, q_pos0, past_len, present_len, n_sel):
    kb = KEY_BLOCK // 2
    variants, search_from = [], nq
    for j in range(nq):
        n_present = min(((q_pos0 + (j + 1) * tq - 1) // CHUNK + 1) * CHUNK - past_len, present_len)
        if past_len + n_present <= n_sel:
            assert not variants
            continue
        search_from = min(search_from, j)
        nkb = past_len // kb + -(-n_present // kb)
        if variants and variants[-1][2] == nkb:
            variants[-1] = (variants[-1][0], j + 1, nkb)
        else:
            variants.append((j, j + 1, nkb))
    return search_from, tuple(variants)


def _attend(q, qi, wt, ga, present, past, q_pos0, past_len, present_len):
    b, _, _, t_pad = q.shape
    t = ga.shape[1]
    kb = KEY_BLOCK
    tq = LANES
    nq = t_pad // tq
    tr = t // nq
    nb = max(1, ATTEND_ROWS // tq)
    l_present = present[0].shape[1]
    assert b % nb == 0 and past_len % kb == 0 and l_present % kb == 0 and q_pos0 >= past_len
    assert t % nq == 0 and (nq == 1 or tr == tq)
    n_sel = min(TOPK_MAX, (past_len + present_len) // 4)
    search_from, variants = _attend_variants(nq, tq, q_pos0, past_len, present_len, n_sel)
    nkb_max = past_len // kb + l_present // kb
    kern = functools.partial(_attend_kernel, has_past=past is not None, variants=variants,
                             search_from=search_from, q_pos0=q_pos0, past_len=past_len,
                             present_len=present_len, n_sel=n_sel)

    def kspec(a):
        nd = a.ndim
        return pl.BlockSpec((nb,) + a.shape[1:], lambda i, j: (i,) + (0,) * (nd - 1),
                            pipeline_mode=pl.Buffered(1))

    keys = (tuple(past) if past is not None else ()) + tuple(present)
    return pl.pallas_call(
        kern,
        grid=(b // nb, nq),
        in_specs=[pl.BlockSpec((nb, N_HEADS, HEAD_DIM, tq), lambda i, j: (i, 0, 0, j)),
                  pl.BlockSpec((nb, N_IDX_HEADS, IDX_DIM, tq), lambda i, j: (i, 0, 0, j)),
                  pl.BlockSpec((nb, N_IDX_HEADS, tq), lambda i, j: (i, 0, j)),
                  pl.BlockSpec((nb, tr, ATTN_WIDTH), lambda i, j: (i, j, 0))]
        + [kspec(a) for a in keys],
        out_specs=pl.BlockSpec((nb, tr, ATTN_WIDTH), lambda i, j: (i, j, 0)),
        out_shape=jax.ShapeDtypeStruct((b, t, ATTN_WIDTH), BF16),
        scratch_shapes=[pltpu.VMEM((nb, nkb_max, kb, tq), F32),
                        pltpu.VMEM((nb, 8, tq), F32), pltpu.VMEM((nb, 8, tq), F32),
                        pltpu.VMEM((nb, N_KV_HEADS, 8, N_HEADS // N_KV_HEADS * tq), F32),
                        pltpu.VMEM((nb, N_KV_HEADS, VX_ROWS, N_HEADS // N_KV_HEADS * tq), F32)],
        compiler_params=_cparams(("parallel", "parallel")),
        name="attend",
    )(q, qi, wt, ga, *keys)


SSM_SLAB = LANES


def _ssm_kernel(u_ref, h0r_ref, h0i_ref, a_ref, bbd_ref, cbd_ref, d_ref,
                y_ref, hr_ref, hi_ref, s_ref, st_ref, perm_ref):
    nb, tt, w = u_ref.shape
    rows = tt * nb
    n = SSM_LANES

    @pl.when(pl.program_id(0) == 0)
    def _():
        st_ref[0] = h0r_ref[...]
        st_ref[1] = h0i_ref[...]
        r_out = lax.broadcasted_iota(jnp.int32, (rows, rows), 0)
        r_in = lax.broadcasted_iota(jnp.int32, (rows, rows), 1)
        perm_ref[0] = (r_in == (r_out % nb) * tt + r_out // nb).astype(BF16)
        perm_ref[1] = (r_in == (r_out % tt) * nb + r_out // tt).astype(BF16)

    u2 = jnp.dot(perm_ref[0], u_ref[...].reshape(rows, w),
                 preferred_element_type=F32).astype(BF16)
    lanes_per_slab = SSM_SLAB // SSM_GROUP * SSM_STATE
    ys = []
    for q in range(w // SSM_SLAB):
        ch = slice(q * SSM_SLAB, (q + 1) * SSM_SLAB)
        re = slice(q * lanes_per_slab, (q + 1) * lanes_per_slab)
        im = slice(n + q * lanes_per_slab, n + (q + 1) * lanes_per_slab)
        s_ref[:, re] = jnp.dot(u2[:, ch], bbd_ref[ch, re], preferred_element_type=F32)
        s_ref[:, im] = jnp.dot(u2[:, ch], bbd_ref[ch, im], preferred_element_type=F32)
        ar = jnp.broadcast_to(a_ref[0:1, re], (nb, lanes_per_slab))
        ai = jnp.broadcast_to(a_ref[1:2, re], (nb, lanes_per_slab))
        hr, hi = st_ref[0, :, re], st_ref[1, :, re]
        for t in range(tt):
            rws = slice(t * nb, (t + 1) * nb)
            hr, hi = (ar * hr - ai * hi + s_ref[rws, re], ar * hi + ai * hr + s_ref[rws, im])
            s_ref[rws, re] = hr
            s_ref[rws, im] = hi
        st_ref[0, :, re] = hr
        st_ref[1, :, re] = hi
        nt = (((1,), (1,)), ((), ()))
        ys.append(lax.dot_general(s_ref[:, re].astype(BF16), cbd_ref[ch, re], nt,
                                  preferred_element_type=F32)
                  + lax.dot_general(s_ref[:, im].astype(BF16), cbd_ref[ch, im], nt,
                                    preferred_element_type=F32))
    y = jnp.concatenate(ys, axis=1)
    y = y + d_ref[...] * u2.astype(F32)
    back = jnp.dot(perm_ref[1], jax.nn.gelu(y).astype(BF16), preferred_element_type=F32)
    y_ref[...] = back.astype(y_ref.dtype).reshape(nb, tt, w)
    hr_ref[...] = st_ref[0]
    hi_ref[...] = st_ref[1]


def _ssm(u, h0r, h0i, a2, bbd, cbd, d_skip):
    nb, t, w = u.shape
    tt = max(1, 512 // nb)
    assert t % tt == 0
    n = SSM_LANES
    return pl.pallas_call(
        _ssm_kernel,
        grid=(t // tt,),
        in_specs=[pl.BlockSpec((nb, tt, w), lambda i: (0, i, 0)),
                  _const_spec((nb, n)), _const_spec((nb, n)), _const_spec((2, n)),
                  _const_spec((w, 2 * n)), _const_spec((w, 2 * n)), _const_spec((1, w))],
        out_specs=[pl.BlockSpec((nb, tt, w), lambda i: (0, i, 0)),
                   _const_spec((nb, n)), _const_spec((nb, n))],
        out_shape=[jax.ShapeDtypeStruct((nb, t, w), BF16),
                   jax.ShapeDtypeStruct((nb, n), F32), jax.ShapeDtypeStruct((nb, n), F32)],
        scratch_shapes=[pltpu.VMEM((tt * nb, 2 * n), F32), pltpu.VMEM((2, nb, n), F32),
                        pltpu.VMEM((2, tt * nb, tt * nb), BF16)],
        compiler_params=_cparams(("arbitrary",)),
        name="ssm",
    )(u, h0r, h0i, a2, bbd, cbd, d_skip.reshape(1, w))


def _outproj_kernel(x_ref, mod_ref, a_ref, yg_ref, gs_ref, ma_ref, mb_ref,
                    wa_ref, wg_ref, ws_ref, wo_ref, gf_ref, y_ref):
    bb, tt, d = x_ref.shape
    rows = bb * tt

    def flat(ref):
        return ref[...].reshape(rows, ref.shape[-1])

    branch_a = jnp.dot(flat(a_ref), wa_ref[...], preferred_element_type=F32)
    g_lin = jnp.dot(flat(yg_ref), wg_ref[...], preferred_element_type=F32)
    y_glu = g_lin[:, :SSM_WIDTH] * jax.nn.sigmoid(g_lin[:, SSM_WIDTH:])
    gated = y_glu * flat(gs_ref).astype(F32)
    branch_b = jnp.dot(gated.astype(BF16), ws_ref[...], preferred_element_type=F32)
    merged = flat(ma_ref).astype(F32) * branch_a + flat(mb_ref).astype(F32) * branch_b
    proj = jnp.dot(merged.astype(BF16), wo_ref[...], preferred_element_type=F32)
    gate = mod_ref[:, :, 2 * d:3 * d]
    xo = x_ref[...] + gate * proj.reshape(bb, tt, d)
    y = xo * lax.rsqrt(jnp.mean(xo * xo, axis=-1, keepdims=True) + NORM_EPS) * gf_ref[...]
    y_ref[...] = y


def _outproj(x, mod, a, yg, gs, ma, mb, wa, wg, ws, wo, g_final):
    b, t, d = x.shape
    bb, tt = _row_blocking(b, t)

    def tok(width):
        return pl.BlockSpec((bb, tt, width), lambda i, j: (i, j, 0))

    return pl.pallas_call(
        _outproj_kernel,
        grid=(b // bb, t // tt),
        in_specs=[tok(d), pl.BlockSpec((bb, 1, 3 * d), lambda i, j: (i, 0, 0)),
                  tok(ATTN_WIDTH), tok(SSM_WIDTH), tok(SSM_WIDTH), tok(d), tok(d),
                  _const_spec(wa.shape), _const_spec(wg.shape), _const_spec(ws.shape),
                  _const_spec(wo.shape), _const_spec((1, 1, d))],
        out_specs=tok(d),
        out_shape=jax.ShapeDtypeStruct((b, t, d), F32),
        compiler_params=_cparams(("parallel", "parallel")),
        name="outproj",
    )(x, mod.reshape(b, 1, 3 * d), a, yg, gs, ma, mb, wa, wg, ws, wo, g_final.reshape(1, 1, d))


def _pad_rows(a, rows):
    return jnp.pad(a, ((0, 0), (0, rows - a.shape[1]), (0, 0)))


def _layer(x, mod, pos0, past, prm):
    (g_norm, w_packed, a2, bbd, cbd, d_skip, wg, wa, ws, wo, g_final) = prm
    b, t, _ = x.shape
    pos = pos0 + jnp.arange(t, dtype=jnp.int32)
    q, k, v, qi, ki, wt, kp, vx, kip, ga, u, gs, ma, mb = _inproj(x, mod, g_norm, w_packed, pos)

    t_pad = -(-t // KEY_BLOCK) * KEY_BLOCK
    present = ((_pad_rows(kip, t_pad), _pad_rows(kp, t_pad)) if t_pad != t else (kip, kp)) + (vx,)
    if past is None:
        past_keys, past_len = None, 0
        h0r = jnp.zeros((b, SSM_LANES), F32)
        h0i = jnp.zeros((b, SSM_LANES), F32)
    else:
        ck, cv, cki, h0r, h0i = past
        past_len = ck.shape[1]
        past_keys = (cki, ck.reshape(b, past_len, KV_WIDTH), cv.reshape(b, past_len, KV_WIDTH))
        h0r = h0r.reshape(b, SSM_LANES)
        h0i = h0i.reshape(b, SSM_LANES)

    a = _attend(q, qi, wt, ga, present, past_keys, pos0, past_len, t)

    yg, hr, hi = _ssm(u, h0r, h0i, a2, bbd, cbd, d_skip)

    y = _outproj(x, mod, a, yg, gs, ma, mb, wa, wg, ws, wo, g_final)
    return (y, k.reshape(b, t, N_KV_HEADS, HEAD_DIM), v.reshape(b, t, N_KV_HEADS, HEAD_DIM), ki,
            hr.reshape(b, N_SSM_GROUPS, SSM_STATE), hi.reshape(b, N_SSM_GROUPS, SSM_STATE))


def kernel(x_prompt, x_sample, cache_k, cache_v, cache_idx_k, state_ssm_re, state_ssm_im,
           c_prompt, c_sample, w_mod, b_mod, g_norm, w_in, lambda_re, lambda_im, log_dt,
           ssm_b_re, ssm_b_im, ssm_c_re, ssm_c_im, d_skip, w_glu, w_attn_proj, w_ssm_proj,
           w_out, g_final):
    depth = w_in.shape[0]
    assert depth == 1, "the final norm is fused into the (single) layer's output kernel"
    nbp = x_prompt.shape[0]
    past_len = cache_k.shape[2]

    def layer0(a):
        return a.reshape(a.shape[1:])

    mod = _modulation(jnp.concatenate([c_prompt, c_sample], axis=0), layer0(w_mod), layer0(b_mod))
    mod_p, mod_s = mod[:nbp], mod[nbp:]

    cut = _SEG_OFF["kiwi"][0] + IDX_DIM + N_IDX_HEADS
    w = layer0(w_in)
    col = lax.broadcasted_iota(jnp.int32, (1, IN_PACKED), 1)
    w_packed = jnp.where(col < cut, jnp.pad(w, ((0, 0), (0, _KIWI_PAD))),
                         jnp.where(col < cut + _KIWI_PAD, 0.0,
                                   jnp.pad(w, ((0, 0), (_KIWI_PAD, 0))))).astype(BF16)

    a2, bbd, cbd = _discretize(*map(layer0, (lambda_re, lambda_im, log_dt, ssm_b_re, ssm_b_im,
                                             ssm_c_re, ssm_c_im)))

    prm = (layer0(g_norm), w_packed, a2, bbd, cbd, layer0(d_skip), layer0(w_glu).astype(BF16),
           layer0(w_attn_proj).astype(BF16), layer0(w_ssm_proj).astype(BF16),
           layer0(w_out).astype(BF16), g_final)

    yp, kp, vp, kip, hrp, hip = _layer(x_prompt, mod_p, 0, None, prm)
    past = tuple(map(layer0, (cache_k, cache_v, cache_idx_k, state_ssm_re, state_ssm_im)))
    ys, ks, vs, kis, hrs, his = _layer(x_sample, mod_s, past_len, past, prm)

    def st(z):
        return z[None]

    return (yp, ys, st(kp), st(vp), st(kip), st(hrp), st(hip),
            st(ks), st(vs), st(kis), st(hrs), st(his))
```

```python
import functools
import math

import jax
import jax.numpy as jnp
from jax import lax
from jax.experimental import pallas as pl
from jax.experimental.pallas import tpu as pltpu

F32 = jnp.float32
BF16 = jnp.bfloat16

LANES = 128
VMEM_LIMIT = 56 * 1024 * 1024

D_MODEL = 1024
CHUNK = 64
N_HEADS = 8
N_KV_HEADS = 2
HEAD_DIM = 64
ATTN_WIDTH = N_HEADS * HEAD_DIM
KV_WIDTH = N_KV_HEADS * HEAD_DIM
ROT_DIM = HEAD_DIM // 4
ROPE_THETA = 500000.0
N_IDX_HEADS = 8
IDX_DIM = 64
IDX_WIDTH = N_IDX_HEADS * IDX_DIM
TOPK_MAX = 256
SSM_WIDTH = D_MODEL // 2
SSM_GROUP = 16
N_SSM_GROUPS = SSM_WIDTH // SSM_GROUP
SSM_STATE = 64
SSM_LANES = N_SSM_GROUPS * SSM_STATE
NORM_EPS = 1e-6

_KIWI_PAD = LANES - IDX_DIM - N_IDX_HEADS
_SEG_WIDTHS = (ATTN_WIDTH, 2 * KV_WIDTH, IDX_WIDTH, LANES, ATTN_WIDTH, SSM_WIDTH, SSM_WIDTH,
               D_MODEL, D_MODEL)
_SEG_NAMES = ("q", "kv", "qi", "kiwi", "ga", "u", "gs", "ma", "mb")
_SEG_OFF = {}
_off = 0
for _n, _w in zip(_SEG_NAMES, _SEG_WIDTHS):
    _SEG_OFF[_n] = (_off, _off + _w)
    _off += _w
IN_PACKED = _off
_WI_COL = IDX_DIM

ROW_TILE = 512
PAD_HEAD = 2 * HEAD_DIM
KEY_BLOCK = 256
VX_ROWS = HEAD_DIM + 16
INT_MIN = -2 ** 31


def _cparams(sem):
    return pltpu.CompilerParams(dimension_semantics=sem, vmem_limit_bytes=VMEM_LIMIT)


def _const_spec(shape):
    nd = len(shape)
    return pl.BlockSpec(shape, lambda *_: (0,) * nd)


def _mod_kernel(c_ref, w_ref, b_ref, o_ref):
    s = jax.nn.silu(c_ref[...])
    o_ref[...] = (jnp.dot(s.astype(BF16), w_ref[...].astype(BF16), preferred_element_type=F32)
                  + b_ref[...])


def _modulation(c, w_mod, b_mod):
    n = c.shape[0]
    return pl.pallas_call(
        _mod_kernel,
        out_shape=jax.ShapeDtypeStruct((n, 3 * D_MODEL), F32),
        compiler_params=pltpu.CompilerParams(vmem_limit_bytes=VMEM_LIMIT),
        name="mod",
    )(c, w_mod, b_mod.reshape(1, 3 * D_MODEL))


def _disc_kernel(lre_ref, lim_ref, ldt_ref, bre_ref, bim_ref, cre_ref, cim_ref,
                 a_ref, bbd_ref, cbd_ref):
    n = lre_ref.shape[1]
    dt = jnp.exp(ldt_ref[...])
    lr, li = lre_ref[...], lim_ref[...]
    mag = jnp.exp(lr * dt)
    ar, ai = mag * jnp.cos(li * dt), mag * jnp.sin(li * dt)
    den = lr * lr + li * li
    zr = ((ar - 1.0) * lr + ai * li) / den
    zi = (ai * lr - (ar - 1.0) * li) / den
    a_ref[0:1, :] = ar
    a_ref[1:2, :] = ai

    eye = (lax.broadcasted_iota(jnp.int32, (SSM_GROUP, SSM_GROUP), 0)
           == lax.broadcasted_iota(jnp.int32, (SSM_GROUP, SSM_GROUP), 1)).astype(BF16)

    def transposed(x):
        out, rest = None, x
        for _ in range(3):
            limb = rest.astype(BF16)
            rest = rest - limb.astype(F32)
            t = lax.dot_general(eye, limb, (((1,), (1,)), ((), ())), preferred_element_type=F32)
            out = t if out is None else out + t
        return out

    br, bi = transposed(bre_ref[...]), transposed(bim_ref[...])
    bbar_re = zr * br - zi * bi
    bbar_im = zr * bi + zi * br
    group_of_lane = lax.broadcasted_iota(jnp.int32, (SSM_GROUP, n), 1) // SSM_STATE
    for g in range(n // SSM_STATE):
        rows = slice(g * SSM_GROUP, (g + 1) * SSM_GROUP)
        mine = group_of_lane == g
        bbd_ref[rows, 0:n] = jnp.where(mine, bbar_re, 0.0).astype(bbd_ref.dtype)
        bbd_ref[rows, n:2 * n] = jnp.where(mine, bbar_im, 0.0).astype(bbd_ref.dtype)
    rows_n = cre_ref.shape[0]
    diag = (lax.broadcasted_iota(jnp.int32, (rows_n, n), 0) // SSM_GROUP
            == lax.broadcasted_iota(jnp.int32, (rows_n, n), 1) // SSM_STATE)
    cbd_ref[:, 0:n] = jnp.where(diag, cre_ref[...], 0.0).astype(cbd_ref.dtype)
    cbd_ref[:, n:2 * n] = jnp.where(diag, -cim_ref[...], 0.0).astype(cbd_ref.dtype)


def _discretize(lambda_re, lambda_im, log_dt, b_re, b_im, c_re, c_im):
    g, p, c = b_re.shape
    n = g * p

    def tiled(z):
        return jnp.tile(z.reshape(g * c, p), (1, g))

    return pl.pallas_call(
        _disc_kernel,
        out_shape=(jax.ShapeDtypeStruct((2, n), F32), jax.ShapeDtypeStruct((g * c, 2 * n), BF16),
                   jax.ShapeDtypeStruct((g * c, 2 * n), BF16)),
        compiler_params=pltpu.CompilerParams(vmem_limit_bytes=VMEM_LIMIT),
        name="disc",
    )(lambda_re.reshape(1, n), lambda_im.reshape(1, n), jnp.repeat(log_dt, p).reshape(1, n),
      b_re.reshape(n, c), b_im.reshape(n, c), tiled(c_re), tiled(c_im))


def _rope_block(z, cos, sa, sb):
    up = pltpu.roll(z, LANES - ROT_DIM // 2, axis=1)
    dn = pltpu.roll(z, ROT_DIM // 2, axis=1)
    return z * cos + up * sa + dn * sb


def _split_heads(z, fill):
    low = lax.broadcasted_iota(jnp.int32, z.shape, 1) < HEAD_DIM
    return jnp.concatenate([jnp.where(low, z, fill),
                            jnp.where(low, pltpu.roll(z, HEAD_DIM, axis=1), fill)], axis=1)


def _inproj_kernel(x_ref, mod_ref, g_ref, w_ref, rope2_ref, rope1_ref,
                   q_ref, k_ref, v_ref, qi_ref, ki_ref, wt_ref, kp_ref, vx_ref, kip_ref,
                   ga_ref, u_ref, gs_ref, ma_ref, mb_ref):
    bb, tt, d = x_ref.shape
    rows = bb * tt
    x = x_ref[...]
    y = x * lax.rsqrt(jnp.mean(x * x, axis=-1, keepdims=True) + NORM_EPS) * g_ref[...]
    shift = mod_ref[:, :, 0:d]
    scale = mod_ref[:, :, d:2 * d]
    h = (y * (1.0 + scale) + shift).reshape(rows, d).astype(BF16)

    def seg(name):
        a, b = _SEG_OFF[name]
        return jnp.dot(h, w_ref[:, a:b], preferred_element_type=F32)

    def roped(z, tab_ref):
        cos, sa, sb = tab_ref[0], tab_ref[1], tab_ref[2]
        blocks = [_rope_block(z[:, i:i + LANES], cos, sa, sb) for i in range(0, z.shape[1], LANES)]
        return blocks[0] if len(blocks) == 1 else jnp.concatenate(blocks, axis=1)

    def put(ref, val):
        ref[...] = val.reshape(ref.shape).astype(ref.dtype)

    def split_all(z, fill=0.0):
        return jnp.concatenate([_split_heads(z[:, i:i + LANES], fill)
                                for i in range(0, z.shape[1], LANES)], axis=1)

    def token_lanes(z, b, lanes):
        blk = z[b * tt:(b + 1) * tt]
        if lanes > tt:
            blk = jnp.concatenate([blk, jnp.zeros((lanes - tt, LANES), F32)], axis=0)
        return blk.T

    def put_heads_t(ref, z):
        for b in range(bb):
            for pair in range(ref.shape[1] // 2):
                t = token_lanes(z[:, pair * LANES:(pair + 1) * LANES], b, ref.shape[3])
                ref[b, 2 * pair] = t[:HEAD_DIM].astype(ref.dtype)
                ref[b, 2 * pair + 1] = t[HEAD_DIM:].astype(ref.dtype)

    put_heads_t(q_ref, roped(seg("q"), rope2_ref) * (HEAD_DIM ** -0.5 * math.log2(math.e)))
    kv = seg("kv")
    k = roped(kv[:, :KV_WIDTH], rope2_ref)
    put(k_ref, k)
    put(kp_ref, split_all(k))
    v = kv[:, KV_WIDTH:]
    put(v_ref, v)
    n_kb = vx_ref.shape[2]
    ones = jnp.ones((VX_ROWS - HEAD_DIM, KEY_BLOCK), F32)
    for b in range(bb):
        vt = token_lanes(v, b, n_kb * KEY_BLOCK)
        for g in range(N_KV_HEADS):
            for kb_i in range(n_kb):
                blk = vt[g * HEAD_DIM:(g + 1) * HEAD_DIM, kb_i * KEY_BLOCK:(kb_i + 1) * KEY_BLOCK]
                vx_ref[b, g, kb_i] = jnp.concatenate([blk, ones], axis=0).astype(vx_ref.dtype)
    put_heads_t(qi_ref, roped(seg("qi"), rope2_ref) * (IDX_DIM ** -0.5))
    kiwi = roped(seg("kiwi"), rope1_ref)
    for b in range(bb):
        wt_ref[b] = token_lanes(kiwi, b, wt_ref.shape[2])[_WI_COL:_WI_COL + N_IDX_HEADS]
    put(ki_ref, kiwi[:, 0:IDX_DIM])
    put(kip_ref, kiwi[:, 0:IDX_DIM])
    put(ga_ref, jax.nn.silu(seg("ga")))
    put(u_ref, seg("u"))
    put(gs_ref, jax.nn.silu(seg("gs")))
    put(ma_ref, jax.nn.sigmoid(seg("ma")))
    put(mb_ref, jax.nn.sigmoid(seg("mb")))


def _rope_tables(pos, heads_in_block):
    half = ROT_DIM // 2
    inv = jnp.power(ROPE_THETA, -jnp.arange(half, dtype=F32) * (2.0 / ROT_DIM))
    ang = pos.astype(F32)[:, None] * inv[None, :]
    cos, sin = jnp.cos(ang), jnp.sin(ang)
    t = pos.shape[0]
    ones = jnp.ones((t, HEAD_DIM - ROT_DIM), F32)
    zeros = jnp.zeros((t, HEAD_DIM - ROT_DIM), F32)
    zh = jnp.zeros((t, half), F32)
    c_head = jnp.concatenate([cos, cos, ones], axis=1)
    sa_head = jnp.concatenate([-sin, zh, zeros], axis=1)
    sb_head = jnp.concatenate([zh, sin, zeros], axis=1)
    n_id = LANES // HEAD_DIM - heads_in_block
    ident = [jnp.ones((t, HEAD_DIM), F32)] * n_id
    zero = [jnp.zeros((t, HEAD_DIM), F32)] * n_id
    c = jnp.concatenate([c_head] * heads_in_block + ident, axis=1)
    sa = jnp.concatenate([sa_head] * heads_in_block + zero, axis=1)
    sb = jnp.concatenate([sb_head] * heads_in_block + zero, axis=1)
    return jnp.stack([c, sa, sb])


def _row_blocking(b, t):
    tt = min(t, ROW_TILE)
    bb = ROW_TILE // tt
    assert t % tt == 0 and b % bb == 0 and tt % 16 == 0
    return bb, tt


def _inproj(x, mod, g_norm, w_packed, pos):
    b, t, d = x.shape
    bb, tt = _row_blocking(b, t)
    rows = bb * tt
    rope2 = jnp.tile(_rope_tables(pos, 2), (1, bb, 1)) if bb > 1 else _rope_tables(pos, 2)
    rope1 = jnp.tile(_rope_tables(pos, 1), (1, bb, 1)) if bb > 1 else _rope_tables(pos, 1)

    def tok(width, dtype):
        return (jax.ShapeDtypeStruct((b, t, width), dtype),
                pl.BlockSpec((bb, tt, width), lambda i, j: (i, j, 0)))

    tq = max(tt, LANES)
    nkb = -(-tt // KEY_BLOCK)

    def heads_t(n):
        return (jax.ShapeDtypeStruct((b, n, HEAD_DIM, t // tt * tq), BF16),
                pl.BlockSpec((bb, n, HEAD_DIM, tq), lambda i, j: (i, 0, 0, j)))

    w_t = (jax.ShapeDtypeStruct((b, N_IDX_HEADS, t // tt * tq), F32),
           pl.BlockSpec((bb, N_IDX_HEADS, tq), lambda i, j: (i, 0, j)))
    vx_t = (jax.ShapeDtypeStruct((b, N_KV_HEADS, t // tt * nkb, VX_ROWS, KEY_BLOCK), BF16),
            pl.BlockSpec((bb, N_KV_HEADS, nkb, VX_ROWS, KEY_BLOCK), lambda i, j: (i, 0, j, 0, 0)))

    outs = [heads_t(N_HEADS), tok(KV_WIDTH, F32), tok(KV_WIDTH, F32),
            heads_t(N_IDX_HEADS), tok(IDX_DIM, F32), w_t,
            tok(2 * KV_WIDTH, BF16), vx_t, tok(IDX_DIM, BF16),
            tok(ATTN_WIDTH, BF16), tok(SSM_WIDTH, BF16),
            tok(SSM_WIDTH, BF16), tok(D_MODEL, BF16), tok(D_MODEL, BF16)]
    return pl.pallas_call(
        _inproj_kernel,
        grid=(b // bb, t // tt),
        in_specs=[
            pl.BlockSpec((bb, tt, d), lambda i, j: (i, j, 0)),
            pl.BlockSpec((bb, 1, 3 * d), lambda i, j: (i, 0, 0)),
            _const_spec((1, 1, d)),
            pl.BlockSpec((d, IN_PACKED), lambda i, j: (0, 0), pipeline_mode=pl.Buffered(1)),
            pl.BlockSpec((3, rows, LANES), lambda i, j: (0, j, 0)),
            pl.BlockSpec((3, rows, LANES), lambda i, j: (0, j, 0)),
        ],
        out_specs=[o[1] for o in outs],
        out_shape=[o[0] for o in outs],
        compiler_params=_cparams(("parallel", "parallel")),
        name="inproj",
    )(x, mod.reshape(b, 1, 3 * d), g_norm.reshape(1, 1, d), w_packed, rope2, rope1)


ATTEND_ROWS = 1024


def _float_of_key(key):
    bits = jnp.where(key >= 0, key, key ^ jnp.int32(0x7FFFFFFF))
    return lax.bitcast_convert_type(bits, F32)


def _count(sc_ref, bi, n_half, pred):
    half = sc_ref.shape[2] // 2
    acc = None
    for blk in range(-(-n_half // 2)):
        x = sc_ref[bi, blk] if 2 * blk + 1 < n_half else sc_ref[bi, blk, 0:half, :]
        hit = jnp.where(pred(x), 1.0, 0.0)
        part = jnp.sum(hit.reshape(-1, 8, hit.shape[1]), axis=0)
        acc = part if acc is None else acc + part
    return jnp.sum(acc, axis=0, keepdims=True)


def _search_thresholds(sc_ref, thr_ref, need_ref, kf, nkb):
    nb, _, _, tq = sc_ref.shape

    def body(i, keys):
        bit = jnp.left_shift(jnp.int32(1), 31 - i)
        out = []
        for bi, key in enumerate(keys):
            cand = key ^ bit
            cf = _float_of_key(cand)
            cnt = _count(sc_ref, bi, nkb, lambda x, cf=cf: x >= cf)
            out.append(jnp.where(cnt >= kf, cand, key))
        return tuple(out)

    init = tuple(jnp.full((1, tq), INT_MIN, jnp.int32) for _ in range(nb))
    keys = lax.fori_loop(0, 32, body, init)
    for bi, key in enumerate(keys):
        thr = _float_of_key(key)
        need = kf - _count(sc_ref, bi, nkb, lambda x, thr=thr: x > thr)
        thr_ref[bi] = jnp.broadcast_to(thr, thr_ref.shape[1:])
        need_ref[bi] = jnp.broadcast_to(need, need_ref.shape[1:])


def _attend_kernel(*refs, has_past, variants, search_from, q_pos0, past_len, present_len, n_sel):
    q_ref, qi_ref, wt_ref, ga_ref = refs[:4]
    n_in = 10 if has_past else 7
    past = refs[4:7] if has_past else None
    present = refs[n_in - 3:n_in]
    o_ref = refs[n_in]
    sc_ref, thr_ref, need_ref, macc_ref, oacc_ref = refs[n_in + 1:]
    nb, tq = q_ref.shape[0], q_ref.shape[3]
    kb = KEY_BLOCK
    n_past = past_len // kb
    rep = N_HEADS // N_KV_HEADS
    j = pl.program_id(1)

    qpos = q_pos0 + j * tq + lax.broadcasted_iota(jnp.int32, (1, tq), 1)
    qchunk = qpos // CHUNK
    n_present = jnp.minimum((qchunk + 1) * CHUNK - past_len, present_len)
    kf = jnp.minimum(past_len + n_present, n_sel).astype(F32)
    last_chunk = (q_pos0 + (j + 1) * tq - 1) // CHUNK
    nkb_present = (jnp.minimum((last_chunk + 1) * CHUNK - past_len, present_len) + kb - 1) // kb

    def admissible(i):
        local = i * kb + lax.broadcasted_iota(jnp.int32, (kb, 1), 0)
        return (((past_len + local) // CHUNK) <= qchunk) & (local < present_len)

    def for_blocks(fn, carry=0):
        if n_past:
            carry = lax.fori_loop(0, n_past, lambda i, c: fn(past, i, i, c), carry)
        return lax.fori_loop(0, nkb_present, lambda i, c: fn(present, i, n_past + i, c), carry)

    def rows_of(i):
        return pl.ds(pl.multiple_of(i * kb, kb), kb)

    def keys_values(src, bi, i):
        kblk = src[1][bi, rows_of(i), :]
        if src is present:
            return [(kblk[:, g * PAD_HEAD:g * PAD_HEAD + HEAD_DIM], src[2][bi, g, i])
                    for g in range(N_KV_HEADS)]
        vt = src[2][bi, rows_of(i), :].T
        ones = jnp.ones((VX_ROWS - HEAD_DIM, kb), F32)
        out = []
        for g in range(N_KV_HEADS):
            kg = kblk if g == 0 else pltpu.roll(kblk, LANES - g * HEAD_DIM, axis=1)
            vg = jnp.concatenate([vt[g * HEAD_DIM:(g + 1) * HEAD_DIM], ones], axis=0)
            out.append((kg[:, :HEAD_DIM].astype(BF16), vg.astype(BF16)))
        return out

    def search_path():
        qi_all = [jnp.concatenate([qi_ref[bi, h] for h in range(N_IDX_HEADS)], axis=1)
                  for bi in range(nb)]
        wts = [[wt_ref[bi, h:h + 1, :] * (N_IDX_HEADS ** -0.5) for h in range(N_IDX_HEADS)]
               for bi in range(nb)]

        def score_block(src, i, blk, c):
            for bi in range(nb):
                s = jnp.dot(src[0][bi, rows_of(i), :].astype(BF16), qi_all[bi],
                            preferred_element_type=F32)
                acc = None
                for h in range(N_IDX_HEADS):
                    term = wts[bi][h] * jnp.maximum(s[:, h * tq:(h + 1) * tq], 0.0)
                    acc = term if acc is None else acc + term
                if src is present:
                    acc = jnp.where(admissible(i), acc, -jnp.inf)
                sc_ref[bi, blk] = acc
            return c

        for_blocks(score_block)

        for lo, hi, nkb in variants:
            if len(variants) == 1:
                _search_thresholds(sc_ref, thr_ref, need_ref, kf, nkb)
            else:
                pl.when((j >= lo) & (j < hi))(
                    functools.partial(_search_thresholds, sc_ref, thr_ref, need_ref, kf, nkb))

        tri = (lax.broadcasted_iota(jnp.int32, (kb, kb), 1)
               <= lax.broadcasted_iota(jnp.int32, (kb, kb), 0)).astype(BF16)

        def bias_block(src, i, blk, seen):
            out = []
            for bi in range(nb):
                x = sc_ref[bi, blk]
                thr = thr_ref[bi, 0:1, :]
                eq = x == thr
                e = jnp.where(eq, 1.0, 0.0).astype(BF16)
                inblock = jnp.dot(tri, e, preferred_element_type=F32)
                tie = jnp.where(inblock + seen[bi] <= need_ref[bi, 0:1, :], 0.0, -jnp.inf)
                sc_ref[bi, blk] = jnp.where(eq, tie, jnp.where(x > thr, 0.0, -jnp.inf))
                out.append(seen[bi] + inblock[kb - 1:kb, :])
            return tuple(out)

        for_blocks(bias_block, tuple(jnp.zeros((1, tq), F32) for _ in range(nb)))

    def all_admissible_path():
        def bias_block(src, i, blk, c):
            for bi in range(nb):
                if src is present:
                    sc_ref[bi, blk] = jnp.where(admissible(i), 0.0, -jnp.inf)
                else:
                    sc_ref[bi, blk] = jnp.zeros((kb, tq), F32)
            return c

        for_blocks(bias_block)

    if search_from == 0:
        search_path()
    else:
        pl.when(j >= search_from)(search_path)
        pl.when(j < search_from)(all_admissible_path)

    q_all = [[jnp.concatenate([q_ref[bi, g * rep + h] for h in range(rep)], axis=1)
              for g in range(N_KV_HEADS)] for bi in range(nb)]

    macc_ref[...] = jnp.full(macc_ref.shape, jnp.finfo(F32).min, F32)
    oacc_ref[...] = jnp.zeros(oacc_ref.shape, F32)

    def pv_block(src, i, blk, c):
        units = [(bi, g, kg, vxg) for bi in range(nb)
                 for g, (kg, vxg) in enumerate(keys_values(src, bi, i))]
        biases = [jnp.concatenate([sc_ref[bi, blk]] * rep, axis=1) for bi in range(nb)]
        logits = [jnp.dot(kg, q_all[bi][g], preferred_element_type=F32) + biases[bi]
                  for bi, g, kg, _ in units]
        probs, alphas = [], []
        for (bi, g, _, _), lg in zip(units, logits):
            m_old = macc_ref[bi, g, 0:1, :]
            blk_max = jnp.max(jnp.max(lg.reshape(-1, 8, lg.shape[1]), axis=0), axis=0,
                              keepdims=True)
            m_new = jnp.maximum(m_old, blk_max)
            probs.append(jnp.exp2(lg - m_new).astype(BF16))
            alphas.append(jnp.exp2(m_old - m_new))
            macc_ref[bi, g] = jnp.broadcast_to(m_new, macc_ref.shape[2:])
        for (bi, g, _, vxg), p, alpha in zip(units, probs, alphas):
            oacc_ref[bi, g] = (oacc_ref[bi, g] * alpha
                               + jnp.dot(vxg, p, preferred_element_type=F32))
        return c

    for_blocks(pv_block)

    t_out = ga_ref.shape[1]
    for bi in range(nb):
        outs = []
        for g in range(N_KV_HEADS):
            acc = oacc_ref[bi, g]
            o = acc[:HEAD_DIM] / acc[HEAD_DIM:HEAD_DIM + 1]
            for h in range(0, rep, 2):
                pair = jnp.concatenate([o[:, h * tq:(h + 1) * tq], o[:, (h + 1) * tq:(h + 2) * tq]],
                                       axis=0)
                outs.append(pair.T)
        o_all = jnp.concatenate(outs, axis=1)[:t_out]
        o_ref[bi] = (o_all * ga_ref[bi].astype(F32)).astype(o_ref.dtype)


def _attend_variants(nq, tq, q_pos0, past_len, present_len, n_sel):
    kb = KEY_BLOCK // 2
    variants, search_from = [], nq
    for j in range(nq):
        n_present = min(((q_pos0 + (j + 1) * tq - 1) // CHUNK + 1) * CHUNK - past_len, present_len)
        if past_len + n_present <= n_sel:
            assert not variants
            continue
        search_from = min(search_from, j)
        nkb = past_len // kb + -(-n_present // kb)
        if variants and variants[-1][2] == nkb:
            variants[-1] = (variants[-1][0], j + 1, nkb)
        else:
            variants.append((j, j + 1, nkb))
    return search_from, tuple(variants)


def _attend(q, qi, wt, ga, present, past, q_pos0, past_len, present_len):
    b, _, _, t_pad = q.shape
    t = ga.shape[1]
    kb = KEY_BLOCK
    tq = LANES
    nq = t_pad // tq
    tr = t // nq
    nb = max(1, ATTEND_ROWS // tq)
    l_present = present[0].shape[1]
    assert b % nb == 0 and past_len % kb == 0 and l_present % kb == 0 and q_pos0 >= past_len
    assert t % nq == 0 and (nq == 1 or tr == tq)
    n_sel = min(TOPK_MAX, (past_len + present_len) // 4)
    search_from, variants = _attend_variants(nq, tq, q_pos0, past_len, present_len, n_sel)
    nkb_max = past_len // kb + l_present // kb
    kern = functools.partial(_attend_kernel, has_past=past is not None, variants=variants,
                             search_from=search_from, q_pos0=q_pos0, past_len=past_len,
                             present_len=present_len, n_sel=n_sel)

    def kspec(a):
        nd = a.ndim
        return pl.BlockSpec((nb,) + a.shape[1:], lambda i, j: (i,) + (0,) * (nd - 1),
                            pipeline_mode=pl.Buffered(1))

    keys = (tuple(past) if past is not None else ()) + tuple(present)
    return pl.pallas_call(
        kern,
        grid=(b // nb, nq),
        in_specs=[pl.BlockSpec((nb, N_HEADS, HEAD_DIM, tq), lambda i, j: (i, 0, 0, j)),
                  pl.BlockSpec((nb, N_IDX_HEADS, IDX_DIM, tq), lambda i, j: (i, 0, 0, j)),
                  pl.BlockSpec((nb, N_IDX_HEADS, tq), lambda i, j: (i, 0, j)),
                  pl.BlockSpec((nb, tr, ATTN_WIDTH), lambda i, j: (i, j, 0))]
        + [kspec(a) for a in keys],
        out_specs=pl.BlockSpec((nb, tr, ATTN_WIDTH), lambda i, j: (i, j, 0)),
        out_shape=jax.ShapeDtypeStruct((b, t, ATTN_WIDTH), BF16),
        scratch_shapes=[pltpu.VMEM((nb, nkb_max, kb, tq), F32),
                        pltpu.VMEM((nb, 8, tq), F32), pltpu.VMEM((nb, 8, tq), F32),
                        pltpu.VMEM((nb, N_KV_HEADS, 8, N_HEADS // N_KV_HEADS * tq), F32),
                        pltpu.VMEM((nb, N_KV_HEADS, VX_ROWS, N_HEADS // N_KV_HEADS * tq), F32)],
        compiler_params=_cparams(("parallel", "parallel")),
        name="attend",
    )(q, qi, wt, ga, *keys)


SSM_SLAB = LANES


def _ssm_kernel(u_ref, h0r_ref, h0i_ref, a_ref, bbd_ref, cbd_ref, d_ref,
                y_ref, hr_ref, hi_ref, s_ref, st_ref, perm_ref):
    nb, tt, w = u_ref.shape
    rows = tt * nb
    n = SSM_LANES

    @pl.when(pl.program_id(0) == 0)
    def _():
        st_ref[0] = h0r_ref[...]
        st_ref[1] = h0i_ref[...]
        r_out = lax.broadcasted_iota(jnp.int32, (rows, rows), 0)
        r_in = lax.broadcasted_iota(jnp.int32, (rows, rows), 1)
        perm_ref[0] = (r_in == (r_out % nb) * tt + r_out // nb).astype(BF16)
        perm_ref[1] = (r_in == (r_out % tt) * nb + r_out // tt).astype(BF16)

    u2 = jnp.dot(perm_ref[0], u_ref[...].reshape(rows, w),
                 preferred_element_type=F32).astype(BF16)
    lanes_per_slab = SSM_SLAB // SSM_GROUP * SSM_STATE
    ys = []
    for q in range(w // SSM_SLAB):
        ch = slice(q * SSM_SLAB, (q + 1) * SSM_SLAB)
        re = slice(q * lanes_per_slab, (q + 1) * lanes_per_slab)
        im = slice(n + q * lanes_per_slab, n + (q + 1) * lanes_per_slab)
        s_ref[:, re] = jnp.dot(u2[:, ch], bbd_ref[ch, re], preferred_element_type=F32)
        s_ref[:, im] = jnp.dot(u2[:, ch], bbd_ref[ch, im], preferred_element_type=F32)
        ar = jnp.broadcast_to(a_ref[0:1, re], (nb, lanes_per_slab))
        ai = jnp.broadcast_to(a_ref[1:2, re], (nb, lanes_per_slab))
        hr, hi = st_ref[0, :, re], st_ref[1, :, re]
        for t in range(tt):
            rws = slice(t * nb, (t + 1) * nb)
            hr, hi = (ar * hr - ai * hi + s_ref[rws, re], ar * hi + ai * hr + s_ref[rws, im])
            s_ref[rws, re] = hr
            s_ref[rws, im] = hi
        st_ref[0, :, re] = hr
        st_ref[1, :, re] = hi
        nt = (((1,), (1,)), ((), ()))
        ys.append(lax.dot_general(s_ref[:, re].astype(BF16), cbd_ref[ch, re], nt,
                                  preferred_element_type=F32)
                  + lax.dot_general(s_ref[:, im].astype(BF16), cbd_ref[ch, im], nt,
                                    preferred_element_type=F32))
    y = jnp.concatenate(ys, axis=1)
    y = y + d_ref[...] * u2.astype(F32)
    back = jnp.dot(perm_ref[1], jax.nn.gelu(y).astype(BF16), preferred_element_type=F32)
    y_ref[...] = back.astype(y_ref.dtype).reshape(nb, tt, w)
    hr_ref[...] = st_ref[0]
    hi_ref[...] = st_ref[1]


def _ssm(u, h0r, h0i, a2, bbd, cbd, d_skip):
    nb, t, w = u.shape
    tt = max(1, 512 // nb)
    assert t % tt == 0
    n = SSM_LANES
    return pl.pallas_call(
        _ssm_kernel,
        grid=(t // tt,),
        in_specs=[pl.BlockSpec((nb, tt, w), lambda i: (0, i, 0)),
                  _const_spec((nb, n)), _const_spec((nb, n)), _const_spec((2, n)),
                  _const_spec((w, 2 * n)), _const_spec((w, 2 * n)), _const_spec((1, w))],
        out_specs=[pl.BlockSpec((nb, tt, w), lambda i: (0, i, 0)),
                   _const_spec((nb, n)), _const_spec((nb, n))],
        out_shape=[jax.ShapeDtypeStruct((nb, t, w), BF16),
                   jax.ShapeDtypeStruct((nb, n), F32), jax.ShapeDtypeStruct((nb, n), F32)],
        scratch_shapes=[pltpu.VMEM((tt * nb, 2 * n), F32), pltpu.VMEM((2, nb, n), F32),
                        pltpu.VMEM((2, tt * nb, tt * nb), BF16)],
        compiler_params=_cparams(("arbitrary",)),
        name="ssm",
    )(u, h0r, h0i, a2, bbd, cbd, d_skip.reshape(1, w))


OUT_SPLIT = 2


def _outproj_kernel(x_ref, mod_ref, a_ref, yg_ref, gs_ref, ma_ref, mb_ref,
                    wa_ref, wg_ref, ws_ref, wo_ref, gf_ref, y_ref):
    bb, tt, d = x_ref.shape
    for half in range(OUT_SPLIT):
        if bb >= OUT_SPLIT:
            sel = (slice(half * bb // OUT_SPLIT, (half + 1) * bb // OUT_SPLIT), slice(None))
        else:
            sel = (slice(None), slice(half * tt // OUT_SPLIT, (half + 1) * tt // OUT_SPLIT))
        x = x_ref[sel]
        hb, ht = x.shape[0], x.shape[1]

        def flat(ref):
            return ref[sel].reshape(hb * ht, ref.shape[-1])

        branch_a = jnp.dot(flat(a_ref), wa_ref[...], preferred_element_type=F32)
        g_lin = jnp.dot(flat(yg_ref), wg_ref[...], preferred_element_type=F32)
        y_glu = g_lin[:, :SSM_WIDTH] * jax.nn.sigmoid(g_lin[:, SSM_WIDTH:])
        gated = y_glu * flat(gs_ref).astype(F32)
        branch_b = jnp.dot(gated.astype(BF16), ws_ref[...], preferred_element_type=F32)
        merged = flat(ma_ref).astype(F32) * branch_a + flat(mb_ref).astype(F32) * branch_b
        proj = jnp.dot(merged.astype(BF16), wo_ref[...], preferred_element_type=F32)
        gate = mod_ref[sel[0], :, 2 * d:3 * d]
        xo = x + gate * proj.reshape(hb, ht, d)
        y = xo * lax.rsqrt(jnp.mean(xo * xo, axis=-1, keepdims=True) + NORM_EPS) * gf_ref[...]
        y_ref[sel] = y


def _outproj(x, mod, a, yg, gs, ma, mb, wa, wg, ws, wo, g_final):
    b, t, d = x.shape
    tt = min(t, OUT_SPLIT * ROW_TILE)
    bb = OUT_SPLIT * ROW_TILE // tt
    assert t % tt == 0 and b % bb == 0 and (bb % OUT_SPLIT == 0 or tt % (16 * OUT_SPLIT) == 0)

    def tok(width):
        return pl.BlockSpec((bb, tt, width), lambda i, j: (i, j, 0))

    return pl.pallas_call(
        _outproj_kernel,
        grid=(b // bb, t // tt),
        in_specs=[tok(d), pl.BlockSpec((bb, 1, 3 * d), lambda i, j: (i, 0, 0)),
                  tok(ATTN_WIDTH), tok(SSM_WIDTH), tok(SSM_WIDTH), tok(d), tok(d),
                  _const_spec(wa.shape), _const_spec(wg.shape), _const_spec(ws.shape),
                  _const_spec(wo.shape), _const_spec((1, 1, d))],
        out_specs=tok(d),
        out_shape=jax.ShapeDtypeStruct((b, t, d), F32),
        compiler_params=_cparams(("parallel", "parallel")),
        name="outproj",
    )(x, mod.reshape(b, 1, 3 * d), a, yg, gs, ma, mb, wa, wg, ws, wo, g_final.reshape(1, 1, d))


def _pad_rows(a, rows):
    return jnp.pad(a, ((0, 0), (0, rows - a.shape[1]), (0, 0)))


def _layer(x, mod, pos0, past, prm):
    (g_norm, w_packed, a2, bbd, cbd, d_skip, wg, wa, ws, wo, g_final) = prm
    b, t, _ = x.shape
    pos = pos0 + jnp.arange(t, dtype=jnp.int32)
    q, k, v, qi, ki, wt, kp, vx, kip, ga, u, gs, ma, mb = _inproj(x, mod, g_norm, w_packed, pos)

    t_pad = -(-t // KEY_BLOCK) * KEY_BLOCK
    present = ((_pad_rows(kip, t_pad), _pad_rows(kp, t_pad)) if t_pad != t else (kip, kp)) + (vx,)
    if past is None:
        past_keys, past_len = None, 0
        h0r = jnp.zeros((b, SSM_LANES), F32)
        h0i = jnp.zeros((b, SSM_LANES), F32)
    else:
        ck, cv, cki, h0r, h0i = past
        past_len = ck.shape[1]
        past_keys = (cki, ck.reshape(b, past_len, KV_WIDTH), cv.reshape(b, past_len, KV_WIDTH))
        h0r = h0r.reshape(b, SSM_LANES)
        h0i = h0i.reshape(b, SSM_LANES)

    a = _attend(q, qi, wt, ga, present, past_keys, pos0, past_len, t)

    yg, hr, hi = _ssm(u, h0r, h0i, a2, bbd, cbd, d_skip)

    y = _outproj(x, mod, a, yg, gs, ma, mb, wa, wg, ws, wo, g_final)
    return (y, k.reshape(b, t, N_KV_HEADS, HEAD_DIM), v.reshape(b, t, N_KV_HEADS, HEAD_DIM), ki,
            hr.reshape(b, N_SSM_GROUPS, SSM_STATE), hi.reshape(b, N_SSM_GROUPS, SSM_STATE))


def kernel(x_prompt, x_sample, cache_k, cache_v, cache_idx_k, state_ssm_re, state_ssm_im,
           c_prompt, c_sample, w_mod, b_mod, g_norm, w_in, lambda_re, lambda_im, log_dt,
           ssm_b_re, ssm_b_im, ssm_c_re, ssm_c_im, d_skip, w_glu, w_attn_proj, w_ssm_proj,
           w_out, g_final):
    depth = w_in.shape[0]
    assert depth == 1, "the final norm is fused into the (single) layer's output kernel"
    nbp = x_prompt.shape[0]
    past_len = cache_k.shape[2]

    def layer0(a):
        return a.reshape(a.shape[1:])

    mod = _modulation(jnp.concatenate([c_prompt, c_sample], axis=0), layer0(w_mod), layer0(b_mod))
    mod_p, mod_s = mod[:nbp], mod[nbp:]

    cut = _SEG_OFF["kiwi"][0] + IDX_DIM + N_IDX_HEADS
    w = layer0(w_in)
    col = lax.broadcasted_iota(jnp.int32, (1, IN_PACKED), 1)
    w_packed = jnp.where(col < cut, jnp.pad(w, ((0, 0), (0, _KIWI_PAD))),
                         jnp.where(col < cut + _KIWI_PAD, 0.0,
                                   jnp.pad(w, ((0, 0), (_KIWI_PAD, 0))))).astype(BF16)

    a2, bbd, cbd = _discretize(*map(layer0, (lambda_re, lambda_im, log_dt, ssm_b_re, ssm_b_im,
                                             ssm_c_re, ssm_c_im)))

    prm = (layer0(g_norm), w_packed, a2, bbd, cbd, layer0(d_skip), layer0(w_glu).astype(BF16),
           layer0(w_attn_proj).astype(BF16), layer0(w_ssm_proj).astype(BF16),
           layer0(w_out).astype(BF16), g_final)

    yp, kp, vp, kip, hrp, hip = _layer(x_prompt, mod_p, 0, None, prm)
    past = tuple(map(layer0, (cache_k, cache_v, cache_idx_k, state_ssm_re, state_ssm_im)))
    ys, ks, vs, kis, hrs, his = _layer(x_sample, mod_s, past_len, past, prm)

    def st(z):
        return z[None]

    return (yp, ys, st(kp), st(vp), st(kip), st(hrp), st(hip),
            st(ks), st(vs), st(kis), st(hrs), st(his))
```

```python
import functools
import math

import jax
import jax.numpy as jnp
from jax import lax
from jax.experimental import pallas as pl
from jax.experimental.pallas import tpu as pltpu

F32 = jnp.float32
BF16 = jnp.bfloat16

LANES = 128
VMEM_LIMIT = 56 * 1024 * 1024

D_MODEL = 1024
CHUNK = 64
N_HEADS = 8
N_KV_HEADS = 2
HEAD_DIM = 64
ATTN_WIDTH = N_HEADS * HEAD_DIM
KV_WIDTH = N_KV_HEADS * HEAD_DIM
ROT_DIM = HEAD_DIM // 4
ROPE_THETA = 500000.0
N_IDX_HEADS = 8
IDX_DIM = 64
IDX_WIDTH = N_IDX_HEADS * IDX_DIM
TOPK_MAX = 256
SSM_WIDTH = D_MODEL // 2
SSM_GROUP = 16
N_SSM_GROUPS = SSM_WIDTH // SSM_GROUP
SSM_STATE = 64
SSM_LANES = N_SSM_GROUPS * SSM_STATE
NORM_EPS = 1e-6

_KIWI_PAD = LANES - IDX_DIM - N_IDX_HEADS
_SEG_WIDTHS = (ATTN_WIDTH, 2 * KV_WIDTH, IDX_WIDTH, LANES, ATTN_WIDTH, SSM_WIDTH, SSM_WIDTH,
               D_MODEL, D_MODEL)
_SEG_NAMES = ("q", "kv", "qi", "kiwi", "ga", "u", "gs", "ma", "mb")
_SEG_OFF = {}
_off = 0
for _n, _w in zip(_SEG_NAMES, _SEG_WIDTHS):
    _SEG_OFF[_n] = (_off, _off + _w)
    _off += _w
IN_PACKED = _off
_WI_COL = IDX_DIM

ROW_TILE = 512
PAD_HEAD = 2 * HEAD_DIM
KEY_BLOCK = 256
VX_ROWS = HEAD_DIM + 16
INT_MIN = -2 ** 31


def _cparams(sem):
    return pltpu.CompilerParams(dimension_semantics=sem, vmem_limit_bytes=VMEM_LIMIT)


def _const_spec(shape):
    nd = len(shape)
    return pl.BlockSpec(shape, lambda *_: (0,) * nd)


def _mod_kernel(c_ref, w_ref, b_ref, o_ref):
    s = jax.nn.silu(c_ref[...])
    o_ref[...] = (jnp.dot(s.astype(BF16), w_ref[...].astype(BF16), preferred_element_type=F32)
                  + b_ref[...])


def _modulation(c, w_mod, b_mod):
    n = c.shape[0]
    return pl.pallas_call(
        _mod_kernel,
        out_shape=jax.ShapeDtypeStruct((n, 3 * D_MODEL), F32),
        compiler_params=pltpu.CompilerParams(vmem_limit_bytes=VMEM_LIMIT),
        name="mod",
    )(c, w_mod, b_mod.reshape(1, 3 * D_MODEL))


def _disc_kernel(lre_ref, lim_ref, ldt_ref, bre_ref, bim_ref, cre_ref, cim_ref,
                 a_ref, bbd_ref, cbd_ref):
    n = lre_ref.shape[1]
    dt = jnp.exp(ldt_ref[...])
    lr, li = lre_ref[...], lim_ref[...]
    mag = jnp.exp(lr * dt)
    ar, ai = mag * jnp.cos(li * dt), mag * jnp.sin(li * dt)
    den = lr * lr + li * li
    zr = ((ar - 1.0) * lr + ai * li) / den
    zi = (ai * lr - (ar - 1.0) * li) / den
    a_ref[0:1, :] = ar
    a_ref[1:2, :] = ai

    eye = (lax.broadcasted_iota(jnp.int32, (SSM_GROUP, SSM_GROUP), 0)
           == lax.broadcasted_iota(jnp.int32, (SSM_GROUP, SSM_GROUP), 1)).astype(BF16)

    def transposed(x):
        out, rest = None, x
        for _ in range(3):
            limb = rest.astype(BF16)
            rest = rest - limb.astype(F32)
            t = lax.dot_general(eye, limb, (((1,), (1,)), ((), ())), preferred_element_type=F32)
            out = t if out is None else out + t
        return out

    br, bi = transposed(bre_ref[...]), transposed(bim_ref[...])
    bbar_re = zr * br - zi * bi
    bbar_im = zr * bi + zi * br
    group_of_lane = lax.broadcasted_iota(jnp.int32, (SSM_GROUP, n), 1) // SSM_STATE
    for g in range(n // SSM_STATE):
        rows = slice(g * SSM_GROUP, (g + 1) * SSM_GROUP)
        mine = group_of_lane == g
        bbd_ref[rows, 0:n] = jnp.where(mine, bbar_re, 0.0).astype(bbd_ref.dtype)
        bbd_ref[rows, n:2 * n] = jnp.where(mine, bbar_im, 0.0).astype(bbd_ref.dtype)
    rows_n = cre_ref.shape[0]
    diag = (lax.broadcasted_iota(jnp.int32, (rows_n, n), 0) // SSM_GROUP
            == lax.broadcasted_iota(jnp.int32, (rows_n, n), 1) // SSM_STATE)
    cbd_ref[:, 0:n] = jnp.where(diag, cre_ref[...], 0.0).astype(cbd_ref.dtype)
    cbd_ref[:, n:2 * n] = jnp.where(diag, -cim_ref[...], 0.0).astype(cbd_ref.dtype)


def _discretize(lambda_re, lambda_im, log_dt, b_re, b_im, c_re, c_im):
    g, p, c = b_re.shape
    n = g * p

    def tiled(z):
        return jnp.tile(z.reshape(g * c, p), (1, g))

    return pl.pallas_call(
        _disc_kernel,
        out_shape=(jax.ShapeDtypeStruct((2, n), F32), jax.ShapeDtypeStruct((g * c, 2 * n), BF16),
                   jax.ShapeDtypeStruct((g * c, 2 * n), BF16)),
        compiler_params=pltpu.CompilerParams(vmem_limit_bytes=VMEM_LIMIT),
        name="disc",
    )(lambda_re.reshape(1, n), lambda_im.reshape(1, n), jnp.repeat(log_dt, p).reshape(1, n),
      b_re.reshape(n, c), b_im.reshape(n, c), tiled(c_re), tiled(c_im))


def _rope_block(z, cos, sa, sb):
    up = pltpu.roll(z, LANES - ROT_DIM // 2, axis=1)
    dn = pltpu.roll(z, ROT_DIM // 2, axis=1)
    return z * cos + up * sa + dn * sb


def _split_heads(z, fill):
    low = lax.broadcasted_iota(jnp.int32, z.shape, 1) < HEAD_DIM
    return jnp.concatenate([jnp.where(low, z, fill),
                            jnp.where(low, pltpu.roll(z, HEAD_DIM, axis=1), fill)], axis=1)


def _inproj_kernel(x_ref, mod_ref, g_ref, w_ref, rope2_ref, rope1_ref,
                   q_ref, k_ref, v_ref, qi_ref, ki_ref, wt_ref, kp_ref, vx_ref, kip_ref,
                   ga_ref, u_ref, gs_ref, ma_ref, mb_ref):
    bb, tt, d = x_ref.shape
    rows = bb * tt
    x = x_ref[...]
    y = x * lax.rsqrt(jnp.mean(x * x, axis=-1, keepdims=True) + NORM_EPS) * g_ref[...]
    shift = mod_ref[:, :, 0:d]
    scale = mod_ref[:, :, d:2 * d]
    h = (y * (1.0 + scale) + shift).reshape(rows, d).astype(BF16)

    def seg(name):
        a, b = _SEG_OFF[name]
        return jnp.dot(h, w_ref[:, a:b], preferred_element_type=F32)

    def roped(z, tab_ref):
        cos, sa, sb = tab_ref[0], tab_ref[1], tab_ref[2]
        blocks = [_rope_block(z[:, i:i + LANES], cos, sa, sb) for i in range(0, z.shape[1], LANES)]
        return blocks[0] if len(blocks) == 1 else jnp.concatenate(blocks, axis=1)

    def put(ref, val):
        ref[...] = val.reshape(ref.shape).astype(ref.dtype)

    def split_all(z, fill=0.0):
        return jnp.concatenate([_split_heads(z[:, i:i + LANES], fill)
                                for i in range(0, z.shape[1], LANES)], axis=1)

    def token_lanes(z, b, lanes):
        blk = z[b * tt:(b + 1) * tt]
        if lanes > tt:
            blk = jnp.concatenate([blk, jnp.zeros((lanes - tt, LANES), F32)], axis=0)
        return blk.T

    def put_heads_t(ref, z):
        for b in range(bb):
            for pair in range(ref.shape[1] // 2):
                t = token_lanes(z[:, pair * LANES:(pair + 1) * LANES], b, ref.shape[3])
                ref[b, 2 * pair] = t[:HEAD_DIM].astype(ref.dtype)
                ref[b, 2 * pair + 1] = t[HEAD_DIM:].astype(ref.dtype)

    put_heads_t(q_ref, roped(seg("q"), rope2_ref) * (HEAD_DIM ** -0.5 * math.log2(math.e)))
    kv = seg("kv")
    k = roped(kv[:, :KV_WIDTH], rope2_ref)
    put(k_ref, k)
    put(kp_ref, split_all(k))
    v = kv[:, KV_WIDTH:]
    put(v_ref, v)
    n_kb = vx_ref.shape[2]
    ones = jnp.ones((VX_ROWS - HEAD_DIM, KEY_BLOCK), F32)
    for b in range(bb):
        vt = token_lanes(v, b, n_kb * KEY_BLOCK)
        for g in range(N_KV_HEADS):
            for kb_i in range(n_kb):
                blk = vt[g * HEAD_DIM:(g + 1) * HEAD_DIM, kb_i * KEY_BLOCK:(kb_i + 1) * KEY_BLOCK]
                vx_ref[b, g, kb_i] = jnp.concatenate([blk, ones], axis=0).astype(vx_ref.dtype)
    put_heads_t(qi_ref, roped(seg("qi"), rope2_ref) * (IDX_DIM ** -0.5))
    kiwi = roped(seg("kiwi"), rope1_ref)
    for b in range(bb):
        wt_ref[b] = token_lanes(kiwi, b, wt_ref.shape[2])[_WI_COL:_WI_COL + N_IDX_HEADS]
    put(ki_ref, kiwi[:, 0:IDX_DIM])
    put(kip_ref, kiwi[:, 0:IDX_DIM])
    put(ga_ref, jax.nn.silu(seg("ga")))
    put(u_ref, seg("u"))
    put(gs_ref, jax.nn.silu(seg("gs")))
    put(ma_ref, jax.nn.sigmoid(seg("ma")))
    put(mb_ref, jax.nn.sigmoid(seg("mb")))


def _rope_tables(pos, heads_in_block):
    half = ROT_DIM // 2
    inv = jnp.power(ROPE_THETA, -jnp.arange(half, dtype=F32) * (2.0 / ROT_DIM))
    ang = pos.astype(F32)[:, None] * inv[None, :]
    cos, sin = jnp.cos(ang), jnp.sin(ang)
    t = pos.shape[0]
    ones = jnp.ones((t, HEAD_DIM - ROT_DIM), F32)
    zeros = jnp.zeros((t, HEAD_DIM - ROT_DIM), F32)
    zh = jnp.zeros((t, half), F32)
    c_head = jnp.concatenate([cos, cos, ones], axis=1)
    sa_head = jnp.concatenate([-sin, zh, zeros], axis=1)
    sb_head = jnp.concatenate([zh, sin, zeros], axis=1)
    n_id = LANES // HEAD_DIM - heads_in_block
    ident = [jnp.ones((t, HEAD_DIM), F32)] * n_id
    zero = [jnp.zeros((t, HEAD_DIM), F32)] * n_id
    c = jnp.concatenate([c_head] * heads_in_block + ident, axis=1)
    sa = jnp.concatenate([sa_head] * heads_in_block + zero, axis=1)
    sb = jnp.concatenate([sb_head] * heads_in_block + zero, axis=1)
    return jnp.stack([c, sa, sb])


def _row_blocking(b, t):
    tt = min(t, ROW_TILE)
    bb = ROW_TILE // tt
    assert t % tt == 0 and b % bb == 0 and tt % 16 == 0
    return bb, tt


def _inproj(x, mod, g_norm, w_packed, pos):
    b, t, d = x.shape
    bb, tt = _row_blocking(b, t)
    rows = bb * tt
    rope2 = jnp.tile(_rope_tables(pos, 2), (1, bb, 1)) if bb > 1 else _rope_tables(pos, 2)
    rope1 = jnp.tile(_rope_tables(pos, 1), (1, bb, 1)) if bb > 1 else _rope_tables(pos, 1)

    def tok(width, dtype):
        return (jax.ShapeDtypeStruct((b, t, width), dtype),
                pl.BlockSpec((bb, tt, width), lambda i, j: (i, j, 0)))

    tq = max(tt, LANES)
    nkb = -(-tt // KEY_BLOCK)

    def heads_t(n):
        return (jax.ShapeDtypeStruct((b, n, HEAD_DIM, t // tt * tq), BF16),
                pl.BlockSpec((bb, n, HEAD_DIM, tq), lambda i, j: (i, 0, 0, j)))

    w_t = (jax.ShapeDtypeStruct((b, N_IDX_HEADS, t // tt * tq), F32),
           pl.BlockSpec((bb, N_IDX_HEADS, tq), lambda i, j: (i, 0, j)))
    vx_t = (jax.ShapeDtypeStruct((b, N_KV_HEADS, t // tt * nkb, VX_ROWS, KEY_BLOCK), BF16),
            pl.BlockSpec((bb, N_KV_HEADS, nkb, VX_ROWS, KEY_BLOCK), lambda i, j: (i, 0, j, 0, 0)))

    outs = [heads_t(N_HEADS), tok(KV_WIDTH, F32), tok(KV_WIDTH, F32),
            heads_t(N_IDX_HEADS), tok(IDX_DIM, F32), w_t,
            tok(2 * KV_WIDTH, BF16), vx_t, tok(IDX_DIM, BF16),
            tok(ATTN_WIDTH, BF16), tok(SSM_WIDTH, BF16),
            tok(SSM_WIDTH, BF16), tok(D_MODEL, BF16), tok(D_MODEL, BF16)]
    return pl.pallas_call(
        _inproj_kernel,
        grid=(b // bb, t // tt),
        in_specs=[
            pl.BlockSpec((bb, tt, d), lambda i, j: (i, j, 0)),
            pl.BlockSpec((bb, 1, 3 * d), lambda i, j: (i, 0, 0)),
            _const_spec((1, 1, d)),
            pl.BlockSpec((d, IN_PACKED), lambda i, j: (0, 0), pipeline_mode=pl.Buffered(1)),
            pl.BlockSpec((3, rows, LANES), lambda i, j: (0, j, 0)),
            pl.BlockSpec((3, rows, LANES), lambda i, j: (0, j, 0)),
        ],
        out_specs=[o[1] for o in outs],
        out_shape=[o[0] for o in outs],
        compiler_params=_cparams(("parallel", "parallel")),
        name="inproj",
    )(x, mod.reshape(b, 1, 3 * d), g_norm.reshape(1, 1, d), w_packed, rope2, rope1)


ATTEND_ROWS = 1024


def _float_of_key(key):
    bits = jnp.where(key >= 0, key, key ^ jnp.int32(0x7FFFFFFF))
    return lax.bitcast_convert_type(bits, F32)


def _count(sc_ref, bi, n_half, pred):
    half = sc_ref.shape[2] // 2
    acc = None
    for blk in range(-(-n_half // 2)):
        x = sc_ref[bi, blk] if 2 * blk + 1 < n_half else sc_ref[bi, blk, 0:half, :]
        hit = jnp.where(pred(x), 1.0, 0.0)
        part = jnp.sum(hit.reshape(-1, 8, hit.shape[1]), axis=0)
        acc = part if acc is None else acc + part
    return jnp.sum(acc, axis=0, keepdims=True)


def _search_thresholds(sc_ref, thr_ref, need_ref, kf, nkb):
    nb, _, _, tq = sc_ref.shape

    def body(i, keys):
        bit = jnp.left_shift(jnp.int32(1), 31 - i)
        out = []
        for bi, key in enumerate(keys):
            cand = key ^ bit
            cf = _float_of_key(cand)
            cnt = _count(sc_ref, bi, nkb, lambda x, cf=cf: x >= cf)
            out.append(jnp.where(cnt >= kf, cand, key))
        return tuple(out)

    init = tuple(jnp.full((1, tq), INT_MIN, jnp.int32) for _ in range(nb))
    keys = lax.fori_loop(0, 32, body, init)
    for bi, key in enumerate(keys):
        thr = _float_of_key(key)
        need = kf - _count(sc_ref, bi, nkb, lambda x, thr=thr: x > thr)
        thr_ref[bi] = jnp.broadcast_to(thr, thr_ref.shape[1:])
        need_ref[bi] = jnp.broadcast_to(need, need_ref.shape[1:])


def _attend_kernel(*refs, has_past, variants, search_from, q_pos0, past_len, present_len, n_sel):
    q_ref, qi_ref, wt_ref, ga_ref = refs[:4]
    n_in = 10 if has_past else 7
    past = refs[4:7] if has_past else None
    present = refs[n_in - 3:n_in]
    o_ref = refs[n_in]
    sc_ref, thr_ref, need_ref, macc_ref, oacc_ref = refs[n_in + 1:]
    nb, tq = q_ref.shape[0], q_ref.shape[3]
    kb = KEY_BLOCK
    n_past = past_len // kb
    rep = N_HEADS // N_KV_HEADS
    j = pl.program_id(1)

    qpos = q_pos0 + j * tq + lax.broadcasted_iota(jnp.int32, (1, tq), 1)
    qchunk = qpos // CHUNK
    n_present = jnp.minimum((qchunk + 1) * CHUNK - past_len, present_len)
    kf = jnp.minimum(past_len + n_present, n_sel).astype(F32)
    last_chunk = (q_pos0 + (j + 1) * tq - 1) // CHUNK
    half = kb // 2
    n_half_present = (jnp.minimum((last_chunk + 1) * CHUNK - past_len, present_len) + half - 1) // half
    nkb_present = n_half_present // 2
    has_tail = n_half_present % 2 == 1

    def admissible(i, n):
        local = i * kb + lax.broadcasted_iota(jnp.int32, (n, 1), 0)
        return (((past_len + local) // CHUNK) <= qchunk) & (local < present_len)

    def for_blocks(fn, carry=0):
        if n_past:
            carry = lax.fori_loop(0, n_past, lambda i, c: fn(past, i, i, c, kb), carry)
        carry = lax.fori_loop(0, nkb_present, lambda i, c: fn(present, i, n_past + i, c, kb), carry)

        @pl.when(has_tail)
        def _():
            fn(present, nkb_present, n_past + nkb_present, carry, half)

    def rows_of(i, n=kb):
        return pl.ds(pl.multiple_of(i * kb, half), n)

    def keys_values(src, bi, i, n):
        kblk = src[1][bi, rows_of(i, n), :]
        if src is present:
            return [(kblk[:, g * PAD_HEAD:g * PAD_HEAD + HEAD_DIM], src[2][bi, g, i, :, 0:n])
                    for g in range(N_KV_HEADS)]
        vt = src[2][bi, rows_of(i), :].T
        ones = jnp.ones((VX_ROWS - HEAD_DIM, kb), F32)
        out = []
        for g in range(N_KV_HEADS):
            kg = kblk if g == 0 else pltpu.roll(kblk, LANES - g * HEAD_DIM, axis=1)
            vg = jnp.concatenate([vt[g * HEAD_DIM:(g + 1) * HEAD_DIM], ones], axis=0)
            out.append((kg[:, :HEAD_DIM].astype(BF16), vg.astype(BF16)))
        return out

    def search_path():
        qi_all = [jnp.concatenate([qi_ref[bi, h] for h in range(N_IDX_HEADS)], axis=1)
                  for bi in range(nb)]
        wts = [[wt_ref[bi, h:h + 1, :] * (N_IDX_HEADS ** -0.5) for h in range(N_IDX_HEADS)]
               for bi in range(nb)]

        def score_block(src, i, blk, c, n):
            for bi in range(nb):
                s = jnp.dot(src[0][bi, rows_of(i, n), :].astype(BF16), qi_all[bi],
                            preferred_element_type=F32)
                acc = None
                for h in range(N_IDX_HEADS):
                    term = wts[bi][h] * jnp.maximum(s[:, h * tq:(h + 1) * tq], 0.0)
                    acc = term if acc is None else acc + term
                if src is present:
                    acc = jnp.where(admissible(i, n), acc, -jnp.inf)
                sc_ref[bi, blk, 0:n, :] = acc
            return c

        for_blocks(score_block)

        for lo, hi, nkb in variants:
            if len(variants) == 1:
                _search_thresholds(sc_ref, thr_ref, need_ref, kf, nkb)
            else:
                pl.when((j >= lo) & (j < hi))(
                    functools.partial(_search_thresholds, sc_ref, thr_ref, need_ref, kf, nkb))

        tri = (lax.broadcasted_iota(jnp.int32, (kb, kb), 1)
               <= lax.broadcasted_iota(jnp.int32, (kb, kb), 0)).astype(BF16)

        def bias_block(src, i, blk, seen, n):
            out = []
            for bi in range(nb):
                x = sc_ref[bi, blk, 0:n, :]
                thr = thr_ref[bi, 0:1, :]
                eq = x == thr
                e = jnp.where(eq, 1.0, 0.0).astype(BF16)
                inblock = jnp.dot(tri[0:n, 0:n], e, preferred_element_type=F32)
                tie = jnp.where(inblock + seen[bi] <= need_ref[bi, 0:1, :], 0.0, -jnp.inf)
                sc_ref[bi, blk, 0:n, :] = jnp.where(eq, tie, jnp.where(x > thr, 0.0, -jnp.inf))
                out.append(seen[bi] + inblock[n - 1:n, :])
            return tuple(out)

        for_blocks(bias_block, tuple(jnp.zeros((1, tq), F32) for _ in range(nb)))

    def all_admissible_path():
        def bias_block(src, i, blk, c, n):
            for bi in range(nb):
                if src is present:
                    sc_ref[bi, blk, 0:n, :] = jnp.where(admissible(i, n), 0.0, -jnp.inf)
                else:
                    sc_ref[bi, blk] = jnp.zeros((kb, tq), F32)
            return c

        for_blocks(bias_block)

    if search_from == 0:
        search_path()
    else:
        pl.when(j >= search_from)(search_path)
        pl.when(j < search_from)(all_admissible_path)

    q_all = [[jnp.concatenate([q_ref[bi, g * rep + h] for h in range(rep)], axis=1)
              for g in range(N_KV_HEADS)] for bi in range(nb)]

    macc_ref[...] = jnp.full(macc_ref.shape, jnp.finfo(F32).min, F32)
    oacc_ref[...] = jnp.zeros(oacc_ref.shape, F32)

    def pv_block(src, i, blk, c, n):
        units = [(bi, g, kg, vxg) for bi in range(nb)
                 for g, (kg, vxg) in enumerate(keys_values(src, bi, i, n))]
        biases = [jnp.concatenate([sc_ref[bi, blk, 0:n, :]] * rep, axis=1) for bi in range(nb)]
        logits = [jnp.dot(kg, q_all[bi][g], preferred_element_type=F32) + biases[bi]
                  for bi, g, kg, _ in units]
        probs, alphas = [], []
        for (bi, g, _, _), lg in zip(units, logits):
            m_old = macc_ref[bi, g, 0:1, :]
            blk_max = jnp.max(jnp.max(lg.reshape(-1, 8, lg.shape[1]), axis=0), axis=0,
                              keepdims=True)
            m_new = jnp.maximum(m_old, blk_max)
            probs.append(jnp.exp2(lg - m_new).astype(BF16))
            alphas.append(jnp.exp2(m_old - m_new))
            macc_ref[bi, g] = jnp.broadcast_to(m_new, macc_ref.shape[2:])
        for (bi, g, _, vxg), p, alpha in zip(units, probs, alphas):
            oacc_ref[bi, g] = (oacc_ref[bi, g] * alpha
                               + jnp.dot(vxg, p, preferred_element_type=F32))
        return c

    for_blocks(pv_block)

    t_out = ga_ref.shape[1]
    for bi in range(nb):
        outs = []
        for g in range(N_KV_HEADS):
            acc = oacc_ref[bi, g]
            o = acc[:HEAD_DIM] / acc[HEAD_DIM:HEAD_DIM + 1]
            for h in range(0, rep, 2):
                pair = jnp.concatenate([o[:, h * tq:(h + 1) * tq], o[:, (h + 1) * tq:(h + 2) * tq]],
                                       axis=0)
                outs.append(pair.T)
        o_all = jnp.concatenate(outs, axis=1)[:t_out]
        o_ref[bi] = (o_all * ga_ref[bi].astype(F32)).astype(o_ref.dtype)


def _attend_variants(nq, tq, q_pos0, past_len, present_len, n_sel):
    kb = KEY_BLOCK // 2
    variants, search_from = [], nq
    for j in range(nq):
        n_present = min(((q_pos0 + (j + 1) * tq - 1) // CHUNK + 1) * CHUNK - past_len, present_len)
        if past_len + n_present <= n_sel:
            assert not variants
            continue
        search_from = min(search_from, j)
        nkb = past_len // kb + -(-n_present // kb)
        if variants and variants[-1][2] == nkb:
            variants[-1] = (variants[-1][0], j + 1, nkb)
        else:
            variants.append((j, j + 1, nkb))
    return search_from, tuple(variants)


def _attend(q, qi, wt, ga, present, past, q_pos0, past_len, present_len):
    b, _, _, t_pad = q.shape
    t = ga.shape[1]
    kb = KEY_BLOCK
    tq = LANES
    nq = t_pad // tq
    tr = t // nq
    nb = max(1, ATTEND_ROWS // tq)
    l_present = present[0].shape[1]
    assert b % nb == 0 and past_len % kb == 0 and l_present % kb == 0 and q_pos0 >= past_len
    assert t % nq == 0 and (nq == 1 or tr == tq)
    n_sel = min(TOPK_MAX, (past_len + present_len) // 4)
    search_from, variants = _attend_variants(nq, tq, q_pos0, past_len, present_len, n_sel)
    nkb_max = past_len // kb + l_present // kb
    kern = functools.partial(_attend_kernel, has_past=past is not None, variants=variants,
                             search_from=search_from, q_pos0=q_pos0, past_len=past_len,
                             present_len=present_len, n_sel=n_sel)

    def kspec(a):
        nd = a.ndim
        return pl.BlockSpec((nb,) + a.shape[1:], lambda i, j: (i,) + (0,) * (nd - 1),
                            pipeline_mode=pl.Buffered(1))

    keys = (tuple(past) if past is not None else ()) + tuple(present)
    return pl.pallas_call(
        kern,
        grid=(b // nb, nq),
        in_specs=[pl.BlockSpec((nb, N_HEADS, HEAD_DIM, tq), lambda i, j: (i, 0, 0, j)),
                  pl.BlockSpec((nb, N_IDX_HEADS, IDX_DIM, tq), lambda i, j: (i, 0, 0, j)),
                  pl.BlockSpec((nb, N_IDX_HEADS, tq), lambda i, j: (i, 0, j)),
                  pl.BlockSpec((nb, tr, ATTN_WIDTH), lambda i, j: (i, j, 0))]
        + [kspec(a) for a in keys],
        out_specs=pl.BlockSpec((nb, tr, ATTN_WIDTH), lambda i, j: (i, j, 0)),
        out_shape=jax.ShapeDtypeStruct((b, t, ATTN_WIDTH), BF16),
        scratch_shapes=[pltpu.VMEM((nb, nkb_max, kb, tq), F32),
                        pltpu.VMEM((nb, 8, tq), F32), pltpu.VMEM((nb, 8, tq), F32),
                        pltpu.VMEM((nb, N_KV_HEADS, 8, N_HEADS // N_KV_HEADS * tq), F32),
                        pltpu.VMEM((nb, N_KV_HEADS, VX_ROWS, N_HEADS // N_KV_HEADS * tq), F32)],
        compiler_params=_cparams(("parallel", "parallel")),
        name="attend",
    )(q, qi, wt, ga, *keys)


SSM_SLAB = LANES


def _ssm_kernel(u_ref, h0r_ref, h0i_ref, a_ref, bbd_ref, cbd_ref, d_ref,
                y_ref, hr_ref, hi_ref, s_ref, st_ref, perm_ref):
    nb, tt, w = u_ref.shape
    rows = tt * nb
    n = SSM_LANES

    @pl.when(pl.program_id(0) == 0)
    def _():
        st_ref[0] = h0r_ref[...]
        st_ref[1] = h0i_ref[...]
        r_out = lax.broadcasted_iota(jnp.int32, (rows, rows), 0)
        r_in = lax.broadcasted_iota(jnp.int32, (rows, rows), 1)
        perm_ref[0] = (r_in == (r_out % nb) * tt + r_out // nb).astype(BF16)
        perm_ref[1] = (r_in == (r_out % tt) * nb + r_out // tt).astype(BF16)

    u2 = jnp.dot(perm_ref[0], u_ref[...].reshape(rows, w),
                 preferred_element_type=F32).astype(BF16)
    lanes_per_slab = SSM_SLAB // SSM_GROUP * SSM_STATE
    ys = []
    for q in range(w // SSM_SLAB):
        ch = slice(q * SSM_SLAB, (q + 1) * SSM_SLAB)
        re = slice(q * lanes_per_slab, (q + 1) * lanes_per_slab)
        im = slice(n + q * lanes_per_slab, n + (q + 1) * lanes_per_slab)
        s_ref[:, re] = jnp.dot(u2[:, ch], bbd_ref[ch, re], preferred_element_type=F32)
        s_ref[:, im] = jnp.dot(u2[:, ch], bbd_ref[ch, im], preferred_element_type=F32)
        ar = jnp.broadcast_to(a_ref[0:1, re], (nb, lanes_per_slab))
        ai = jnp.broadcast_to(a_ref[1:2, re], (nb, lanes_per_slab))
        hr, hi = st_ref[0, :, re], st_ref[1, :, re]
        for t in range(tt):
            rws = slice(t * nb, (t + 1) * nb)
            hr, hi = (ar * hr - ai * hi + s_ref[rws, re], ar * hi + ai * hr + s_ref[rws, im])
            s_ref[rws, re] = hr
            s_ref[rws, im] = hi
        st_ref[0, :, re] = hr
        st_ref[1, :, re] = hi
        nt = (((1,), (1,)), ((), ()))
        ys.append(lax.dot_general(s_ref[:, re].astype(BF16), cbd_ref[ch, re], nt,
                                  preferred_element_type=F32)
                  + lax.dot_general(s_ref[:, im].astype(BF16), cbd_ref[ch, im], nt,
                                    preferred_element_type=F32))
    y = jnp.concatenate(ys, axis=1)
    y = y + d_ref[...] * u2.astype(F32)
    back = jnp.dot(perm_ref[1], jax.nn.gelu(y).astype(BF16), preferred_element_type=F32)
    y_ref[...] = back.astype(y_ref.dtype).reshape(nb, tt, w)
    hr_ref[...] = st_ref[0]
    hi_ref[...] = st_ref[1]


def _ssm(u, h0r, h0i, a2, bbd, cbd, d_skip):
    nb, t, w = u.shape
    tt = max(1, 512 // nb)
    assert t % tt == 0
    n = SSM_LANES
    return pl.pallas_call(
        _ssm_kernel,
        grid=(t // tt,),
        in_specs=[pl.BlockSpec((nb, tt, w), lambda i: (0, i, 0)),
                  _const_spec((nb, n)), _const_spec((nb, n)), _const_spec((2, n)),
                  _const_spec((w, 2 * n)), _const_spec((w, 2 * n)), _const_spec((1, w))],
        out_specs=[pl.BlockSpec((nb, tt, w), lambda i: (0, i, 0)),
                   _const_spec((nb, n)), _const_spec((nb, n))],
        out_shape=[jax.ShapeDtypeStruct((nb, t, w), BF16),
                   jax.ShapeDtypeStruct((nb, n), F32), jax.ShapeDtypeStruct((nb, n), F32)],
        scratch_shapes=[pltpu.VMEM((tt * nb, 2 * n), F32), pltpu.VMEM((2, nb, n), F32),
                        pltpu.VMEM((2, tt * nb, tt * nb), BF16)],
        compiler_params=_cparams(("arbitrary",)),
        name="ssm",
    )(u, h0r, h0i, a2, bbd, cbd, d_skip.reshape(1, w))


OUT_SPLIT = 2


def _outproj_kernel(x_ref, mod_ref, a_ref, yg_ref, gs_ref, ma_ref, mb_ref,
                    wa_ref, wg_ref, ws_ref, wo_ref, gf_ref, y_ref):
    bb, tt, d = x_ref.shape
    for half in range(OUT_SPLIT):
        if bb >= OUT_SPLIT:
            sel = (slice(half * bb // OUT_SPLIT, (half + 1) * bb // OUT_SPLIT), slice(None))
        else:
            sel = (slice(None), slice(half * tt // OUT_SPLIT, (half + 1) * tt // OUT_SPLIT))
        x = x_ref[sel]
        hb, ht = x.shape[0], x.shape[1]

        def flat(ref):
            return ref[sel].reshape(hb * ht, ref.shape[-1])

        branch_a = jnp.dot(flat(a_ref), wa_ref[...], preferred_element_type=F32)
        g_lin = jnp.dot(flat(yg_ref), wg_ref[...], preferred_element_type=F32)
        y_glu = g_lin[:, :SSM_WIDTH] * jax.nn.sigmoid(g_lin[:, SSM_WIDTH:])
        gated = y_glu * flat(gs_ref).astype(F32)
        branch_b = jnp.dot(gated.astype(BF16), ws_ref[...], preferred_element_type=F32)
        merged = flat(ma_ref).astype(F32) * branch_a + flat(mb_ref).astype(F32) * branch_b
        proj = jnp.dot(merged.astype(BF16), wo_ref[...], preferred_element_type=F32)
        gate = mod_ref[sel[0], :, 2 * d:3 * d]
        xo = x + gate * proj.reshape(hb, ht, d)
        y = xo * lax.rsqrt(jnp.mean(xo * xo, axis=-1, keepdims=True) + NORM_EPS) * gf_ref[...]
        y_ref[sel] = y


def _outproj(x, mod, a, yg, gs, ma, mb, wa, wg, ws, wo, g_final):
    b, t, d = x.shape
    tt = min(t, OUT_SPLIT * ROW_TILE)
    bb = OUT_SPLIT * ROW_TILE // tt
    assert t % tt == 0 and b % bb == 0 and (bb % OUT_SPLIT == 0 or tt % (16 * OUT_SPLIT) == 0)

    def tok(width):
        return pl.BlockSpec((bb, tt, width), lambda i, j: (i, j, 0))

    return pl.pallas_call(
        _outproj_kernel,
        grid=(b // bb, t // tt),
        in_specs=[tok(d), pl.BlockSpec((bb, 1, 3 * d), lambda i, j: (i, 0, 0)),
                  tok(ATTN_WIDTH), tok(SSM_WIDTH), tok(SSM_WIDTH), tok(d), tok(d),
                  _const_spec(wa.shape), _const_spec(wg.shape), _const_spec(ws.shape),
                  _const_spec(wo.shape), _const_spec((1, 1, d))],
        out_specs=tok(d),
        out_shape=jax.ShapeDtypeStruct((b, t, d), F32),
        compiler_params=_cparams(("parallel", "parallel")),
        name="outproj",
    )(x, mod.reshape(b, 1, 3 * d), a, yg, gs, ma, mb, wa, wg, ws, wo, g_final.reshape(1, 1, d))


def _pad_rows(a, rows):
    return jnp.pad(a, ((0, 0), (0, rows - a.shape[1]), (0, 0)))


def _layer(x, mod, pos0, past, prm):
    (g_norm, w_packed, a2, bbd, cbd, d_skip, wg, wa, ws, wo, g_final) = prm
    b, t, _ = x.shape
    pos = pos0 + jnp.arange(t, dtype=jnp.int32)
    q, k, v, qi, ki, wt, kp, vx, kip, ga, u, gs, ma, mb = _inproj(x, mod, g_norm, w_packed, pos)

    t_pad = -(-t // KEY_BLOCK) * KEY_BLOCK
    present = ((_pad_rows(kip, t_pad), _pad_rows(kp, t_pad)) if t_pad != t else (kip, kp)) + (vx,)
    if past is None:
        past_keys, past_len = None, 0
        h0r = jnp.zeros((b, SSM_LANES), F32)
        h0i = jnp.zeros((b, SSM_LANES), F32)
    else:
        ck, cv, cki, h0r, h0i = past
        past_len = ck.shape[1]
        past_keys = (cki, ck.reshape(b, past_len, KV_WIDTH), cv.reshape(b, past_len, KV_WIDTH))
        h0r = h0r.reshape(b, SSM_LANES)
        h0i = h0i.reshape(b, SSM_LANES)

    a = _attend(q, qi, wt, ga, present, past_keys, pos0, past_len, t)

    yg, hr, hi = _ssm(u, h0r, h0i, a2, bbd, cbd, d_skip)

    y = _outproj(x, mod, a, yg, gs, ma, mb, wa, wg, ws, wo, g_final)
    return (y, k.reshape(b, t, N_KV_HEADS, HEAD_DIM), v.reshape(b, t, N_KV_HEADS, HEAD_DIM), ki,
            hr.reshape(b, N_SSM_GROUPS, SSM_STATE), hi.reshape(b, N_SSM_GROUPS, SSM_STATE))


def kernel(x_prompt, x_sample, cache_k, cache_v, cache_idx_k, state_ssm_re, state_ssm_im,
           c_prompt, c_sample, w_mod, b_mod, g_norm, w_in, lambda_re, lambda_im, log_dt,
           ssm_b_re, ssm_b_im, ssm_c_re, ssm_c_im, d_skip, w_glu, w_attn_proj, w_ssm_proj,
           w_out, g_final):
    depth = w_in.shape[0]
    assert depth == 1, "the final norm is fused into the (single) layer's output kernel"
    nbp = x_prompt.shape[0]
    past_len = cache_k.shape[2]

    def layer0(a):
        return a.reshape(a.shape[1:])

    mod = _modulation(jnp.concatenate([c_prompt, c_sample], axis=0), layer0(w_mod), layer0(b_mod))
    mod_p, mod_s = mod[:nbp], mod[nbp:]

    cut = _SEG_OFF["kiwi"][0] + IDX_DIM + N_IDX_HEADS
    w = layer0(w_in)
    col = lax.broadcasted_iota(jnp.int32, (1, IN_PACKED), 1)
    w_packed = jnp.where(col < cut, jnp.pad(w, ((0, 0), (0, _KIWI_PAD))),
                         jnp.where(col < cut + _KIWI_PAD, 0.0,
                                   jnp.pad(w, ((0, 0), (_KIWI_PAD, 0))))).astype(BF16)

    a2, bbd, cbd = _discretize(*map(layer0, (lambda_re, lambda_im, log_dt, ssm_b_re, ssm_b_im,
                                             ssm_c_re, ssm_c_im)))

    prm = (layer0(g_norm), w_packed, a2, bbd, cbd, layer0(d_skip), layer0(w_glu).astype(BF16),
           layer0(w_attn_proj).astype(BF16), layer0(w_ssm_proj).astype(BF16),
           layer0(w_out).astype(BF16), g_final)

    yp, kp, vp, kip, hrp, hip = _layer(x_prompt, mod_p, 0, None, prm)
    past = tuple(map(layer0, (cache_k, cache_v, cache_idx_k, state_ssm_re, state_ssm_im)))
    ys, ks, vs, kis, hrs, his = _layer(x_sample, mod_s, past_len, past, prm)

    def st(z):
        return z[None]

    return (yp, ys, st(kp), st(vp), st(kip), st(hrp), st(hip),
            st(ks), st(vs), st(kis), st(hrs), st(his))
```

```python
import functools
import math

import jax
import jax.numpy as jnp
from jax import lax
from jax.experimental import pallas as pl
from jax.experimental.pallas import tpu as pltpu

F32 = jnp.float32
BF16 = jnp.bfloat16

LANES = 128
VMEM_LIMIT = 56 * 1024 * 1024

D_MODEL = 1024
CHUNK = 64
N_HEADS = 8
N_KV_HEADS = 2
HEAD_DIM = 64
ATTN_WIDTH = N_HEADS * HEAD_DIM
KV_WIDTH = N_KV_HEADS * HEAD_DIM
ROT_DIM = HEAD_DIM // 4
ROPE_THETA = 500000.0
N_IDX_HEADS = 8
IDX_DIM = 64
IDX_WIDTH = N_IDX_HEADS * IDX_DIM
TOPK_MAX = 256
SSM_WIDTH = D_MODEL // 2
SSM_GROUP = 16
N_SSM_GROUPS = SSM_WIDTH // SSM_GROUP
SSM_STATE = 64
SSM_LANES = N_SSM_GROUPS * SSM_STATE
NORM_EPS = 1e-6

_KIWI_PAD = LANES - IDX_DIM - N_IDX_HEADS
_SEG_WIDTHS = (ATTN_WIDTH, 2 * KV_WIDTH, IDX_WIDTH, LANES, ATTN_WIDTH, SSM_WIDTH, SSM_WIDTH,
               D_MODEL, D_MODEL)
_SEG_NAMES = ("q", "kv", "qi", "kiwi", "ga", "u", "gs", "ma", "mb")
_SEG_OFF = {}
_off = 0
for _n, _w in zip(_SEG_NAMES, _SEG_WIDTHS):
    _SEG_OFF[_n] = (_off, _off + _w)
    _off += _w
IN_PACKED = _off
_WI_COL = IDX_DIM

ROW_TILE = 512
PAD_HEAD = 2 * HEAD_DIM
KEY_BLOCK = 256
VX_ROWS = HEAD_DIM + 16
INT_MIN = -2 ** 31


def _cparams(sem):
    return pltpu.CompilerParams(dimension_semantics=sem, vmem_limit_bytes=VMEM_LIMIT)


def _const_spec(shape):
    nd = len(shape)
    return pl.BlockSpec(shape, lambda *_: (0,) * nd)


def _mod_kernel(c_ref, w_ref, b_ref, o_ref):
    s = jax.nn.silu(c_ref[...])
    o_ref[...] = (jnp.dot(s.astype(BF16), w_ref[...].astype(BF16), preferred_element_type=F32)
                  + b_ref[...])


def _modulation(c, w_mod, b_mod):
    n = c.shape[0]
    return pl.pallas_call(
        _mod_kernel,
        out_shape=jax.ShapeDtypeStruct((n, 3 * D_MODEL), F32),
        compiler_params=pltpu.CompilerParams(vmem_limit_bytes=VMEM_LIMIT),
        name="mod",
    )(c, w_mod, b_mod.reshape(1, 3 * D_MODEL))


def _disc_kernel(lre_ref, lim_ref, ldt_ref, bre_ref, bim_ref, cre_ref, cim_ref,
                 a_ref, bbd_ref, cbd_ref):
    n = lre_ref.shape[1]
    dt = jnp.exp(ldt_ref[...])
    lr, li = lre_ref[...], lim_ref[...]
    mag = jnp.exp(lr * dt)
    ar, ai = mag * jnp.cos(li * dt), mag * jnp.sin(li * dt)
    den = lr * lr + li * li
    zr = ((ar - 1.0) * lr + ai * li) / den
    zi = (ai * lr - (ar - 1.0) * li) / den
    a_ref[0:1, :] = ar
    a_ref[1:2, :] = ai

    eye = (lax.broadcasted_iota(jnp.int32, (SSM_GROUP, SSM_GROUP), 0)
           == lax.broadcasted_iota(jnp.int32, (SSM_GROUP, SSM_GROUP), 1)).astype(BF16)

    def transposed(x):
        out, rest = None, x
        for _ in range(3):
            limb = rest.astype(BF16)
            rest = rest - limb.astype(F32)
            t = lax.dot_general(eye, limb, (((1,), (1,)), ((), ())), preferred_element_type=F32)
            out = t if out is None else out + t
        return out

    br, bi = transposed(bre_ref[...]), transposed(bim_ref[...])
    bbar_re = zr * br - zi * bi
    bbar_im = zr * bi + zi * br
    group_of_lane = lax.broadcasted_iota(jnp.int32, (SSM_GROUP, n), 1) // SSM_STATE
    for g in range(n // SSM_STATE):
        rows = slice(g * SSM_GROUP, (g + 1) * SSM_GROUP)
        mine = group_of_lane == g
        bbd_ref[rows, 0:n] = jnp.where(mine, bbar_re, 0.0).astype(bbd_ref.dtype)
        bbd_ref[rows, n:2 * n] = jnp.where(mine, bbar_im, 0.0).astype(bbd_ref.dtype)
    rows_n = cre_ref.shape[0]
    diag = (lax.broadcasted_iota(jnp.int32, (rows_n, n), 0) // SSM_GROUP
            == lax.broadcasted_iota(jnp.int32, (rows_n, n), 1) // SSM_STATE)
    cbd_ref[:, 0:n] = jnp.where(diag, cre_ref[...], 0.0).astype(cbd_ref.dtype)
    cbd_ref[:, n:2 * n] = jnp.where(diag, -cim_ref[...], 0.0).astype(cbd_ref.dtype)


def _discretize(lambda_re, lambda_im, log_dt, b_re, b_im, c_re, c_im):
    g, p, c = b_re.shape
    n = g * p

    def tiled(z):
        return jnp.tile(z.reshape(g * c, p), (1, g))

    return pl.pallas_call(
        _disc_kernel,
        out_shape=(jax.ShapeDtypeStruct((2, n), F32), jax.ShapeDtypeStruct((g * c, 2 * n), BF16),
                   jax.ShapeDtypeStruct((g * c, 2 * n), BF16)),
        compiler_params=pltpu.CompilerParams(vmem_limit_bytes=VMEM_LIMIT),
        name="disc",
    )(lambda_re.reshape(1, n), lambda_im.reshape(1, n), jnp.repeat(log_dt, p).reshape(1, n),
      b_re.reshape(n, c), b_im.reshape(n, c), tiled(c_re), tiled(c_im))


def _rope_block(z, cos, sa, sb):
    up = pltpu.roll(z, LANES - ROT_DIM // 2, axis=1)
    dn = pltpu.roll(z, ROT_DIM // 2, axis=1)
    return z * cos + up * sa + dn * sb


def _split_heads(z, fill):
    low = lax.broadcasted_iota(jnp.int32, z.shape, 1) < HEAD_DIM
    return jnp.concatenate([jnp.where(low, z, fill),
                            jnp.where(low, pltpu.roll(z, HEAD_DIM, axis=1), fill)], axis=1)


def _inproj_kernel(x_ref, mod_ref, g_ref, w_ref, rope2_ref, rope1_ref,
                   q_ref, k_ref, v_ref, qi_ref, ki_ref, wt_ref, kp_ref, vx_ref, kip_ref,
                   ga_ref, u_ref, gs_ref, ma_ref, mb_ref):
    bb, tt, d = x_ref.shape
    rows = bb * tt
    x = x_ref[...]
    y = x * lax.rsqrt(jnp.mean(x * x, axis=-1, keepdims=True) + NORM_EPS) * g_ref[...]
    shift = mod_ref[:, :, 0:d]
    scale = mod_ref[:, :, d:2 * d]
    h = (y * (1.0 + scale) + shift).reshape(rows, d).astype(BF16)

    def seg(name):
        a, b = _SEG_OFF[name]
        return jnp.dot(h, w_ref[:, a:b], preferred_element_type=F32)

    def roped(z, tab_ref):
        cos, sa, sb = tab_ref[0], tab_ref[1], tab_ref[2]
        blocks = [_rope_block(z[:, i:i + LANES], cos, sa, sb) for i in range(0, z.shape[1], LANES)]
        return blocks[0] if len(blocks) == 1 else jnp.concatenate(blocks, axis=1)

    def put(ref, val):
        ref[...] = val.reshape(ref.shape).astype(ref.dtype)

    def split_all(z, fill=0.0):
        return jnp.concatenate([_split_heads(z[:, i:i + LANES], fill)
                                for i in range(0, z.shape[1], LANES)], axis=1)

    def token_lanes(z, b, lanes):
        blk = z[b * tt:(b + 1) * tt]
        if lanes > tt:
            blk = jnp.concatenate([blk, jnp.zeros((lanes - tt, LANES), F32)], axis=0)
        return blk.T

    def query_lanes(z, p, lanes):
        return z[p * lanes:(p + 1) * lanes].T

    def put_heads_t(ref, z):
        for p in range(ref.shape[0]):
            for pair in range(ref.shape[1] // 2):
                t = query_lanes(z[:, pair * LANES:(pair + 1) * LANES], p, ref.shape[3])
                ref[p, 2 * pair] = t[:HEAD_DIM].astype(ref.dtype)
                ref[p, 2 * pair + 1] = t[HEAD_DIM:].astype(ref.dtype)

    put_heads_t(q_ref, roped(seg("q"), rope2_ref) * (HEAD_DIM ** -0.5 * math.log2(math.e)))
    kv = seg("kv")
    k = roped(kv[:, :KV_WIDTH], rope2_ref)
    put(k_ref, k)
    put(kp_ref, split_all(k))
    v = kv[:, KV_WIDTH:]
    put(v_ref, v)
    n_kb = vx_ref.shape[2]
    ones = jnp.ones((VX_ROWS - HEAD_DIM, KEY_BLOCK), F32)
    for b in range(bb):
        vt = token_lanes(v, b, n_kb * KEY_BLOCK)
        for g in range(N_KV_HEADS):
            for kb_i in range(n_kb):
                blk = vt[g * HEAD_DIM:(g + 1) * HEAD_DIM, kb_i * KEY_BLOCK:(kb_i + 1) * KEY_BLOCK]
                vx_ref[b, g, kb_i] = jnp.concatenate([blk, ones], axis=0).astype(vx_ref.dtype)
    put_heads_t(qi_ref, roped(seg("qi"), rope2_ref) * (IDX_DIM ** -0.5))
    kiwi = roped(seg("kiwi"), rope1_ref)
    for p in range(wt_ref.shape[0]):
        wt_ref[p] = query_lanes(kiwi, p, wt_ref.shape[2])[_WI_COL:_WI_COL + N_IDX_HEADS]
    put(ki_ref, kiwi[:, 0:IDX_DIM])
    put(kip_ref, kiwi[:, 0:IDX_DIM])
    put(ga_ref, jax.nn.silu(seg("ga")))
    put(u_ref, seg("u"))
    put(gs_ref, jax.nn.silu(seg("gs")))
    put(ma_ref, jax.nn.sigmoid(seg("ma")))
    put(mb_ref, jax.nn.sigmoid(seg("mb")))


def _rope_tables(pos, heads_in_block):
    half = ROT_DIM // 2
    inv = jnp.power(ROPE_THETA, -jnp.arange(half, dtype=F32) * (2.0 / ROT_DIM))
    ang = pos.astype(F32)[:, None] * inv[None, :]
    cos, sin = jnp.cos(ang), jnp.sin(ang)
    t = pos.shape[0]
    ones = jnp.ones((t, HEAD_DIM - ROT_DIM), F32)
    zeros = jnp.zeros((t, HEAD_DIM - ROT_DIM), F32)
    zh = jnp.zeros((t, half), F32)
    c_head = jnp.concatenate([cos, cos, ones], axis=1)
    sa_head = jnp.concatenate([-sin, zh, zeros], axis=1)
    sb_head = jnp.concatenate([zh, sin, zeros], axis=1)
    n_id = LANES // HEAD_DIM - heads_in_block
    ident = [jnp.ones((t, HEAD_DIM), F32)] * n_id
    zero = [jnp.zeros((t, HEAD_DIM), F32)] * n_id
    c = jnp.concatenate([c_head] * heads_in_block + ident, axis=1)
    sa = jnp.concatenate([sa_head] * heads_in_block + zero, axis=1)
    sb = jnp.concatenate([sb_head] * heads_in_block + zero, axis=1)
    return jnp.stack([c, sa, sb])


def _row_blocking(b, t):
    tt = min(t, ROW_TILE)
    bb = ROW_TILE // tt
    assert t % tt == 0 and b % bb == 0 and tt % 16 == 0
    return bb, tt


def _inproj(x, mod, g_norm, w_packed, pos):
    b, t, d = x.shape
    bb, tt = _row_blocking(b, t)
    rows = bb * tt
    rope2 = jnp.tile(_rope_tables(pos, 2), (1, bb, 1)) if bb > 1 else _rope_tables(pos, 2)
    rope1 = jnp.tile(_rope_tables(pos, 1), (1, bb, 1)) if bb > 1 else _rope_tables(pos, 1)

    def tok(width, dtype):
        return (jax.ShapeDtypeStruct((b, t, width), dtype),
                pl.BlockSpec((bb, tt, width), lambda i, j: (i, j, 0)))

    tq = max(tt, LANES)
    grp = tq // tt
    assert bb % grp == 0
    nkb = -(-tt // KEY_BLOCK)

    def heads_t(n):
        return (jax.ShapeDtypeStruct((b // grp, n, HEAD_DIM, t // tt * tt * grp), BF16),
                pl.BlockSpec((bb // grp, n, HEAD_DIM, tq), lambda i, j: (i, 0, 0, j)))

    w_t = (jax.ShapeDtypeStruct((b // grp, N_IDX_HEADS, t // tt * tt * grp), F32),
           pl.BlockSpec((bb // grp, N_IDX_HEADS, tq), lambda i, j: (i, 0, j)))
    vx_t = (jax.ShapeDtypeStruct((b, N_KV_HEADS, t // tt * nkb, VX_ROWS, KEY_BLOCK), BF16),
            pl.BlockSpec((bb, N_KV_HEADS, nkb, VX_ROWS, KEY_BLOCK), lambda i, j: (i, 0, j, 0, 0)))

    outs = [heads_t(N_HEADS), tok(KV_WIDTH, F32), tok(KV_WIDTH, F32),
            heads_t(N_IDX_HEADS), tok(IDX_DIM, F32), w_t,
            tok(2 * KV_WIDTH, BF16), vx_t, tok(IDX_DIM, BF16),
            tok(ATTN_WIDTH, BF16), tok(SSM_WIDTH, BF16),
            tok(SSM_WIDTH, BF16), tok(D_MODEL, BF16), tok(D_MODEL, BF16)]
    return pl.pallas_call(
        _inproj_kernel,
        grid=(b // bb, t // tt),
        in_specs=[
            pl.BlockSpec((bb, tt, d), lambda i, j: (i, j, 0)),
            pl.BlockSpec((bb, 1, 3 * d), lambda i, j: (i, 0, 0)),
            _const_spec((1, 1, d)),
            pl.BlockSpec((d, IN_PACKED), lambda i, j: (0, 0), pipeline_mode=pl.Buffered(1)),
            pl.BlockSpec((3, rows, LANES), lambda i, j: (0, j, 0)),
            pl.BlockSpec((3, rows, LANES), lambda i, j: (0, j, 0)),
        ],
        out_specs=[o[1] for o in outs],
        out_shape=[o[0] for o in outs],
        compiler_params=_cparams(("parallel", "parallel")),
        name="inproj",
    )(x, mod.reshape(b, 1, 3 * d), g_norm.reshape(1, 1, d), w_packed, rope2, rope1)


ATTEND_ROWS = 1024


def _float_of_key(key):
    bits = jnp.where(key >= 0, key, key ^ jnp.int32(0x7FFFFFFF))
    return lax.bitcast_convert_type(bits, F32)


def _count(sc_ref, bi, n_half, pred):
    half = sc_ref.shape[2] // 2
    acc = None
    for blk in range(-(-n_half // 2)):
        x = sc_ref[bi, blk] if 2 * blk + 1 < n_half else sc_ref[bi, blk, 0:half, :]
        hit = jnp.where(pred(x), 1.0, 0.0)
        part = jnp.sum(hit.reshape(-1, 8, hit.shape[1]), axis=0)
        acc = part if acc is None else acc + part
    return jnp.sum(acc, axis=0, keepdims=True)


def _search_thresholds(sc_ref, thr_ref, need_ref, kf, nkb):
    nb, _, _, tq = sc_ref.shape

    def body(i, keys):
        bit = jnp.left_shift(jnp.int32(1), 31 - i)
        out = []
        for bi, key in enumerate(keys):
            cand = key ^ bit
            cf = _float_of_key(cand)
            cnt = _count(sc_ref, bi, nkb, lambda x, cf=cf: x >= cf)
            out.append(jnp.where(cnt >= kf, cand, key))
        return tuple(out)

    init = tuple(jnp.full((1, tq), INT_MIN, jnp.int32) for _ in range(nb))
    keys = lax.fori_loop(0, 32, body, init)
    for bi, key in enumerate(keys):
        thr = _float_of_key(key)
        need = kf - _count(sc_ref, bi, nkb, lambda x, thr=thr: x > thr)
        thr_ref[bi] = jnp.broadcast_to(thr, thr_ref.shape[1:])
        need_ref[bi] = jnp.broadcast_to(need, need_ref.shape[1:])


def _attend_kernel(*refs, has_past, variants, search_from, q_pos0, past_len, present_len, n_sel,
                   sides):
    q_ref, qi_ref, wt_ref, ga_ref = refs[:4]
    n_in = 10 if has_past else 7
    past = refs[4:7] if has_past else None
    present = refs[n_in - 3:n_in]
    o_ref = refs[n_in]
    sc_ref, thr_ref, need_ref, macc_ref, oacc_ref = refs[n_in + 1:]
    nb, tq = q_ref.shape[0], q_ref.shape[3]
    kb = KEY_BLOCK
    n_past = past_len // kb
    rep = N_HEADS // N_KV_HEADS
    j = pl.program_id(1)

    lane = lax.broadcasted_iota(jnp.int32, (1, tq), 1)
    qpos = q_pos0 + j * tq + lane % (tq // sides)
    side_of_lane = lane // (tq // sides)
    qchunk = qpos // CHUNK

    def by_side(parts, width):
        out = parts[0]
        own = jnp.concatenate([side_of_lane] * width, axis=1)
        for sd in range(1, sides):
            out = jnp.where(own == sd, parts[sd], out)
        return out
    n_present = jnp.minimum((qchunk + 1) * CHUNK - past_len, present_len)
    kf = jnp.minimum(past_len + n_present, n_sel).astype(F32)
    last_chunk = (q_pos0 + (j + 1) * tq - 1) // CHUNK
    half = kb // 2
    n_half_present = (jnp.minimum((last_chunk + 1) * CHUNK - past_len, present_len) + half - 1) // half
    nkb_present = n_half_present // 2
    has_tail = n_half_present % 2 == 1

    def admissible(i, n):
        local = i * kb + lax.broadcasted_iota(jnp.int32, (n, 1), 0)
        return (((past_len + local) // CHUNK) <= qchunk) & (local < present_len)

    def for_blocks(fn, carry=0):
        if n_past:
            carry = lax.fori_loop(0, n_past, lambda i, c: fn(past, i, i, c, kb), carry)
        carry = lax.fori_loop(0, nkb_present, lambda i, c: fn(present, i, n_past + i, c, kb), carry)

        @pl.when(has_tail)
        def _():
            fn(present, nkb_present, n_past + nkb_present, carry, half)

    def rows_of(i, n=kb):
        return pl.ds(pl.multiple_of(i * kb, half), n)

    def keys_values(src, bi, i, n):
        kblk = src[1][bi, rows_of(i, n), :]
        if src is present:
            return [(kblk[:, g * PAD_HEAD:g * PAD_HEAD + HEAD_DIM], src[2][bi, g, i, :, 0:n])
                    for g in range(N_KV_HEADS)]
        vt = src[2][bi, rows_of(i), :].T
        ones = jnp.ones((VX_ROWS - HEAD_DIM, kb), F32)
        out = []
        for g in range(N_KV_HEADS):
            kg = kblk if g == 0 else pltpu.roll(kblk, LANES - g * HEAD_DIM, axis=1)
            vg = jnp.concatenate([vt[g * HEAD_DIM:(g + 1) * HEAD_DIM], ones], axis=0)
            out.append((kg[:, :HEAD_DIM].astype(BF16), vg.astype(BF16)))
        return out

    def search_path():
        qi_all = [jnp.concatenate([qi_ref[bi, h] for h in range(N_IDX_HEADS)], axis=1)
                  for bi in range(nb)]
        wts = [[wt_ref[bi, h:h + 1, :] * (N_IDX_HEADS ** -0.5) for h in range(N_IDX_HEADS)]
               for bi in range(nb)]

        def score_block(src, i, blk, c, n):
            for bi in range(nb):
                s = by_side([jnp.dot(src[0][sides * bi + sd, rows_of(i, n), :].astype(BF16),
                                     qi_all[bi], preferred_element_type=F32)
                             for sd in range(sides)], N_IDX_HEADS)
                acc = None
                for h in range(N_IDX_HEADS):
                    term = wts[bi][h] * jnp.maximum(s[:, h * tq:(h + 1) * tq], 0.0)
                    acc = term if acc is None else acc + term
                if src is present:
                    acc = jnp.where(admissible(i, n), acc, -jnp.inf)
                sc_ref[bi, blk, 0:n, :] = acc
            return c

        for_blocks(score_block)

        for lo, hi, nkb in variants:
            if len(variants) == 1:
                _search_thresholds(sc_ref, thr_ref, need_ref, kf, nkb)
            else:
                pl.when((j >= lo) & (j < hi))(
                    functools.partial(_search_thresholds, sc_ref, thr_ref, need_ref, kf, nkb))

        tri = (lax.broadcasted_iota(jnp.int32, (kb, kb), 1)
               <= lax.broadcasted_iota(jnp.int32, (kb, kb), 0)).astype(BF16)

        def bias_block(src, i, blk, seen, n):
            out = []
            for bi in range(nb):
                x = sc_ref[bi, blk, 0:n, :]
                thr = thr_ref[bi, 0:1, :]
                eq = x == thr
                e = jnp.where(eq, 1.0, 0.0).astype(BF16)
                inblock = jnp.dot(tri[0:n, 0:n], e, preferred_element_type=F32)
                tie = jnp.where(inblock + seen[bi] <= need_ref[bi, 0:1, :], 0.0, -jnp.inf)
                sc_ref[bi, blk, 0:n, :] = jnp.where(eq, tie, jnp.where(x > thr, 0.0, -jnp.inf))
                out.append(seen[bi] + inblock[n - 1:n, :])
            return tuple(out)

        for_blocks(bias_block, tuple(jnp.zeros((1, tq), F32) for _ in range(nb)))

    def all_admissible_path():
        def bias_block(src, i, blk, c, n):
            for bi in range(nb):
                if src is present:
                    sc_ref[bi, blk, 0:n, :] = jnp.where(admissible(i, n), 0.0, -jnp.inf)
                else:
                    sc_ref[bi, blk] = jnp.zeros((kb, tq), F32)
            return c

        for_blocks(bias_block)

    if search_from == 0:
        search_path()
    else:
        pl.when(j >= search_from)(search_path)
        pl.when(j < search_from)(all_admissible_path)

    q_all = [[jnp.concatenate([q_ref[bi, g * rep + h] for h in range(rep)], axis=1)
              for g in range(N_KV_HEADS)] for bi in range(nb)]

    macc_ref[...] = jnp.full(macc_ref.shape, jnp.finfo(F32).min, F32)
    oacc_ref[...] = jnp.zeros(oacc_ref.shape, F32)

    def pv_block(src, i, blk, c, n):
        kvs = [[keys_values(src, sides * bi + sd, i, n) for sd in range(sides)] for bi in range(nb)]
        units = [(bi, g) for bi in range(nb) for g in range(N_KV_HEADS)]
        biases = [jnp.concatenate([sc_ref[bi, blk, 0:n, :]] * rep, axis=1) for bi in range(nb)]
        logits = [by_side([jnp.dot(kvs[bi][sd][g][0], q_all[bi][g], preferred_element_type=F32)
                           for sd in range(sides)], rep) + biases[bi]
                  for bi, g in units]
        probs, alphas = [], []
        for (bi, g), lg in zip(units, logits):
            m_old = macc_ref[bi, g, 0:1, :]
            blk_max = jnp.max(jnp.max(lg.reshape(-1, 8, lg.shape[1]), axis=0), axis=0,
                              keepdims=True)
            m_new = jnp.maximum(m_old, blk_max)
            probs.append(jnp.exp2(lg - m_new).astype(BF16))
            alphas.append(jnp.exp2(m_old - m_new))
            macc_ref[bi, g] = jnp.broadcast_to(m_new, macc_ref.shape[2:])
        own = jnp.concatenate([side_of_lane] * rep, axis=1)
        for (bi, g), p, alpha in zip(units, probs, alphas):
            acc = oacc_ref[bi, g] * alpha
            for sd in range(sides):
                mine = p if sides == 1 else jnp.where(own == sd, p, jnp.zeros_like(p))
                acc = acc + jnp.dot(kvs[bi][sd][g][1], mine, preferred_element_type=F32)
            oacc_ref[bi, g] = acc
        return c

    for_blocks(pv_block)

    t_out = ga_ref.shape[1]
    for bi in range(nb):
        outs = []
        for g in range(N_KV_HEADS):
            acc = oacc_ref[bi, g]
            o = acc[:HEAD_DIM] / acc[HEAD_DIM:HEAD_DIM + 1]
            for h in range(0, rep, 2):
                pair = jnp.concatenate([o[:, h * tq:(h + 1) * tq], o[:, (h + 1) * tq:(h + 2) * tq]],
                                       axis=0)
                outs.append(pair.T)
        o_all = jnp.concatenate(outs, axis=1)
        for sd in range(sides):
            rows = o_all[sd * (tq // sides):sd * (tq // sides) + t_out]
            o_ref[sides * bi + sd] = (rows * ga_ref[sides * bi + sd].astype(F32)).astype(o_ref.dtype)


def _attend_variants(nq, tq, q_pos0, past_len, present_len, n_sel):
    kb = KEY_BLOCK // 2
    variants, search_from = [], nq
    for j in range(nq):
        n_present = min(((q_pos0 + (j + 1) * tq - 1) // CHUNK + 1) * CHUNK - past_len, present_len)
        if past_len + n_present <= n_sel:
            assert not variants
            continue
        search_from = min(search_from, j)
        nkb = past_len // kb + -(-n_present // kb)
        if variants and variants[-1][2] == nkb:
            variants[-1] = (variants[-1][0], j + 1, nkb)
        else:
            variants.append((j, j + 1, nkb))
    return search_from, tuple(variants)


def _attend(q, qi, wt, ga, present, past, q_pos0, past_len, present_len):
    bq, _, _, t_pad = q.shape
    b, t = ga.shape[0], ga.shape[1]
    sides = b // bq
    kb = KEY_BLOCK
    tq = LANES
    nq = t_pad // tq
    tr = t // nq if sides == 1 else t
    nb = max(1, ATTEND_ROWS // tq // sides)
    l_present = present[0].shape[1]
    assert bq % nb == 0 and past_len % kb == 0 and l_present % kb == 0 and q_pos0 >= past_len
    assert (sides == 1 and t % nq == 0 and tr == tq) or (nq == 1 and sides * t == tq)
    n_sel = min(TOPK_MAX, (past_len + present_len) // 4)
    search_from, variants = _attend_variants(nq, tq, q_pos0, past_len, present_len, n_sel)
    nkb_max = past_len // kb + l_present // kb
    kern = functools.partial(_attend_kernel, has_past=past is not None, variants=variants,
                             search_from=search_from, q_pos0=q_pos0, past_len=past_len,
                             present_len=present_len, n_sel=n_sel, sides=sides)

    def kspec(a):
        nd = a.ndim
        return pl.BlockSpec((nb * sides,) + a.shape[1:], lambda i, j: (i,) + (0,) * (nd - 1),
                            pipeline_mode=pl.Buffered(1))

    keys = (tuple(past) if past is not None else ()) + tuple(present)
    return pl.pallas_call(
        kern,
        grid=(bq // nb, nq),
        in_specs=[pl.BlockSpec((nb, N_HEADS, HEAD_DIM, tq), lambda i, j: (i, 0, 0, j)),
                  pl.BlockSpec((nb, N_IDX_HEADS, IDX_DIM, tq), lambda i, j: (i, 0, 0, j)),
                  pl.BlockSpec((nb, N_IDX_HEADS, tq), lambda i, j: (i, 0, j)),
                  pl.BlockSpec((nb * sides, tr, ATTN_WIDTH), lambda i, j: (i, j, 0))]
        + [kspec(a) for a in keys],
        out_specs=pl.BlockSpec((nb * sides, tr, ATTN_WIDTH), lambda i, j: (i, j, 0)),
        out_shape=jax.ShapeDtypeStruct((b, t, ATTN_WIDTH), BF16),
        scratch_shapes=[pltpu.VMEM((nb, nkb_max, kb, tq), F32),
                        pltpu.VMEM((nb, 8, tq), F32), pltpu.VMEM((nb, 8, tq), F32),
                        pltpu.VMEM((nb, N_KV_HEADS, 8, N_HEADS // N_KV_HEADS * tq), F32),
                        pltpu.VMEM((nb, N_KV_HEADS, VX_ROWS, N_HEADS // N_KV_HEADS * tq), F32)],
        compiler_params=_cparams(("parallel", "parallel")),
        name="attend",
    )(q, qi, wt, ga, *keys)


SSM_SLAB = LANES


def _ssm_kernel(u_ref, h0r_ref, h0i_ref, a_ref, bbd_ref, cbd_ref, d_ref,
                y_ref, hr_ref, hi_ref, s_ref, st_ref, perm_ref):
    nb, tt, w = u_ref.shape
    rows = tt * nb
    n = SSM_LANES

    @pl.when(pl.program_id(0) == 0)
    def _():
        st_ref[0] = h0r_ref[...]
        st_ref[1] = h0i_ref[...]
        r_out = lax.broadcasted_iota(jnp.int32, (rows, rows), 0)
        r_in = lax.broadcasted_iota(jnp.int32, (rows, rows), 1)
        perm_ref[0] = (r_in == (r_out % nb) * tt + r_out // nb).astype(BF16)
        perm_ref[1] = (r_in == (r_out % tt) * nb + r_out // tt).astype(BF16)

    u2 = jnp.dot(perm_ref[0], u_ref[...].reshape(rows, w),
                 preferred_element_type=F32).astype(BF16)
    lanes_per_slab = SSM_SLAB // SSM_GROUP * SSM_STATE
    ys = []
    for q in range(w // SSM_SLAB):
        ch = slice(q * SSM_SLAB, (q + 1) * SSM_SLAB)
        re = slice(q * lanes_per_slab, (q + 1) * lanes_per_slab)
        im = slice(n + q * lanes_per_slab, n + (q + 1) * lanes_per_slab)
        s_ref[:, re] = jnp.dot(u2[:, ch], bbd_ref[ch, re], preferred_element_type=F32)
        s_ref[:, im] = jnp.dot(u2[:, ch], bbd_ref[ch, im], preferred_element_type=F32)
        ar = jnp.broadcast_to(a_ref[0:1, re], (nb, lanes_per_slab))
        ai = jnp.broadcast_to(a_ref[1:2, re], (nb, lanes_per_slab))
        hr, hi = st_ref[0, :, re], st_ref[1, :, re]
        for t in range(tt):
            rws = slice(t * nb, (t + 1) * nb)
            hr, hi = (ar * hr - ai * hi + s_ref[rws, re], ar * hi + ai * hr + s_ref[rws, im])
            s_ref[rws, re] = hr
            s_ref[rws, im] = hi
        st_ref[0, :, re] = hr
        st_ref[1, :, re] = hi
        nt = (((1,), (1,)), ((), ()))
        ys.append(lax.dot_general(s_ref[:, re].astype(BF16), cbd_ref[ch, re], nt,
                                  preferred_element_type=F32)
                  + lax.dot_general(s_ref[:, im].astype(BF16), cbd_ref[ch, im], nt,
                                    preferred_element_type=F32))
    y = jnp.concatenate(ys, axis=1)
    y = y + d_ref[...] * u2.astype(F32)
    back = jnp.dot(perm_ref[1], jax.nn.gelu(y).astype(BF16), preferred_element_type=F32)
    y_ref[...] = back.astype(y_ref.dtype).reshape(nb, tt, w)
    hr_ref[...] = st_ref[0]
    hi_ref[...] = st_ref[1]


def _ssm(u, h0r, h0i, a2, bbd, cbd, d_skip):
    nb, t, w = u.shape
    tt = max(1, 512 // nb)
    assert t % tt == 0
    n = SSM_LANES
    return pl.pallas_call(
        _ssm_kernel,
        grid=(t // tt,),
        in_specs=[pl.BlockSpec((nb, tt, w), lambda i: (0, i, 0)),
                  _const_spec((nb, n)), _const_spec((nb, n)), _const_spec((2, n)),
                  _const_spec((w, 2 * n)), _const_spec((w, 2 * n)), _const_spec((1, w))],
        out_specs=[pl.BlockSpec((nb, tt, w), lambda i: (0, i, 0)),
                   _const_spec((nb, n)), _const_spec((nb, n))],
        out_shape=[jax.ShapeDtypeStruct((nb, t, w), BF16),
                   jax.ShapeDtypeStruct((nb, n), F32), jax.ShapeDtypeStruct((nb, n), F32)],
        scratch_shapes=[pltpu.VMEM((tt * nb, 2 * n), F32), pltpu.VMEM((2, nb, n), F32),
                        pltpu.VMEM((2, tt * nb, tt * nb), BF16)],
        compiler_params=_cparams(("arbitrary",)),
        name="ssm",
    )(u, h0r, h0i, a2, bbd, cbd, d_skip.reshape(1, w))


OUT_SPLIT = 2


def _outproj_kernel(x_ref, mod_ref, a_ref, yg_ref, gs_ref, ma_ref, mb_ref,
                    wa_ref, wg_ref, ws_ref, wo_ref, gf_ref, y_ref):
    bb, tt, d = x_ref.shape
    for half in range(OUT_SPLIT):
        if bb >= OUT_SPLIT:
            sel = (slice(half * bb // OUT_SPLIT, (half + 1) * bb // OUT_SPLIT), slice(None))
        else:
            sel = (slice(None), slice(half * tt // OUT_SPLIT, (half + 1) * tt // OUT_SPLIT))
        x = x_ref[sel]
        hb, ht = x.shape[0], x.shape[1]

        def flat(ref):
            return ref[sel].reshape(hb * ht, ref.shape[-1])

        branch_a = jnp.dot(flat(a_ref), wa_ref[...], preferred_element_type=F32)
        g_lin = jnp.dot(flat(yg_ref), wg_ref[...], preferred_element_type=F32)
        y_glu = g_lin[:, :SSM_WIDTH] * jax.nn.sigmoid(g_lin[:, SSM_WIDTH:])
        gated = y_glu * flat(gs_ref).astype(F32)
        branch_b = jnp.dot(gated.astype(BF16), ws_ref[...], preferred_element_type=F32)
        merged = flat(ma_ref).astype(F32) * branch_a + flat(mb_ref).astype(F32) * branch_b
        proj = jnp.dot(merged.astype(BF16), wo_ref[...], preferred_element_type=F32)
        gate = mod_ref[sel[0], :, 2 * d:3 * d]
        xo = x + gate * proj.reshape(hb, ht, d)
        y = xo * lax.rsqrt(jnp.mean(xo * xo, axis=-1, keepdims=True) + NORM_EPS) * gf_ref[...]
        y_ref[sel] = y


def _outproj(x, mod, a, yg, gs, ma, mb, wa, wg, ws, wo, g_final):
    b, t, d = x.shape
    tt = min(t, OUT_SPLIT * ROW_TILE)
    bb = OUT_SPLIT * ROW_TILE // tt
    assert t % tt == 0 and b % bb == 0 and (bb % OUT_SPLIT == 0 or tt % (16 * OUT_SPLIT) == 0)

    def tok(width):
        return pl.BlockSpec((bb, tt, width), lambda i, j: (i, j, 0))

    return pl.pallas_call(
        _outproj_kernel,
        grid=(b // bb, t // tt),
        in_specs=[tok(d), pl.BlockSpec((bb, 1, 3 * d), lambda i, j: (i, 0, 0)),
                  tok(ATTN_WIDTH), tok(SSM_WIDTH), tok(SSM_WIDTH), tok(d), tok(d),
                  _const_spec(wa.shape), _const_spec(wg.shape), _const_spec(ws.shape),
                  _const_spec(wo.shape), _const_spec((1, 1, d))],
        out_specs=tok(d),
        out_shape=jax.ShapeDtypeStruct((b, t, d), F32),
        compiler_params=_cparams(("parallel", "parallel")),
        name="outproj",
    )(x, mod.reshape(b, 1, 3 * d), a, yg, gs, ma, mb, wa, wg, ws, wo, g_final.reshape(1, 1, d))


def _pad_rows(a, rows):
    return jnp.pad(a, ((0, 0), (0, rows - a.shape[1]), (0, 0)))


def _layer(x, mod, pos0, past, prm):
    (g_norm, w_packed, a2, bbd, cbd, d_skip, wg, wa, ws, wo, g_final) = prm
    b, t, _ = x.shape
    pos = pos0 + jnp.arange(t, dtype=jnp.int32)
    q, k, v, qi, ki, wt, kp, vx, kip, ga, u, gs, ma, mb = _inproj(x, mod, g_norm, w_packed, pos)

    t_pad = -(-t // KEY_BLOCK) * KEY_BLOCK
    present = ((_pad_rows(kip, t_pad), _pad_rows(kp, t_pad)) if t_pad != t else (kip, kp)) + (vx,)
    if past is None:
        past_keys, past_len = None, 0
        h0r = jnp.zeros((b, SSM_LANES), F32)
        h0i = jnp.zeros((b, SSM_LANES), F32)
    else:
        ck, cv, cki, h0r, h0i = past
        past_len = ck.shape[1]
        past_keys = (cki, ck.reshape(b, past_len, KV_WIDTH), cv.reshape(b, past_len, KV_WIDTH))
        h0r = h0r.reshape(b, SSM_LANES)
        h0i = h0i.reshape(b, SSM_LANES)

    a = _attend(q, qi, wt, ga, present, past_keys, pos0, past_len, t)

    yg, hr, hi = _ssm(u, h0r, h0i, a2, bbd, cbd, d_skip)

    y = _outproj(x, mod, a, yg, gs, ma, mb, wa, wg, ws, wo, g_final)
    return (y, k.reshape(b, t, N_KV_HEADS, HEAD_DIM), v.reshape(b, t, N_KV_HEADS, HEAD_DIM), ki,
            hr.reshape(b, N_SSM_GROUPS, SSM_STATE), hi.reshape(b, N_SSM_GROUPS, SSM_STATE))


def kernel(x_prompt, x_sample, cache_k, cache_v, cache_idx_k, state_ssm_re, state_ssm_im,
           c_prompt, c_sample, w_mod, b_mod, g_norm, w_in, lambda_re, lambda_im, log_dt,
           ssm_b_re, ssm_b_im, ssm_c_re, ssm_c_im, d_skip, w_glu, w_attn_proj, w_ssm_proj,
           w_out, g_final):
    depth = w_in.shape[0]
    assert depth == 1, "the final norm is fused into the (single) layer's output kernel"
    nbp = x_prompt.shape[0]
    past_len = cache_k.shape[2]

    def layer0(a):
        return a.reshape(a.shape[1:])

    mod = _modulation(jnp.concatenate([c_prompt, c_sample], axis=0), layer0(w_mod), layer0(b_mod))
    mod_p, mod_s = mod[:nbp], mod[nbp:]

    cut = _SEG_OFF["kiwi"][0] + IDX_DIM + N_IDX_HEADS
    w = layer0(w_in)
    col = lax.broadcasted_iota(jnp.int32, (1, IN_PACKED), 1)
    w_packed = jnp.where(col < cut, jnp.pad(w, ((0, 0), (0, _KIWI_PAD))),
                         jnp.where(col < cut + _KIWI_PAD, 0.0,
                                   jnp.pad(w, ((0, 0), (_KIWI_PAD, 0))))).astype(BF16)

    a2, bbd, cbd = _discretize(*map(layer0, (lambda_re, lambda_im, log_dt, ssm_b_re, ssm_b_im,
                                             ssm_c_re, ssm_c_im)))

    prm = (layer0(g_norm), w_packed, a2, bbd, cbd, layer0(d_skip), layer0(w_glu).astype(BF16),
           layer0(w_attn_proj).astype(BF16), layer0(w_ssm_proj).astype(BF16),
           layer0(w_out).astype(BF16), g_final)

    yp, kp, vp, kip, hrp, hip = _layer(x_prompt, mod_p, 0, None, prm)
    past = tuple(map(layer0, (cache_k, cache_v, cache_idx_k, state_ssm_re, state_ssm_im)))
    ys, ks, vs, kis, hrs, his = _layer(x_sample, mod_s, past_len, past, prm)

    def st(z):
        return z[None]

    return (yp, ys, st(kp), st(vp), st(kip), st(hrp), st(hip),
            st(ks), st(vs), st(kis), st(hrs), st(his))
```

```python
import functools
import math

import jax
import jax.numpy as jnp
from jax import lax
from jax.experimental import pallas as pl
from jax.experimental.pallas import tpu as pltpu

F32 = jnp.float32
BF16 = jnp.bfloat16

LANES = 128
VMEM_LIMIT = 56 * 1024 * 1024

D_MODEL = 1024
CHUNK = 64
N_HEADS = 8
N_KV_HEADS = 2
HEAD_DIM = 64
ATTN_WIDTH = N_HEADS * HEAD_DIM
KV_WIDTH = N_KV_HEADS * HEAD_DIM
ROT_DIM = HEAD_DIM // 4
ROPE_THETA = 500000.0
N_IDX_HEADS = 8
IDX_DIM = 64
IDX_WIDTH = N_IDX_HEADS * IDX_DIM
TOPK_MAX = 256
SSM_WIDTH = D_MODEL // 2
SSM_GROUP = 16
N_SSM_GROUPS = SSM_WIDTH // SSM_GROUP
SSM_STATE = 64
SSM_LANES = N_SSM_GROUPS * SSM_STATE
NORM_EPS = 1e-6

_KIWI_PAD = LANES - IDX_DIM - N_IDX_HEADS
_SEG_WIDTHS = (ATTN_WIDTH, 2 * KV_WIDTH, IDX_WIDTH, LANES, ATTN_WIDTH, SSM_WIDTH, SSM_WIDTH,
               D_MODEL, D_MODEL)
_SEG_NAMES = ("q", "kv", "qi", "kiwi", "ga", "u", "gs", "ma", "mb")
_SEG_OFF = {}
_off = 0
for _n, _w in zip(_SEG_NAMES, _SEG_WIDTHS):
    _SEG_OFF[_n] = (_off, _off + _w)
    _off += _w
IN_PACKED = _off
_WI_COL = IDX_DIM

ROW_TILE = 512
PAD_HEAD = 2 * HEAD_DIM
KEY_BLOCK = 256
VX_ROWS = HEAD_DIM + 16
INT_MIN = -2 ** 31


def _cparams(sem):
    return pltpu.CompilerParams(dimension_semantics=sem, vmem_limit_bytes=VMEM_LIMIT)


def _const_spec(shape):
    nd = len(shape)
    return pl.BlockSpec(shape, lambda *_: (0,) * nd)


def _mod_kernel(c_ref, w_ref, b_ref, o_ref):
    s = jax.nn.silu(c_ref[...])
    o_ref[...] = (jnp.dot(s.astype(BF16), w_ref[...].astype(BF16), preferred_element_type=F32)
                  + b_ref[...])


def _modulation(c, w_mod, b_mod):
    n = c.shape[0]
    return pl.pallas_call(
        _mod_kernel,
        out_shape=jax.ShapeDtypeStruct((n, 3 * D_MODEL), F32),
        compiler_params=pltpu.CompilerParams(vmem_limit_bytes=VMEM_LIMIT),
        name="mod",
    )(c, w_mod, b_mod.reshape(1, 3 * D_MODEL))


def _disc_kernel(lre_ref, lim_ref, ldt_ref, bre_ref, bim_ref, cre_ref, cim_ref,
                 a_ref, bbd_ref, cbd_ref):
    n = lre_ref.shape[1]
    dt = jnp.exp(ldt_ref[...])
    lr, li = lre_ref[...], lim_ref[...]
    mag = jnp.exp(lr * dt)
    ar, ai = mag * jnp.cos(li * dt), mag * jnp.sin(li * dt)
    den = lr * lr + li * li
    zr = ((ar - 1.0) * lr + ai * li) / den
    zi = (ai * lr - (ar - 1.0) * li) / den
    a_ref[0:1, :] = ar
    a_ref[1:2, :] = ai

    eye = (lax.broadcasted_iota(jnp.int32, (SSM_GROUP, SSM_GROUP), 0)
           == lax.broadcasted_iota(jnp.int32, (SSM_GROUP, SSM_GROUP), 1)).astype(BF16)

    def transposed(x):
        out, rest = None, x
        for _ in range(3):
            limb = rest.astype(BF16)
            rest = rest - limb.astype(F32)
            t = lax.dot_general(eye, limb, (((1,), (1,)), ((), ())), preferred_element_type=F32)
            out = t if out is None else out + t
        return out

    br, bi = transposed(bre_ref[...]), transposed(bim_ref[...])
    bbar_re = zr * br - zi * bi
    bbar_im = zr * bi + zi * br
    group_of_lane = lax.broadcasted_iota(jnp.int32, (SSM_GROUP, n), 1) // SSM_STATE
    for g in range(n // SSM_STATE):
        rows = slice(g * SSM_GROUP, (g + 1) * SSM_GROUP)
        mine = group_of_lane == g
        bbd_ref[rows, 0:n] = jnp.where(mine, bbar_re, 0.0).astype(bbd_ref.dtype)
        bbd_ref[rows, n:2 * n] = jnp.where(mine, bbar_im, 0.0).astype(bbd_ref.dtype)
    rows_n = cre_ref.shape[0]
    diag = (lax.broadcasted_iota(jnp.int32, (rows_n, n), 0) // SSM_GROUP
            == lax.broadcasted_iota(jnp.int32, (rows_n, n), 1) // SSM_STATE)
    cbd_ref[:, 0:n] = jnp.where(diag, cre_ref[...], 0.0).astype(cbd_ref.dtype)
    cbd_ref[:, n:2 * n] = jnp.where(diag, -cim_ref[...], 0.0).astype(cbd_ref.dtype)


def _discretize(lambda_re, lambda_im, log_dt, b_re, b_im, c_re, c_im):
    g, p, c = b_re.shape
    n = g * p

    def tiled(z):
        return jnp.tile(z.reshape(g * c, p), (1, g))

    return pl.pallas_call(
        _disc_kernel,
        out_shape=(jax.ShapeDtypeStruct((2, n), F32), jax.ShapeDtypeStruct((g * c, 2 * n), BF16),
                   jax.ShapeDtypeStruct((g * c, 2 * n), BF16)),
        compiler_params=pltpu.CompilerParams(vmem_limit_bytes=VMEM_LIMIT),
        name="disc",
    )(lambda_re.reshape(1, n), lambda_im.reshape(1, n), jnp.repeat(log_dt, p).reshape(1, n),
      b_re.reshape(n, c), b_im.reshape(n, c), tiled(c_re), tiled(c_im))


def _rope_block(z, cos, sa, sb):
    up = pltpu.roll(z, LANES - ROT_DIM // 2, axis=1)
    dn = pltpu.roll(z, ROT_DIM // 2, axis=1)
    return z * cos + up * sa + dn * sb


def _split_heads(z, fill):
    low = lax.broadcasted_iota(jnp.int32, z.shape, 1) < HEAD_DIM
    return jnp.concatenate([jnp.where(low, z, fill),
                            jnp.where(low, pltpu.roll(z, HEAD_DIM, axis=1), fill)], axis=1)


def _inproj_kernel(x_ref, mod_ref, g_ref, w_ref, rope2_ref, rope1_ref,
                   q_ref, k_ref, v_ref, qi_ref, ki_ref, wt_ref, kp_ref, vx_ref, kip_ref,
                   ga_ref, u_ref, gs_ref, ma_ref, mb_ref):
    bb, tt, d = x_ref.shape
    rows = bb * tt
    x = x_ref[...]
    y = x * lax.rsqrt(jnp.mean(x * x, axis=-1, keepdims=True) + NORM_EPS) * g_ref[...]
    shift = mod_ref[:, :, 0:d]
    scale = mod_ref[:, :, d:2 * d]
    h = (y * (1.0 + scale) + shift).reshape(rows, d).astype(BF16)

    def seg(name):
        a, b = _SEG_OFF[name]
        return jnp.dot(h, w_ref[:, a:b], preferred_element_type=F32)

    def roped(z, tab_ref):
        cos, sa, sb = tab_ref[0], tab_ref[1], tab_ref[2]
        blocks = [_rope_block(z[:, i:i + LANES], cos, sa, sb) for i in range(0, z.shape[1], LANES)]
        return blocks[0] if len(blocks) == 1 else jnp.concatenate(blocks, axis=1)

    def put(ref, val):
        ref[...] = val.reshape(ref.shape).astype(ref.dtype)

    def split_all(z, fill=0.0):
        return jnp.concatenate([_split_heads(z[:, i:i + LANES], fill)
                                for i in range(0, z.shape[1], LANES)], axis=1)

    def token_lanes(z, b, lanes):
        blk = z[b * tt:(b + 1) * tt]
        if lanes > tt:
            blk = jnp.concatenate([blk, jnp.zeros((lanes - tt, LANES), F32)], axis=0)
        return blk.T

    def query_lanes(z, p, lanes):
        return z[p * lanes:(p + 1) * lanes].T

    def put_heads_t(ref, z):
        for p in range(ref.shape[0]):
            for pair in range(ref.shape[1] // 2):
                t = query_lanes(z[:, pair * LANES:(pair + 1) * LANES], p, ref.shape[3])
                ref[p, 2 * pair] = t[:HEAD_DIM].astype(ref.dtype)
                ref[p, 2 * pair + 1] = t[HEAD_DIM:].astype(ref.dtype)

    put_heads_t(q_ref, roped(seg("q"), rope2_ref) * (HEAD_DIM ** -0.5 * math.log2(math.e)))
    kv = seg("kv")
    k = roped(kv[:, :KV_WIDTH], rope2_ref)
    put(k_ref, k)
    put(kp_ref, split_all(k))
    v = kv[:, KV_WIDTH:]
    put(v_ref, v)
    n_kb = vx_ref.shape[2]
    ones = jnp.ones((VX_ROWS - HEAD_DIM, KEY_BLOCK), F32)
    for b in range(bb):
        vt = token_lanes(v, b, n_kb * KEY_BLOCK)
        for g in range(N_KV_HEADS):
            for kb_i in range(n_kb):
                blk = vt[g * HEAD_DIM:(g + 1) * HEAD_DIM, kb_i * KEY_BLOCK:(kb_i + 1) * KEY_BLOCK]
                vx_ref[b, g, kb_i] = jnp.concatenate([blk, ones], axis=0).astype(vx_ref.dtype)
    put_heads_t(qi_ref, roped(seg("qi"), rope2_ref) * (IDX_DIM ** -0.5))
    kiwi = roped(seg("kiwi"), rope1_ref)
    for p in range(wt_ref.shape[0]):
        wt_ref[p] = query_lanes(kiwi, p, wt_ref.shape[2])[_WI_COL:_WI_COL + N_IDX_HEADS]
    put(ki_ref, kiwi[:, 0:IDX_DIM])
    put(kip_ref, kiwi[:, 0:IDX_DIM])
    put(ga_ref, jax.nn.silu(seg("ga")))
    put(u_ref, seg("u"))
    put(gs_ref, jax.nn.silu(seg("gs")))
    put(ma_ref, jax.nn.sigmoid(seg("ma")))
    put(mb_ref, jax.nn.sigmoid(seg("mb")))


def _rope_tables(pos, heads_in_block):
    half = ROT_DIM // 2
    inv = jnp.power(ROPE_THETA, -jnp.arange(half, dtype=F32) * (2.0 / ROT_DIM))
    ang = pos.astype(F32)[:, None] * inv[None, :]
    cos, sin = jnp.cos(ang), jnp.sin(ang)
    t = pos.shape[0]
    ones = jnp.ones((t, HEAD_DIM - ROT_DIM), F32)
    zeros = jnp.zeros((t, HEAD_DIM - ROT_DIM), F32)
    zh = jnp.zeros((t, half), F32)
    c_head = jnp.concatenate([cos, cos, ones], axis=1)
    sa_head = jnp.concatenate([-sin, zh, zeros], axis=1)
    sb_head = jnp.concatenate([zh, sin, zeros], axis=1)
    n_id = LANES // HEAD_DIM - heads_in_block
    ident = [jnp.ones((t, HEAD_DIM), F32)] * n_id
    zero = [jnp.zeros((t, HEAD_DIM), F32)] * n_id
    c = jnp.concatenate([c_head] * heads_in_block + ident, axis=1)
    sa = jnp.concatenate([sa_head] * heads_in_block + zero, axis=1)
    sb = jnp.concatenate([sb_head] * heads_in_block + zero, axis=1)
    return jnp.stack([c, sa, sb])


def _row_blocking(b, t):
    tt = min(t, ROW_TILE)
    bb = ROW_TILE // tt
    assert t % tt == 0 and b % bb == 0 and tt % 16 == 0
    return bb, tt


def _inproj(x, mod, g_norm, w_packed, pos):
    b, t, d = x.shape
    bb, tt = _row_blocking(b, t)
    rows = bb * tt
    rope2 = jnp.tile(_rope_tables(pos, 2), (1, bb, 1)) if bb > 1 else _rope_tables(pos, 2)
    rope1 = jnp.tile(_rope_tables(pos, 1), (1, bb, 1)) if bb > 1 else _rope_tables(pos, 1)

    def tok(width, dtype):
        return (jax.ShapeDtypeStruct((b, t, width), dtype),
                pl.BlockSpec((bb, tt, width), lambda i, j: (i, j, 0)))

    tq = max(tt, LANES)
    grp = tq // tt
    assert bb % grp == 0
    nkb = -(-tt // KEY_BLOCK)

    def heads_t(n):
        return (jax.ShapeDtypeStruct((b // grp, n, HEAD_DIM, t // tt * tt * grp), BF16),
                pl.BlockSpec((bb // grp, n, HEAD_DIM, tq), lambda i, j: (i, 0, 0, j)))

    w_t = (jax.ShapeDtypeStruct((b // grp, N_IDX_HEADS, t // tt * tt * grp), F32),
           pl.BlockSpec((bb // grp, N_IDX_HEADS, tq), lambda i, j: (i, 0, j)))
    vx_t = (jax.ShapeDtypeStruct((b, N_KV_HEADS, t // tt * nkb, VX_ROWS, KEY_BLOCK), BF16),
            pl.BlockSpec((bb, N_KV_HEADS, nkb, VX_ROWS, KEY_BLOCK), lambda i, j: (i, 0, j, 0, 0)))

    outs = [heads_t(N_HEADS), tok(KV_WIDTH, F32), tok(KV_WIDTH, F32),
            heads_t(N_IDX_HEADS), tok(IDX_DIM, F32), w_t,
            tok(2 * KV_WIDTH, BF16), vx_t, tok(IDX_DIM, BF16),
            tok(ATTN_WIDTH, BF16), tok(SSM_WIDTH, BF16),
            tok(SSM_WIDTH, BF16), tok(D_MODEL, BF16), tok(D_MODEL, BF16)]
    return pl.pallas_call(
        _inproj_kernel,
        grid=(b // bb, t // tt),
        in_specs=[
            pl.BlockSpec((bb, tt, d), lambda i, j: (i, j, 0)),
            pl.BlockSpec((bb, 1, 3 * d), lambda i, j: (i, 0, 0)),
            _const_spec((1, 1, d)),
            pl.BlockSpec((d, IN_PACKED), lambda i, j: (0, 0), pipeline_mode=pl.Buffered(1)),
            pl.BlockSpec((3, rows, LANES), lambda i, j: (0, j, 0)),
            pl.BlockSpec((3, rows, LANES), lambda i, j: (0, j, 0)),
        ],
        out_specs=[o[1] for o in outs],
        out_shape=[o[0] for o in outs],
        compiler_params=_cparams(("parallel", "parallel")),
        name="inproj",
    )(x, mod.reshape(b, 1, 3 * d), g_norm.reshape(1, 1, d), w_packed, rope2, rope1)


ATTEND_ROWS = 1024


def _float_of_key(key):
    bits = jnp.where(key >= 0, key, key ^ jnp.int32(0x7FFFFFFF))
    return lax.bitcast_convert_type(bits, F32)


def _count(sc_ref, bi, n_half, pred):
    half = sc_ref.shape[2] // 2
    acc = None
    for blk in range(-(-n_half // 2)):
        x = sc_ref[bi, blk] if 2 * blk + 1 < n_half else sc_ref[bi, blk, 0:half, :]
        hit = jnp.where(pred(x), 1.0, 0.0)
        part = jnp.sum(hit.reshape(-1, 8, hit.shape[1]), axis=0)
        acc = part if acc is None else acc + part
    return jnp.sum(acc, axis=0, keepdims=True)


def _search_thresholds(sc_ref, thr_ref, need_ref, kf, nkb):
    nb, _, _, tq = sc_ref.shape

    def body(i, keys):
        bit = jnp.left_shift(jnp.int32(1), 31 - i)
        out = []
        for bi, key in enumerate(keys):
            cand = key ^ bit
            cf = _float_of_key(cand)
            cnt = _count(sc_ref, bi, nkb, lambda x, cf=cf: x >= cf)
            out.append(jnp.where(cnt >= kf, cand, key))
        return tuple(out)

    init = tuple(jnp.full((1, tq), INT_MIN, jnp.int32) for _ in range(nb))
    keys = lax.fori_loop(0, 32, body, init)
    for bi, key in enumerate(keys):
        thr = _float_of_key(key)
        need = kf - _count(sc_ref, bi, nkb, lambda x, thr=thr: x > thr)
        thr_ref[bi] = jnp.broadcast_to(thr, thr_ref.shape[1:])
        need_ref[bi] = jnp.broadcast_to(need, need_ref.shape[1:])


def _attend_kernel(*refs, has_past, variants, search_from, q_pos0, past_len, present_len, n_sel,
                   sides):
    q_ref, qi_ref, wt_ref, ga_ref = refs[:4]
    n_in = 10 if has_past else 7
    past = refs[4:7] if has_past else None
    present = refs[n_in - 3:n_in]
    o_ref = refs[n_in]
    sc_ref, thr_ref, need_ref, macc_ref, oacc_ref = refs[n_in + 1:]
    nb, tq = q_ref.shape[0], q_ref.shape[3]
    kb = KEY_BLOCK
    n_past = past_len // kb
    rep = N_HEADS // N_KV_HEADS
    j = pl.program_id(1)

    lane = lax.broadcasted_iota(jnp.int32, (1, tq), 1)
    qpos = q_pos0 + j * tq + lane % (tq // sides)
    side_of_lane = lane // (tq // sides)
    qchunk = qpos // CHUNK

    def by_side(parts, width):
        out = parts[0]
        own = jnp.concatenate([side_of_lane] * width, axis=1)
        for sd in range(1, sides):
            out = jnp.where(own == sd, parts[sd], out)
        return out
    n_present = jnp.minimum((qchunk + 1) * CHUNK - past_len, present_len)
    kf = jnp.minimum(past_len + n_present, n_sel).astype(F32)
    last_chunk = (q_pos0 + (j + 1) * tq - 1) // CHUNK
    half = kb // 2
    n_half_present = (jnp.minimum((last_chunk + 1) * CHUNK - past_len, present_len) + half - 1) // half
    nkb_present = n_half_present // 2
    has_tail = n_half_present % 2 == 1

    def admissible(i, n):
        local = i * kb + lax.broadcasted_iota(jnp.int32, (n, 1), 0)
        return (((past_len + local) // CHUNK) <= qchunk) & (local < present_len)

    def for_blocks(fn, carry=0):
        if n_past:
            carry = lax.fori_loop(0, n_past, lambda i, c: fn(past, i, i, c, kb), carry)
        carry = lax.fori_loop(0, nkb_present, lambda i, c: fn(present, i, n_past + i, c, kb), carry)

        @pl.when(has_tail)
        def _():
            fn(present, nkb_present, n_past + nkb_present, carry, half)

    def rows_of(i, n=kb):
        return pl.ds(pl.multiple_of(i * kb, half), n)

    def keys_values(src, bi, i, n):
        kblk = src[1][bi, rows_of(i, n), :]
        if src is present:
            return [(kblk[:, g * PAD_HEAD:g * PAD_HEAD + HEAD_DIM], src[2][bi, g, i, :, 0:n])
                    for g in range(N_KV_HEADS)]
        vt = src[2][bi, rows_of(i), :].T
        ones = jnp.ones((VX_ROWS - HEAD_DIM, kb), F32)
        out = []
        for g in range(N_KV_HEADS):
            kg = kblk if g == 0 else pltpu.roll(kblk, LANES - g * HEAD_DIM, axis=1)
            vg = jnp.concatenate([vt[g * HEAD_DIM:(g + 1) * HEAD_DIM], ones], axis=0)
            out.append((kg[:, :HEAD_DIM].astype(BF16), vg.astype(BF16)))
        return out

    def search_path():
        qi_all = [jnp.concatenate([qi_ref[bi, h] for h in range(N_IDX_HEADS)], axis=1)
                  for bi in range(nb)]
        wts = [[wt_ref[bi, h:h + 1, :] * (N_IDX_HEADS ** -0.5) for h in range(N_IDX_HEADS)]
               for bi in range(nb)]

        def score_block(src, i, blk, c, n):
            for bi in range(nb):
                s = by_side([jnp.dot(src[0][sides * bi + sd, rows_of(i, n), :].astype(BF16),
                                     qi_all[bi], preferred_element_type=F32)
                             for sd in range(sides)], N_IDX_HEADS)
                acc = None
                for h in range(N_IDX_HEADS):
                    term = wts[bi][h] * jnp.maximum(s[:, h * tq:(h + 1) * tq], 0.0)
                    acc = term if acc is None else acc + term
                if src is present:
                    acc = jnp.where(admissible(i, n), acc, -jnp.inf)
                sc_ref[bi, blk, 0:n, :] = acc
            return c

        for_blocks(score_block)

        for lo, hi, nkb in variants:
            if len(variants) == 1:
                _search_thresholds(sc_ref, thr_ref, need_ref, kf, nkb)
            else:
                pl.when((j >= lo) & (j < hi))(
                    functools.partial(_search_thresholds, sc_ref, thr_ref, need_ref, kf, nkb))

        tri = (lax.broadcasted_iota(jnp.int32, (kb, kb), 1)
               <= lax.broadcasted_iota(jnp.int32, (kb, kb), 0)).astype(BF16)

        def bias_block(src, i, blk, seen, n):
            out = []
            for bi in range(nb):
                x = sc_ref[bi, blk, 0:n, :]
                thr = thr_ref[bi, 0:1, :]
                eq = x == thr
                e = jnp.where(eq, 1.0, 0.0).astype(BF16)
                inblock = jnp.dot(tri[0:n, 0:n], e, preferred_element_type=F32)
                tie = jnp.where(inblock + seen[bi] <= need_ref[bi, 0:1, :], 0.0, -jnp.inf)
                sc_ref[bi, blk, 0:n, :] = jnp.where(eq, tie, jnp.where(x > thr, 0.0, -jnp.inf))
                out.append(seen[bi] + inblock[n - 1:n, :])
            return tuple(out)

        for_blocks(bias_block, tuple(jnp.zeros((1, tq), F32) for _ in range(nb)))

    def all_admissible_path():
        def bias_block(src, i, blk, c, n):
            for bi in range(nb):
                if src is present:
                    sc_ref[bi, blk, 0:n, :] = jnp.where(admissible(i, n), 0.0, -jnp.inf)
                else:
                    sc_ref[bi, blk] = jnp.zeros((kb, tq), F32)
            return c

        for_blocks(bias_block)

    if search_from == 0:
        search_path()
    else:
        pl.when(j >= search_from)(search_path)
        pl.when(j < search_from)(all_admissible_path)

    q_all = [[jnp.concatenate([q_ref[bi, g * rep + h] for h in range(rep)], axis=1)
              for g in range(N_KV_HEADS)] for bi in range(nb)]

    macc_ref[...] = jnp.full(macc_ref.shape, jnp.finfo(F32).min, F32)
    oacc_ref[...] = jnp.zeros(oacc_ref.shape, F32)

    def pv_block(src, i, blk, c, n):
        kvs = [[keys_values(src, sides * bi + sd, i, n) for sd in range(sides)] for bi in range(nb)]
        units = [(bi, g) for bi in range(nb) for g in range(N_KV_HEADS)]
        biases = [jnp.concatenate([sc_ref[bi, blk, 0:n, :]] * rep, axis=1) for bi in range(nb)]
        logits = [by_side([jnp.dot(kvs[bi][sd][g][0], q_all[bi][g], preferred_element_type=F32)
                           for sd in range(sides)], rep) + biases[bi]
                  for bi, g in units]
        probs, alphas = [], []
        for (bi, g), lg in zip(units, logits):
            m_old = macc_ref[bi, g, 0:1, :]
            blk_max = jnp.max(jnp.max(lg.reshape(-1, 8, lg.shape[1]), axis=0), axis=0,
                              keepdims=True)
            m_new = jnp.maximum(m_old, blk_max)
            probs.append(jnp.exp2(lg - m_new).astype(BF16))
            alphas.append(jnp.exp2(m_old - m_new))
            macc_ref[bi, g] = jnp.broadcast_to(m_new, macc_ref.shape[2:])
        own = jnp.concatenate([side_of_lane] * rep, axis=1)
        for (bi, g), p, alpha in zip(units, probs, alphas):
            acc = oacc_ref[bi, g] * alpha
            for sd in range(sides):
                mine = p if sides == 1 else jnp.where(own == sd, p, jnp.zeros_like(p))
                acc = acc + jnp.dot(kvs[bi][sd][g][1], mine, preferred_element_type=F32)
            oacc_ref[bi, g] = acc
        return c

    for_blocks(pv_block)

    t_out = ga_ref.shape[1]
    for bi in range(nb):
        outs = []
        for g in range(N_KV_HEADS):
            acc = oacc_ref[bi, g]
            o = acc[:HEAD_DIM] / acc[HEAD_DIM:HEAD_DIM + 1]
            for h in range(0, rep, 2):
                pair = jnp.concatenate([o[:, h * tq:(h + 1) * tq], o[:, (h + 1) * tq:(h + 2) * tq]],
                                       axis=0)
                outs.append(pair.T)
        o_all = jnp.concatenate(outs, axis=1)
        for sd in range(sides):
            rows = o_all[sd * (tq // sides):sd * (tq // sides) + t_out]
            o_ref[sides * bi + sd] = (rows * ga_ref[sides * bi + sd].astype(F32)).astype(o_ref.dtype)


def _attend_variants(nq, tq, q_pos0, past_len, present_len, n_sel):
    kb = KEY_BLOCK // 2
    variants, search_from = [], nq
    for j in range(nq):
        n_present = min(((q_pos0 + (j + 1) * tq - 1) // CHUNK + 1) * CHUNK - past_len, present_len)
        if past_len + n_present <= n_sel:
            assert not variants
            continue
        search_from = min(search_from, j)
        nkb = past_len // kb + -(-n_present // kb)
        if variants and variants[-1][2] == nkb:
            variants[-1] = (variants[-1][0], j + 1, nkb)
        else:
            variants.append((j, j + 1, nkb))
    return search_from, tuple(variants)


def _attend(q, qi, wt, ga, present, past, q_pos0, past_len, present_len):
    bq, _, _, t_pad = q.shape
    b, t = ga.shape[0], ga.shape[1]
    sides = b // bq
    kb = KEY_BLOCK
    tq = LANES
    nq = t_pad // tq
    tr = t // nq if sides == 1 else t
    nb = max(1, ATTEND_ROWS // tq)
    l_present = present[0].shape[1]
    assert bq % nb == 0 and past_len % kb == 0 and l_present % kb == 0 and q_pos0 >= past_len
    assert (sides == 1 and t % nq == 0 and tr == tq) or (nq == 1 and sides * t == tq)
    n_sel = min(TOPK_MAX, (past_len + present_len) // 4)
    search_from, variants = _attend_variants(nq, tq, q_pos0, past_len, present_len, n_sel)
    nkb_max = past_len // kb + l_present // kb
    kern = functools.partial(_attend_kernel, has_past=past is not None, variants=variants,
                             search_from=search_from, q_pos0=q_pos0, past_len=past_len,
                             present_len=present_len, n_sel=n_sel, sides=sides)

    def kspec(a):
        nd = a.ndim
        return pl.BlockSpec((nb * sides,) + a.shape[1:], lambda i, j: (i,) + (0,) * (nd - 1),
                            pipeline_mode=pl.Buffered(1))

    keys = (tuple(past) if past is not None else ()) + tuple(present)
    return pl.pallas_call(
        kern,
        grid=(bq // nb, nq),
        in_specs=[pl.BlockSpec((nb, N_HEADS, HEAD_DIM, tq), lambda i, j: (i, 0, 0, j)),
                  pl.BlockSpec((nb, N_IDX_HEADS, IDX_DIM, tq), lambda i, j: (i, 0, 0, j)),
                  pl.BlockSpec((nb, N_IDX_HEADS, tq), lambda i, j: (i, 0, j)),
                  pl.BlockSpec((nb * sides, tr, ATTN_WIDTH), lambda i, j: (i, j, 0))]
        + [kspec(a) for a in keys],
        out_specs=pl.BlockSpec((nb * sides, tr, ATTN_WIDTH), lambda i, j: (i, j, 0)),
        out_shape=jax.ShapeDtypeStruct((b, t, ATTN_WIDTH), BF16),
        scratch_shapes=[pltpu.VMEM((nb, nkb_max, kb, tq), F32),
                        pltpu.VMEM((nb, 8, tq), F32), pltpu.VMEM((nb, 8, tq), F32),
                        pltpu.VMEM((nb, N_KV_HEADS, 8, N_HEADS // N_KV_HEADS * tq), F32),
                        pltpu.VMEM((nb, N_KV_HEADS, VX_ROWS, N_HEADS // N_KV_HEADS * tq), F32)],
        compiler_params=_cparams(("parallel", "parallel")),
        name="attend",
    )(q, qi, wt, ga, *keys)


SSM_SLAB = LANES


def _ssm_kernel(u_ref, h0r_ref, h0i_ref, a_ref, bbd_ref, cbd_ref, d_ref,
                y_ref, hr_ref, hi_ref, s_ref, st_ref, perm_ref):
    nb, tt, w = u_ref.shape
    rows = tt * nb
    n = SSM_LANES

    @pl.when(pl.program_id(0) == 0)
    def _():
        st_ref[0] = h0r_ref[...]
        st_ref[1] = h0i_ref[...]
        r_out = lax.broadcasted_iota(jnp.int32, (rows, rows), 0)
        r_in = lax.broadcasted_iota(jnp.int32, (rows, rows), 1)
        perm_ref[0] = (r_in == (r_out % nb) * tt + r_out // nb).astype(BF16)
        perm_ref[1] = (r_in == (r_out % tt) * nb + r_out // tt).astype(BF16)

    u2 = jnp.dot(perm_ref[0], u_ref[...].reshape(rows, w),
                 preferred_element_type=F32).astype(BF16)
    lanes_per_slab = SSM_SLAB // SSM_GROUP * SSM_STATE
    ys = []
    for q in range(w // SSM_SLAB):
        ch = slice(q * SSM_SLAB, (q + 1) * SSM_SLAB)
        re = slice(q * lanes_per_slab, (q + 1) * lanes_per_slab)
        im = slice(n + q * lanes_per_slab, n + (q + 1) * lanes_per_slab)
        s_ref[:, re] = jnp.dot(u2[:, ch], bbd_ref[ch, re], preferred_element_type=F32)
        s_ref[:, im] = jnp.dot(u2[:, ch], bbd_ref[ch, im], preferred_element_type=F32)
        ar = jnp.broadcast_to(a_ref[0:1, re], (nb, lanes_per_slab))
        ai = jnp.broadcast_to(a_ref[1:2, re], (nb, lanes_per_slab))
        hr, hi = st_ref[0, :, re], st_ref[1, :, re]
        for t in range(tt):
            rws = slice(t * nb, (t + 1) * nb)
            hr, hi = (ar * hr - ai * hi + s_ref[rws, re], ar * hi + ai * hr + s_ref[rws, im])
            s_ref[rws, re] = hr
            s_ref[rws, im] = hi
        st_ref[0, :, re] = hr
        st_ref[1, :, re] = hi
        nt = (((1,), (1,)), ((), ()))
        ys.append(lax.dot_general(s_ref[:, re].astype(BF16), cbd_ref[ch, re], nt,
                                  preferred_element_type=F32)
                  + lax.dot_general(s_ref[:, im].astype(BF16), cbd_ref[ch, im], nt,
                                    preferred_element_type=F32))
    y = jnp.concatenate(ys, axis=1)
    y = y + d_ref[...] * u2.astype(F32)
    back = jnp.dot(perm_ref[1], jax.nn.gelu(y).astype(BF16), preferred_element_type=F32)
    y_ref[...] = back.astype(y_ref.dtype).reshape(nb, tt, w)
    hr_ref[...] = st_ref[0]
    hi_ref[...] = st_ref[1]


def _ssm(u, h0r, h0i, a2, bbd, cbd, d_skip):
    nb, t, w = u.shape
    tt = max(1, 512 // nb)
    assert t % tt == 0
    n = SSM_LANES
    return pl.pallas_call(
        _ssm_kernel,
        grid=(t // tt,),
        in_specs=[pl.BlockSpec((nb, tt, w), lambda i: (0, i, 0)),
                  _const_spec((nb, n)), _const_spec((nb, n)), _const_spec((2, n)),
                  _const_spec((w, 2 * n)), _const_spec((w, 2 * n)), _const_spec((1, w))],
        out_specs=[pl.BlockSpec((nb, tt, w), lambda i: (0, i, 0)),
                   _const_spec((nb, n)), _const_spec((nb, n))],
        out_shape=[jax.ShapeDtypeStruct((nb, t, w), BF16),
                   jax.ShapeDtypeStruct((nb, n), F32), jax.ShapeDtypeStruct((nb, n), F32)],
        scratch_shapes=[pltpu.VMEM((tt * nb, 2 * n), F32), pltpu.VMEM((2, nb, n), F32),
                        pltpu.VMEM((2, tt * nb, tt * nb), BF16)],
        compiler_params=_cparams(("arbitrary",)),
        name="ssm",
    )(u, h0r, h0i, a2, bbd, cbd, d_skip.reshape(1, w))


OUT_SPLIT = 2


def _outproj_kernel(x_ref, mod_ref, a_ref, yg_ref, gs_ref, ma_ref, mb_ref,
                    wa_ref, wg_ref, ws_ref, wo_ref, gf_ref, y_ref):
    bb, tt, d = x_ref.shape
    for half in range(OUT_SPLIT):
        if bb >= OUT_SPLIT:
            sel = (slice(half * bb // OUT_SPLIT, (half + 1) * bb // OUT_SPLIT), slice(None))
        else:
            sel = (slice(None), slice(half * tt // OUT_SPLIT, (half + 1) * tt // OUT_SPLIT))
        x = x_ref[sel]
        hb, ht = x.shape[0], x.shape[1]

        def flat(ref):
            return ref[sel].reshape(hb * ht, ref.shape[-1])

        branch_a = jnp.dot(flat(a_ref), wa_ref[...], preferred_element_type=F32)
        g_lin = jnp.dot(flat(yg_ref), wg_ref[...], preferred_element_type=F32)
        y_glu = g_lin[:, :SSM_WIDTH] * jax.nn.sigmoid(g_lin[:, SSM_WIDTH:])
        gated = y_glu * flat(gs_ref).astype(F32)
        branch_b = jnp.dot(gated.astype(BF16), ws_ref[...], preferred_element_type=F32)
        merged = flat(ma_ref).astype(F32) * branch_a + flat(mb_ref).astype(F32) * branch_b
        proj = jnp.dot(merged.astype(BF16), wo_ref[...], preferred_element_type=F32)
        gate = mod_ref[sel[0], :, 2 * d:3 * d]
        xo = x + gate * proj.reshape(hb, ht, d)
        y = xo * lax.rsqrt(jnp.mean(xo * xo, axis=-1, keepdims=True) + NORM_EPS) * gf_ref[...]
        y_ref[sel] = y


def _outproj(x, mod, a, yg, gs, ma, mb, wa, wg, ws, wo, g_final):
    b, t, d = x.shape
    tt = min(t, OUT_SPLIT * ROW_TILE)
    bb = OUT_SPLIT * ROW_TILE // tt
    assert t % tt == 0 and b % bb == 0 and (bb % OUT_SPLIT == 0 or tt % (16 * OUT_SPLIT) == 0)

    def tok(width):
        return pl.BlockSpec((bb, tt, width), lambda i, j: (i, j, 0))

    return pl.pallas_call(
        _outproj_kernel,
        grid=(b // bb, t // tt),
        in_specs=[tok(d), pl.BlockSpec((bb, 1, 3 * d), lambda i, j: (i, 0, 0)),
                  tok(ATTN_WIDTH), tok(SSM_WIDTH), tok(SSM_WIDTH), tok(d), tok(d),
                  _const_spec(wa.shape), _const_spec(wg.shape), _const_spec(ws.shape),
                  _const_spec(wo.shape), _const_spec((1, 1, d))],
        out_specs=tok(d),
        out_shape=jax.ShapeDtypeStruct((b, t, d), F32),
        compiler_params=_cparams(("parallel", "parallel")),
        name="outproj",
    )(x, mod.reshape(b, 1, 3 * d), a, yg, gs, ma, mb, wa, wg, ws, wo, g_final.reshape(1, 1, d))


def _pad_rows(a, rows):
    return jnp.pad(a, ((0, 0), (0, rows - a.shape[1]), (0, 0)))


def _layer(x, mod, pos0, past, prm):
    (g_norm, w_packed, a2, bbd, cbd, d_skip, wg, wa, ws, wo, g_final) = prm
    b, t, _ = x.shape
    pos = pos0 + jnp.arange(t, dtype=jnp.int32)
    q, k, v, qi, ki, wt, kp, vx, kip, ga, u, gs, ma, mb = _inproj(x, mod, g_norm, w_packed, pos)

    t_pad = -(-t // KEY_BLOCK) * KEY_BLOCK
    present = ((_pad_rows(kip, t_pad), _pad_rows(kp, t_pad)) if t_pad != t else (kip, kp)) + (vx,)
    if past is None:
        past_keys, past_len = None, 0
        h0r = jnp.zeros((b, SSM_LANES), F32)
        h0i = jnp.zeros((b, SSM_LANES), F32)
    else:
        ck, cv, cki, h0r, h0i = past
        past_len = ck.shape[1]
        past_keys = (cki, ck.reshape(b, past_len, KV_WIDTH), cv.reshape(b, past_len, KV_WIDTH))
        h0r = h0r.reshape(b, SSM_LANES)
        h0i = h0i.reshape(b, SSM_LANES)

    a = _attend(q, qi, wt, ga, present, past_keys, pos0, past_len, t)

    yg, hr, hi = _ssm(u, h0r, h0i, a2, bbd, cbd, d_skip)

    y = _outproj(x, mod, a, yg, gs, ma, mb, wa, wg, ws, wo, g_final)
    return (y, k.reshape(b, t, N_KV_HEADS, HEAD_DIM), v.reshape(b, t, N_KV_HEADS, HEAD_DIM), ki,
            hr.reshape(b, N_SSM_GROUPS, SSM_STATE), hi.reshape(b, N_SSM_GROUPS, SSM_STATE))


def kernel(x_prompt, x_sample, cache_k, cache_v, cache_idx_k, state_ssm_re, state_ssm_im,
           c_prompt, c_sample, w_mod, b_mod, g_norm, w_in, lambda_re, lambda_im, log_dt,
           ssm_b_re, ssm_b_im, ssm_c_re, ssm_c_im, d_skip, w_glu, w_attn_proj, w_ssm_proj,
           w_out, g_final):
    depth = w_in.shape[0]
    assert depth == 1, "the final norm is fused into the (single) layer's output kernel"
    nbp = x_prompt.shape[0]
    past_len = cache_k.shape[2]

    def layer0(a):
        return a.reshape(a.shape[1:])

    mod = _modulation(jnp.concatenate([c_prompt, c_sample], axis=0), layer0(w_mod), layer0(b_mod))
    mod_p, mod_s = mod[:nbp], mod[nbp:]

    cut = _SEG_OFF["kiwi"][0] + IDX_DIM + N_IDX_HEADS
    w = layer0(w_in)
    col = lax.broadcasted_iota(jnp.int32, (1, IN_PACKED), 1)
    w_packed = jnp.where(col < cut, jnp.pad(w, ((0, 0), (0, _KIWI_PAD))),
                         jnp.where(col < cut + _KIWI_PAD, 0.0,
                                   jnp.pad(w, ((0, 0), (_KIWI_PAD, 0))))).astype(BF16)

    a2, bbd, cbd = _discretize(*map(layer0, (lambda_re, lambda_im, log_dt, ssm_b_re, ssm_b_im,
                                             ssm_c_re, ssm_c_im)))

    prm = (layer0(g_norm), w_packed, a2, bbd, cbd, layer0(d_skip), layer0(w_glu).astype(BF16),
           layer0(w_attn_proj).astype(BF16), layer0(w_ssm_proj).astype(BF16),
           layer0(w_out).astype(BF16), g_final)

    yp, kp, vp, kip, hrp, hip = _layer(x_prompt, mod_p, 0, None, prm)
    past = tuple(map(layer0, (cache_k, cache_v, cache_idx_k, state_ssm_re, state_ssm_im)))
    ys, ks, vs, kis, hrs, his = _layer(x_sample, mod_s, past_len, past, prm)

    def st(z):
        return z[None]

    return (yp, ys, st(kp), st(vp), st(kip), st(hrp), st(hip),
            st(ks), st(vs), st(kis), st(hrs), st(his))
```
